```python
import math
import jax, jax.numpy as jnp
from jax import lax
import numpy as np

D_MODEL = 1024
BATCH = 8
SEQ = 4096
DEPTH = 2

CHUNK = 64
Q_BLOCK = 128
A_HEADS = 8
A_HEAD_DIM = 64
A_WIDTH = A_HEADS * A_HEAD_DIM
B_WIDTH = 512
CONV_WIDTH = 3
C_HEADS = 4
C_KEY_DIM = 64
C_VAL_DIM = 128
C_KEY_WIDTH = C_HEADS * C_KEY_DIM
C_VAL_WIDTH = C_HEADS * C_VAL_DIM
C_LOWRANK = 16
C_GATE_TEMP = 16.0
N_BRANCHES = 3
D_FF = 2816
N_EXPERTS = 8
TOP_K = 2
N_DENSE = (DEPTH + 1) // 2
N_MOE = DEPTH // 2
EPS = 1e-6
IN_SIZES = (A_WIDTH, A_WIDTH, A_WIDTH, A_HEADS,
            B_WIDTH, B_WIDTH, B_WIDTH,
            C_KEY_WIDTH, C_KEY_WIDTH, C_VAL_WIDTH, C_LOWRANK, C_VAL_WIDTH)
IN_COLS = sum(IN_SIZES)

kernel_name = "hybrid_fox_conv_gla_moe_block"


def rms_norm(x, g):
    xf = x.astype(jnp.float32)
    y = xf * lax.rsqrt(jnp.mean(xf * xf, axis=-1, keepdims=True) + EPS)
    return (y * g.astype(jnp.float32)).astype(x.dtype)


def fox_attention(q, k, v, f_logit, b_f):
    B, S, H, Dh = q.shape
    nb = S // Q_BLOCK
    log_f = jax.nn.log_sigmoid(f_logit.astype(jnp.float32) + b_f.astype(jnp.float32))
    F = jnp.cumsum(log_f, axis=1).transpose(0, 2, 1)
    qh = q.transpose(0, 2, 1, 3)
    kh = k.transpose(0, 2, 1, 3)
    vh = v.transpose(0, 2, 1, 3)
    q_blocks = qh.reshape(B, H, nb, Q_BLOCK, Dh).transpose(2, 0, 1, 3, 4)
    F_blocks = F.reshape(B, H, nb, Q_BLOCK).transpose(2, 0, 1, 3)
    k_pos = jnp.arange(S)
    scale = 1.0 / math.sqrt(Dh)

    def one_block(args):
        qb, Fqb, blk = args
        s = jnp.einsum('bhqd,bhkd->bhqk', qb, kh).astype(jnp.float32) * scale
        s = s + (Fqb[..., :, None] - F[..., None, :])
        q_pos = blk * Q_BLOCK + jnp.arange(Q_BLOCK)
        s = jnp.where(k_pos[None, :] <= q_pos[:, None], s, -jnp.inf)
        p = jax.nn.softmax(s, axis=-1)
        return jnp.einsum('bhqk,bhkd->bhqd', p.astype(vh.dtype), vh)

    out = lax.map(one_block, (q_blocks, F_blocks, jnp.arange(nb)))
    return out.transpose(1, 0, 3, 2, 4).reshape(B, S, H * Dh)


def short_gated_conv(u, b_gate, c_gate, conv_w):
    S = u.shape[1]
    z = c_gate * u
    zp = jnp.pad(z, ((0, 0), (CONV_WIDTH - 1, 0), (0, 0)))
    conv = conv_w[0] * zp[:, 0:S] + conv_w[1] * zp[:, 1:S + 1] + conv_w[2] * zp[:, 2:S + 2]
    return b_gate * conv


def gla_chunked(q, k, v, a_lowrank, r, w_a2, b_a, g_gla):
    B, S, H, dk = q.shape
    dv = v.shape[-1]
    n = S // CHUNK
    log_a = jax.nn.log_sigmoid((a_lowrank @ w_a2 + b_a).astype(jnp.float32)) / C_GATE_TEMP
    log_a = log_a.reshape(B, S, H, dk)

    def chunks(t):
        return t.reshape(B, n, CHUNK, H, t.shape[-1]).transpose(0, 3, 1, 2, 4)

    qc = chunks(q).astype(jnp.float32) * (dk ** -0.5)
    kc = chunks(k).astype(jnp.float32)
    vc = chunks(v).astype(jnp.float32)
    bcum = jnp.cumsum(chunks(log_a), axis=3)
    b_last = bcum[..., -1, :]
    q_t = qc * jnp.exp(bcum)
    k_t = kc * jnp.exp(-bcum)
    tri = jnp.tril(jnp.ones((CHUNK, CHUNK), dtype=bool))
    A = jnp.where(tri, jnp.einsum('bhnqd,bhnkd->bhnqk', q_t, k_t), 0.0)
    o_intra = jnp.einsum('bhnqk,bhnke->bhnqe', A, vc)
    k_state = kc * jnp.exp(b_last[..., None, :] - bcum)
    chunk_kv = jnp.einsum('bhnkd,bhnke->bhnde', k_state, vc)
    decay = jnp.exp(b_last)

    def step(state, inp):
        dec, kv = inp
        return dec[..., None] * state + kv, state

    init = jnp.zeros((B, H, dk, dv), jnp.float32)
    _, states = lax.scan(step, init, (jnp.moveaxis(decay, 2, 0), jnp.moveaxis(chunk_kv, 2, 0)))
    states = jnp.moveaxis(states, 0, 2)
    o = o_intra + jnp.einsum('bhnqd,bhnde->bhnqe', q_t, states)
    o = o.transpose(0, 2, 3, 1, 4).reshape(B, S, H, dv)
    o = o * lax.rsqrt(jnp.mean(o * o, axis=-1, keepdims=True) + EPS)
    o = o * g_gla.astype(jnp.float32).reshape(H, dv)
    o = o.reshape(B, S, H * dv).astype(r.dtype)
    return o * jax.nn.silu(r)


def hybrid_mixer(h, w_in, b_f, conv_w, w_a2, b_a, g_gla, w_pa, w_pb, w_pc, w_gate, b_gate, w_o):
    B, S, _ = h.shape
    proj = h @ w_in
    split_points = np.cumsum(IN_SIZES)[:-1].tolist()
    (a_q, a_k, a_v, a_f, b_u, b_bg, b_cg,
     c_q, c_k, c_v, c_a, c_r) = jnp.split(proj, split_points, axis=-1)
    o_a = fox_attention(a_q.reshape(B, S, A_HEADS, A_HEAD_DIM),
                        a_k.reshape(B, S, A_HEADS, A_HEAD_DIM),
                        a_v.reshape(B, S, A_HEADS, A_HEAD_DIM), a_f, b_f)
    o_b = short_gated_conv(b_u, b_bg, b_cg, conv_w)
    o_c = gla_chunked(c_q.reshape(B, S, C_HEADS, C_KEY_DIM),
                      c_k.reshape(B, S, C_HEADS, C_KEY_DIM),
                      c_v.reshape(B, S, C_HEADS, C_VAL_DIM), c_a, c_r, w_a2, b_a, g_gla)
    g_a, g_b, g_c = jnp.split(jax.nn.sigmoid(h @ w_gate + b_gate), N_BRANCHES, axis=-1)
    merged = g_a * (o_a @ w_pa) + g_b * (o_b @ w_pb) + g_c * (o_c @ w_pc)
    return merged @ w_o


def swiglu(h, w1, w3, w2):
    return (jax.nn.silu(h @ w1) * (h @ w3)) @ w2


def moe_swiglu(h, w_router, w1, w3, w2):
    logits = (h @ w_router).astype(jnp.float32)
    top_vals, top_idx = lax.top_k(logits, TOP_K)
    top_w = jax.nn.softmax(top_vals, axis=-1)
    gate = jnp.sum(jax.nn.one_hot(top_idx, N_EXPERTS, dtype=jnp.float32) * top_w[..., None], axis=-2)
    gate = gate.astype(h.dtype)
    out = jnp.zeros_like(h)
    for e in range(N_EXPERTS):
        out = out + gate[..., e:e + 1] * swiglu(h, w1[e], w3[e], w2[e])
    return out


def setup_inputs(seed: int = 0) -> dict:
    key = jax.random.key(seed)
    ks = jax.random.split(key, 24)

    def nrm(k, shape, scale):
        return scale * jax.random.normal(k, shape, jnp.float32)

    D = D_MODEL
    return {
        "x": nrm(ks[0], (BATCH, SEQ, D), 1.0),
        "g_mix": 1.0 + nrm(ks[1], (DEPTH, D), 0.02),
        "w_in": nrm(ks[2], (DEPTH, D, IN_COLS), D ** -0.5),
        "b_f": 4.0 + nrm(ks[3], (DEPTH, A_HEADS), 0.1),
        "conv_w": nrm(ks[4], (DEPTH, CONV_WIDTH, B_WIDTH), CONV_WIDTH ** -0.5),
        "w_a2": nrm(ks[5], (DEPTH, C_LOWRANK, C_KEY_WIDTH), C_LOWRANK ** -0.5),
        "b_a": nrm(ks[6], (DEPTH, C_KEY_WIDTH), 0.1),
        "g_gla": 1.0 + nrm(ks[7], (DEPTH, C_VAL_WIDTH), 0.02),
        "w_pa": nrm(ks[8], (DEPTH, A_WIDTH, D), A_WIDTH ** -0.5),
        "w_pb": nrm(ks[9], (DEPTH, B_WIDTH, D), B_WIDTH ** -0.5),
        "w_pc": nrm(ks[10], (DEPTH, C_VAL_WIDTH, D), C_VAL_WIDTH ** -0.5),
        "w_gate": nrm(ks[11], (DEPTH, D, N_BRANCHES * D), D ** -0.5),
        "b_gate": nrm(ks[12], (DEPTH, N_BRANCHES * D), 0.1),
        "w_o": nrm(ks[13], (DEPTH, D, D), D ** -0.5),
        "g_ffn": 1.0 + nrm(ks[14], (DEPTH, D), 0.02),
        "ffn_w1": nrm(ks[15], (N_DENSE, D, D_FF), D ** -0.5),
        "ffn_w3": nrm(ks[16], (N_DENSE, D, D_FF), D ** -0.5),
        "ffn_w2": nrm(ks[17], (N_DENSE, D_FF, D), D_FF ** -0.5),
        "w_router": nrm(ks[18], (N_MOE, D, N_EXPERTS), D ** -0.5),
        "moe_w1": nrm(ks[19], (N_MOE, N_EXPERTS, D, D_FF), D ** -0.5),
        "moe_w3": nrm(ks[20], (N_MOE, N_EXPERTS, D, D_FF), D ** -0.5),
        "moe_w2": nrm(ks[21], (N_MOE, N_EXPERTS, D_FF, D), D_FF ** -0.5),
        "g_final": 1.0 + nrm(ks[22], (D,), 0.02),
    }


def reference(x, g_mix, w_in, b_f, conv_w, w_a2, b_a, g_gla, w_pa, w_pb, w_pc,
              w_gate, b_gate, w_o, g_ffn, ffn_w1, ffn_w3, ffn_w2,
              w_router, moe_w1, moe_w3, moe_w2, g_final):
    for l in range(DEPTH):
        h = rms_norm(x, g_mix[l])
        x = x + hybrid_mixer(h, w_in[l], b_f[l], conv_w[l], w_a2[l], b_a[l], g_gla[l],
                             w_pa[l], w_pb[l], w_pc[l], w_gate[l], b_gate[l], w_o[l])
        h = rms_norm(x, g_ffn[l])
        i = l // 2
        if l % 2 == 0:
            x = x + swiglu(h, ffn_w1[i], ffn_w3[i], ffn_w2[i])
        else:
            x = x + moe_swiglu(h, w_router[i], moe_w1[i], moe_w3[i], moe_w2[i])
    return rms_norm(x, g_final)
```

```python
import functools

import jax
import jax.numpy as jnp
from jax import lax
from jax.experimental import pallas as pl
from jax.experimental.pallas import tpu as pltpu

F32 = jnp.float32
BF16 = jnp.bfloat16
I32 = jnp.int32

EPS = 1e-6
NEG = -1e30

A_HEADS, A_HEAD_DIM = 8, 64
A_WIDTH = A_HEADS * A_HEAD_DIM
B_WIDTH = 512
C_HEADS, C_KEY_DIM, C_VAL_DIM = 4, 64, 128
C_KEY_WIDTH = C_HEADS * C_KEY_DIM
C_VAL_WIDTH = C_HEADS * C_VAL_DIM
C_LOWRANK = 16
C_GATE_TEMP = 16.0
GLA_CHUNK = 64
N_EXPERTS = 8
GROUP_W = 1536

LANES = 128
V7X_VMEM_BYTES = 64 * 1024 * 1024
VMEM_LIMIT = V7X_VMEM_BYTES - 8 * 1024 * 1024

TM = 512
TQ = 512
TL = 512
FF_CHUNK = 256
TMM = 512
TD = 256


def _cparams(sem):
    return pltpu.CompilerParams(dimension_semantics=sem, vmem_limit_bytes=VMEM_LIMIT)


def _dot(a, b):
    return jnp.dot(a, b, preferred_element_type=F32)


def _dot_nt(a, b):
    return lax.dot_general(a, b, (((1,), (1,)), ((), ())), preferred_element_type=F32)


def _rms(x, g):
    return x * lax.rsqrt(jnp.mean(x * x, axis=-1, keepdims=True) + EPS) * g


def _log_sigmoid(z):
    return jnp.minimum(z, 0.0) - jnp.log(1.0 + jnp.exp(-jnp.abs(z)))


def _sigmoid(z):
    return 1.0 / (1.0 + jnp.exp(-z))


def _split3(x):
    hi = x.astype(BF16)
    r1 = x - hi.astype(F32)
    mid = r1.astype(BF16)
    lo = (r1 - mid.astype(F32)).astype(BF16)
    return hi, mid, lo


def _const_spec(shape):
    return pl.BlockSpec(shape, lambda *_: (0,) * len(shape))


def _inproj_kernel(x_ref, g_ref, wbig_ref, wsmall_ref, qkv_ref, conv_ref, gla_ref, small_ref):
    h = _rms(x_ref[...], g_ref[...]).astype(BF16)
    for gi, ref in enumerate((qkv_ref, conv_ref, gla_ref)):
        for c in range(0, GROUP_W, 512):
            w = wbig_ref[:, gi * GROUP_W + c: gi * GROUP_W + c + 512]
            ref[:, c:c + 512] = _dot(h, w).astype(BF16)
    small_ref[...] = _dot(h, wsmall_ref[...])


def _inproj(x2, g, wbig, wsmall):
    t, d = x2.shape
    return pl.pallas_call(
        _inproj_kernel,
        grid=(t // TM,),
        in_specs=[pl.BlockSpec((TM, d), lambda i: (i, 0)),
                  _const_spec((1, d)),
                  _const_spec(wbig.shape),
                  _const_spec(wsmall.shape)],
        out_specs=[pl.BlockSpec((TM, GROUP_W), lambda i: (i, 0))] * 3
        + [pl.BlockSpec((TM, LANES), lambda i: (i, 0))],
        out_shape=[jax.ShapeDtypeStruct((t, GROUP_W), BF16)] * 3
        + [jax.ShapeDtypeStruct((t, LANES), F32)],
        compiler_params=_cparams(("parallel",)),
        name="inproj",
    )(x2, g, wbig, wsmall)


def _fcum_kernel(small_ref, bf_ref, tril_ref, f_ref, ft_ref, carry_ref):
    @pl.when(pl.program_id(1) == 0)
    def _():
        carry_ref[...] = jnp.zeros_like(carry_ref)

    lf = _log_sigmoid(small_ref[0] + bf_ref[...])
    hi, mid, lo = _split3(lf)
    tril = tril_ref[...]
    cs = _dot(tril, hi) + _dot(tril, mid) + _dot(tril, lo) + carry_ref[...]
    f_ref[0] = cs
    ft_ref[0] = cs.T[0:A_HEADS, :]
    carry_ref[...] = cs[TM - 1:TM, :]


def _fcum(small3, bf_pad, tril):
    b, s, _ = small3.shape
    return pl.pallas_call(
        _fcum_kernel,
        grid=(b, s // TM),
        in_specs=[pl.BlockSpec((1, TM, LANES), lambda bi, si: (bi, si, 0)),
                  _const_spec((1, LANES)),
                  _const_spec((TM, TM))],
        out_specs=[pl.BlockSpec((1, TM, LANES), lambda bi, si: (bi, si, 0)),
                   pl.BlockSpec((1, A_HEADS, TM), lambda bi, si: (bi, 0, si))],
        out_shape=[jax.ShapeDtypeStruct((b, s, LANES), F32),
                   jax.ShapeDtypeStruct((b, A_HEADS, s), F32)],
        scratch_shapes=[pltpu.VMEM((1, LANES), F32)],
        compiler_params=_cparams(("parallel", "arbitrary")),
        name="fcum",
    )(small3, bf_pad, tril)


def _attn_kernel(q_ref, k_ref, v_ref, f_ref, ft_ref, o_ref, m_ref, l_ref, acc_ref, fq_ref):
    p = pl.program_id(1)
    qi = pl.program_id(2)
    ki = pl.program_id(3)
    lane = lax.broadcasted_iota(I32, (1, LANES), 1)

    @pl.when(ki == 0)
    def _init():
        m_ref[...] = jnp.full(m_ref.shape, NEG, F32)
        l_ref[...] = jnp.zeros_like(l_ref)
        acc_ref[...] = jnp.zeros_like(acc_ref)
        fblk = f_ref[0]
        for j in range(2):
            col = jnp.sum(jnp.where(lane == 2 * p + j, fblk, 0.0), axis=-1, keepdims=True)
            fq_ref[j] = jnp.broadcast_to(col, (TQ, LANES))

    def _step(diagonal):
        q = q_ref[...]
        k = k_ref[...]
        v = v_ref[...]
        if diagonal:
            row = lax.broadcasted_iota(I32, (TQ, TQ), 0)
            col = lax.broadcasted_iota(I32, (TQ, TQ), 1)
            keep = col <= row
        for j in range(2):
            qj = jnp.where(lane // A_HEAD_DIM == j, q, jnp.zeros_like(q))
            fk = ft_ref[0, pl.ds(2 * p + j, 1), :]
            s = _dot_nt(qj, k) * (A_HEAD_DIM ** -0.5) + (fq_ref[j][:, 0:1] - fk)
            if diagonal:
                s = jnp.where(keep, s, NEG)
            m_prev = m_ref[j]
            m_new = jnp.maximum(m_prev, jnp.max(s, axis=-1, keepdims=True))
            alpha = jnp.exp(m_prev - m_new)
            pr = jnp.exp(s - m_new[:, 0:1])
            l_ref[j] = alpha * l_ref[j] + jnp.sum(pr, axis=-1, keepdims=True)
            acc_ref[j] = alpha * acc_ref[j] + _dot(pr.astype(BF16), v)
            m_ref[j] = m_new

    @pl.when(ki < qi)
    def _below():
        _step(False)

    @pl.when(ki == qi)
    def _diag():
        _step(True)
        o = jnp.where(lane < A_HEAD_DIM, acc_ref[0] / l_ref[0], acc_ref[1] / l_ref[1])
        o_ref[...] = o.astype(BF16)


def _attention(qkv, f_tm, f_t, b, s):
    nq = s // TQ
    npair = A_HEADS // 2
    kv_blk = lambda bi, qi, ki: bi * nq + jnp.minimum(ki, qi)
    return pl.pallas_call(
        _attn_kernel,
        grid=(b, npair, nq, nq),
        in_specs=[pl.BlockSpec((TQ, LANES), lambda bi, p, qi, ki: (bi * nq + qi, p)),
                  pl.BlockSpec((TQ, LANES), lambda bi, p, qi, ki: (kv_blk(bi, qi, ki), npair + p)),
                  pl.BlockSpec((TQ, LANES), lambda bi, p, qi, ki: (kv_blk(bi, qi, ki), 2 * npair + p)),
                  pl.BlockSpec((1, TQ, LANES), lambda bi, p, qi, ki: (bi, qi, 0)),
                  pl.BlockSpec((1, A_HEADS, TQ), lambda bi, p, qi, ki: (bi, 0, jnp.minimum(ki, qi)))],
        out_specs=pl.BlockSpec((TQ, LANES), lambda bi, p, qi, ki: (bi * nq + qi, p)),
        out_shape=jax.ShapeDtypeStruct((b * s, A_WIDTH), BF16),
        scratch_shapes=[pltpu.VMEM((2, TQ, LANES), F32)] * 4,
        compiler_params=_cparams(("parallel", "parallel", "parallel", "arbitrary")),
        name="fox_attention",
    )(qkv, qkv, qkv, f_tm, f_t)


def _mix_kernel(conv_ref, gla_ref, small_ref, convw_ref, wa2_ref, ba_ref, ggla_ref, tri_ref,
                ob_ref, oc_ref, zprev_ref, st_ref, sall_ref):
    @pl.when(pl.program_id(1) == 0)
    def _():
        zprev_ref[...] = jnp.zeros_like(zprev_ref)
        st_ref[...] = jnp.zeros_like(st_ref)

    nc = TL // GLA_CHUNK

    u = conv_ref[:, 0:B_WIDTH].astype(F32)
    bg = conv_ref[:, B_WIDTH:2 * B_WIDTH].astype(F32)
    cg = conv_ref[:, 2 * B_WIDTH:3 * B_WIDTH].astype(F32)
    z = cg * u
    row = lax.broadcasted_iota(I32, (TL, B_WIDTH), 0)
    zp = zprev_ref[...]
    z1 = jnp.where(row == 0, zp[7:8, :], pltpu.roll(z, 1, 0))
    z2 = jnp.where(row == 0, zp[6:7, :], jnp.where(row == 1, zp[7:8, :], pltpu.roll(z, 2, 0)))
    cw = convw_ref[...]
    ob_ref[...] = (bg * (cw[0:1, :] * z2 + cw[1:2, :] * z1 + cw[2:3, :] * z)).astype(BF16)
    zprev_ref[...] = z[TL - 8:TL, :]

    la = _dot(small_ref[...].astype(BF16), wa2_ref[...]) + ba_ref[...]
    la = _log_sigmoid(la) * (1.0 / C_GATE_TEMP)
    hi, mid, lo = _split3(la)
    tri = tri_ref[...]
    bcum = _dot(tri, hi) + _dot(tri, mid) + _dot(tri, lo)
    bl = bcum.reshape(nc, GLA_CHUNK, C_KEY_WIDTH)[:, GLA_CHUNK - 1:GLA_CHUNK, :]
    blast = jnp.broadcast_to(bl, (nc, GLA_CHUNK, C_KEY_WIDTH)).reshape(TL, C_KEY_WIDTH)

    q = gla_ref[:, 0:C_KEY_WIDTH].astype(F32)
    k = gla_ref[:, C_KEY_WIDTH:2 * C_KEY_WIDTH].astype(F32)
    q_t = q * (C_KEY_DIM ** -0.5) * jnp.exp(bcum)
    k_t = (k * jnp.exp(-bcum)).astype(BF16)
    ks_t = (k * jnp.exp(blast - bcum)).T
    dec_t = jnp.broadcast_to(jnp.exp(bl), (nc, LANES, C_KEY_WIDTH)).reshape(nc * LANES, C_KEY_WIDTH).T

    r_i = lax.broadcasted_iota(I32, (TL, TL), 0)
    c_i = lax.broadcasted_iota(I32, (TL, TL), 1)
    same = (r_i // GLA_CHUNK) == (c_i // GLA_CHUNK)
    intra = same & (c_i <= r_i)
    lane_k = lax.broadcasted_iota(I32, (1, C_KEY_WIDTH), 1)
    lane = lax.broadcasted_iota(I32, (1, LANES), 1)

    for h in range(C_HEADS):
        v_h = gla_ref[:, 2 * C_KEY_WIDTH + h * C_VAL_DIM: 2 * C_KEY_WIDTH + (h + 1) * C_VAL_DIM]
        q_h = jnp.where(lane_k // C_KEY_DIM == h, q_t, 0.0).astype(BF16)
        a = jnp.where(intra, _dot_nt(q_h, k_t), 0.0).astype(BF16)

        q2 = q_t[:, (h // 2) * LANES:(h // 2 + 1) * LANES]
        q2r = pltpu.roll(q2, C_KEY_DIM, 1)
        dup = jnp.where((lane < C_KEY_DIM) == (h % 2 == 0), q2, q2r)
        q_exp = jnp.where(same, jnp.concatenate([dup] * (TL // LANES), axis=1), 0.0).astype(BF16)

        ks_h = ks_t[h * C_KEY_DIM:(h + 1) * C_KEY_DIM, :]
        k_exp = jnp.where(same, jnp.concatenate([ks_h] * nc, axis=0), 0.0).astype(BF16)
        kv = _dot(k_exp, v_h)

        st = st_ref[h]
        for c in range(nc):
            sall_ref[h, c * GLA_CHUNK:(c + 1) * GLA_CHUNK, :] = st.astype(BF16)
            dec = dec_t[h * C_KEY_DIM:(h + 1) * C_KEY_DIM, c * LANES:(c + 1) * LANES]
            st = dec * st + kv[c * GLA_CHUNK:(c + 1) * GLA_CHUNK, :]
        st_ref[h] = st

        o = _dot(a, v_h) + _dot(q_exp, sall_ref[h])
        o = o * lax.rsqrt(jnp.mean(o * o, axis=-1, keepdims=True) + EPS)
        o = o * ggla_ref[:, h * C_VAL_DIM:(h + 1) * C_VAL_DIM]
        r = gla_ref[:, 4 * C_KEY_WIDTH + h * C_VAL_DIM: 4 * C_KEY_WIDTH + (h + 1) * C_VAL_DIM].astype(F32)
        oc_ref[:, h * C_VAL_DIM:(h + 1) * C_VAL_DIM] = (o * (r * _sigmoid(r))).astype(BF16)


def _mixers(conv_in, gla_in, small, convw8, wa2p, ba, ggla, tri_bd, b, s):
    ns = s // TL
    tok = lambda bi, si: (bi * ns + si, 0)
    return pl.pallas_call(
        _mix_kernel,
        grid=(b, ns),
        in_specs=[pl.BlockSpec((TL, GROUP_W), tok),
                  pl.BlockSpec((TL, GROUP_W), tok),
                  pl.BlockSpec((TL, LANES), tok),
                  _const_spec((8, B_WIDTH)),
                  _const_spec((LANES, C_KEY_WIDTH)),
                  _const_spec((1, C_KEY_WIDTH)),
                  _const_spec((1, C_VAL_WIDTH)),
                  _const_spec((TL, TL))],
        out_specs=[pl.BlockSpec((TL, B_WIDTH), tok), pl.BlockSpec((TL, C_VAL_WIDTH), tok)],
        out_shape=[jax.ShapeDtypeStruct((b * s, B_WIDTH), BF16),
                   jax.ShapeDtypeStruct((b * s, C_VAL_WIDTH), BF16)],
        scratch_shapes=[pltpu.VMEM((8, B_WIDTH), F32),
                        pltpu.VMEM((C_HEADS, C_KEY_DIM, C_VAL_DIM), F32),
                        pltpu.VMEM((C_HEADS, TL, C_VAL_DIM), BF16)],
        compiler_params=_cparams(("parallel", "arbitrary")),
        name="conv_gla",
    )(conv_in, gla_in, small, convw8, wa2p, ba, ggla, tri_bd)


def _merge_kernel(x_ref, g_ref, oa_ref, ob_ref, oc_ref, wg_ref, bgate_ref, wpa_ref, wpb_ref,
                  wpc_ref, wo_ref, out_ref):
    x = x_ref[...]
    d = x.shape[1]
    h = _rms(x, g_ref[...]).astype(BF16)
    merged = None
    for j, (o_ref, wp_ref) in enumerate(((oa_ref, wpa_ref), (ob_ref, wpb_ref), (oc_ref, wpc_ref))):
        gate = _sigmoid(_dot(h, wg_ref[:, j * d:(j + 1) * d]) + bgate_ref[:, j * d:(j + 1) * d])
        term = gate * _dot(o_ref[...], wp_ref[...])
        merged = term if merged is None else merged + term
    out_ref[...] = x + _dot(merged.astype(BF16), wo_ref[...])


def _merge(x2, g, oa, ob, oc, wg, bgate, wpa, wpb, wpc, wo):
    t, d = x2.shape
    row = lambda w: pl.BlockSpec((TM, w), lambda i: (i, 0))
    return pl.pallas_call(
        _merge_kernel,
        grid=(t // TM,),
        in_specs=[row(d), _const_spec((1, d)), row(A_WIDTH), row(B_WIDTH), row(C_VAL_WIDTH),
                  _const_spec(wg.shape), _const_spec(bgate.shape), _const_spec(wpa.shape),
                  _const_spec(wpb.shape), _const_spec(wpc.shape), _const_spec(wo.shape)],
        out_specs=row(d),
        out_shape=jax.ShapeDtypeStruct((t, d), F32),
        compiler_params=_cparams(("parallel",)),
        name="merge_outproj",
    )(x2, g, oa, ob, oc, wg, bgate, wpa, wpb, wpc, wo)


def _swiglu_tile(h, w1_ref, w3_ref, w2_ref):
    ff = w1_ref.shape[-1]
    acc = None
    for c in range(0, ff, FF_CHUNK):
        a = _dot(h, w1_ref[:, c:c + FF_CHUNK])
        b = _dot(h, w3_ref[:, c:c + FF_CHUNK])
        g = (a * _sigmoid(a) * b).astype(BF16)
        part = _dot(g, w2_ref[c:c + FF_CHUNK, :])
        acc = part if acc is None else acc + part
    return acc


def _ffn_kernel(x_ref, g_ref, w1_ref, w3_ref, w2_ref, out_ref):
    x = x_ref[...]
    h = _rms(x, g_ref[...]).astype(BF16)
    out_ref[...] = x + _swiglu_tile(h, w1_ref, w3_ref, w2_ref)


def _dense_ffn(x2, g, w1, w3, w2):
    t, d = x2.shape
    row = pl.BlockSpec((TM, d), lambda i: (i, 0))
    return pl.pallas_call(
        _ffn_kernel,
        grid=(t // TM,),
        in_specs=[row, _const_spec((1, d)), _const_spec(w1.shape), _const_spec(w3.shape),
                  _const_spec(w2.shape)],
        out_specs=row,
        out_shape=jax.ShapeDtypeStruct((t, d), F32),
        compiler_params=_cparams(("parallel",)),
        name="dense_ffn",
    )(x2, g, w1, w3, w2)


def _router_kernel(x_ref, g_ref, wr_ref, stril_ref, h_ref, ri_ref, rw_ref, cnt_ref, carry_ref):
    @pl.when(pl.program_id(0) == 0)
    def _():
        carry_ref[...] = jnp.zeros_like(carry_ref)

    h = _rms(x_ref[...], g_ref[...])
    h_ref[...] = h
    h_hi = h.astype(BF16)
    h_lo = (h - h_hi.astype(F32)).astype(BF16)
    wr = wr_ref[...]
    w_hi = wr.astype(BF16)
    w_lo = (wr - w_hi.astype(F32)).astype(BF16)
    logits = _dot(h_hi, w_hi) + _dot(h_lo, w_hi) + _dot(h_hi, w_lo)

    lane = lax.broadcasted_iota(I32, (TM, LANES), 1)
    lane_f = lane.astype(F32)
    lg = jnp.where(lane < N_EXPERTS, logits, NEG)
    m1 = jnp.max(lg, axis=-1, keepdims=True)
    i1 = jnp.min(jnp.where(lg == m1, lane_f, float(LANES)), axis=-1, keepdims=True)
    oh1 = lane_f == i1
    lg2 = jnp.where(oh1, NEG, lg)
    m2 = jnp.max(lg2, axis=-1, keepdims=True)
    i2 = jnp.min(jnp.where(lg2 == m2, lane_f, float(LANES)), axis=-1, keepdims=True)
    oh2 = lane_f == i2
    e = jnp.exp(m2 - m1)
    w1 = 1.0 / (1.0 + e)
    w2 = e / (1.0 + e)

    sel = jnp.where(oh1 | oh2, 1.0, 0.0)
    before = _dot(stril_ref[...], sel.astype(BF16)) + carry_ref[...]
    r1 = jnp.sum(jnp.where(oh1, before, 0.0), axis=-1, keepdims=True)
    r2 = jnp.sum(jnp.where(oh2, before, 0.0), axis=-1, keepdims=True)
    total = carry_ref[...] + jnp.sum(sel, axis=0, keepdims=True)
    carry_ref[...] = total
    cnt_ref[...] = jnp.broadcast_to(total, cnt_ref.shape)

    ri = jnp.where(lane == 0, i1, jnp.where(lane == 1, i2, jnp.where(lane == 2, r1, jnp.where(lane == 3, r2, 0.0))))
    ri_ref[...] = ri.astype(I32)
    rw_ref[...] = jnp.where(lane == 0, w1, jnp.where(lane == 1, w2, 0.0))


def _router(x2, g, wr_pad, stril):
    t, d = x2.shape
    row = lambda w: pl.BlockSpec((TM, w), lambda i: (i, 0))
    return pl.pallas_call(
        _router_kernel,
        grid=(t // TM,),
        in_specs=[row(d), _const_spec((1, d)), _const_spec(wr_pad.shape), _const_spec((TM, TM))],
        out_specs=[row(d), row(LANES), row(LANES), _const_spec((8, LANES))],
        out_shape=[jax.ShapeDtypeStruct((t, d), F32), jax.ShapeDtypeStruct((t, LANES), I32),
                   jax.ShapeDtypeStruct((t, LANES), F32), jax.ShapeDtypeStruct((8, LANES), F32)],
        scratch_shapes=[pltpu.VMEM((1, LANES), F32)],
        compiler_params=_cparams(("arbitrary",)),
        name="moe_router",
    )(x2, g, wr_pad, stril)


def _row_copy(src_ref, src_row, dst_ref, dst_row, sem):
    return pltpu.make_async_copy(src_ref.at[pl.ds(src_row, 1)], dst_ref.at[pl.ds(dst_row, 1)], sem)


def _dispatch_kernel(slot_ref, h_ref, xs_in_ref, xs_ref, sem):
    del xs_in_ref

    def start(i, carry):
        _row_copy(h_ref, i // 2, xs_ref, slot_ref[i], sem).start()
        return carry

    lax.fori_loop(0, 2 * TD, start, 0)

    def wait(i, carry):
        _row_copy(h_ref, 0, xs_ref, 0, sem).wait()
        return carry

    lax.fori_loop(0, 2 * TD, wait, 0)


def _dispatch(slot_flat, h2, xs_zero):
    t, d = h2.shape
    return pl.pallas_call(
        _dispatch_kernel,
        grid=(t // TD,),
        in_specs=[pl.BlockSpec((2 * TD,), lambda i: (i,), memory_space=pltpu.SMEM),
                  pl.BlockSpec((TD, d), lambda i: (i, 0)),
                  pl.BlockSpec(memory_space=pl.ANY)],
        out_specs=pl.BlockSpec(memory_space=pl.ANY),
        out_shape=jax.ShapeDtypeStruct(xs_zero.shape, xs_zero.dtype),
        scratch_shapes=[pltpu.SemaphoreType.DMA(())],
        input_output_aliases={2: 0},
        compiler_params=_cparams(("arbitrary",)),
        name="moe_dispatch",
    )(slot_flat, h2, xs_zero)


def _group_ffn_kernel(te_ref, ta_ref, xs_ref, w1_ref, w3_ref, w2_ref, y_ref):
    del te_ref
    active = ta_ref[pl.program_id(0)] == 1

    @pl.when(active)
    def _():
        y_ref[...] = _swiglu_tile(xs_ref[...].astype(BF16), w1_ref.at[0], w3_ref.at[0], w2_ref.at[0])

    @pl.when(jnp.logical_not(active))
    def _():
        y_ref[...] = jnp.zeros_like(y_ref)


def _group_ffn(tile_expert, tile_active, xs, w1, w3, w2):
    ns, d = xs.shape
    ff = w1.shape[-1]
    row = pl.BlockSpec((TMM, d), lambda i, te, ta: (i, 0))
    grid_spec = pltpu.PrefetchScalarGridSpec(
        num_scalar_prefetch=2,
        grid=(ns // TMM,),
        in_specs=[row,
                  pl.BlockSpec((1, d, ff), lambda i, te, ta: (te[i], 0, 0)),
                  pl.BlockSpec((1, d, ff), lambda i, te, ta: (te[i], 0, 0)),
                  pl.BlockSpec((1, ff, d), lambda i, te, ta: (te[i], 0, 0))],
        out_specs=row,
    )
    return pl.pallas_call(
        _group_ffn_kernel,
        grid_spec=grid_spec,
        out_shape=jax.ShapeDtypeStruct((ns, d), F32),
        compiler_params=_cparams(("arbitrary",)),
        name="moe_group_ffn",
    )(tile_expert, tile_active, xs, w1, w3, w2)


def _combine_kernel(slot_ref, x_ref, rw_ref, gf_ref, y_ref, out_ref, ybuf_ref, sem):
    def start(i, carry):
        _row_copy(y_ref, slot_ref[i], ybuf_ref, (i % 2) * TD + i // 2, sem).start()
        return carry

    lax.fori_loop(0, 2 * TD, start, 0)

    def wait(i, carry):
        _row_copy(y_ref, 0, ybuf_ref, 0, sem).wait()
        return carry

    lax.fori_loop(0, 2 * TD, wait, 0)

    rw = rw_ref[...]
    x = x_ref[...] + rw[:, 0:1] * ybuf_ref[0:TD, :] + rw[:, 1:2] * ybuf_ref[TD:2 * TD, :]
    out_ref[...] = _rms(x, gf_ref[...])


def _combine(slot_flat, x2, rw, g_final, y):
    t, d = x2.shape
    return pl.pallas_call(
        _combine_kernel,
        grid=(t // TD,),
        in_specs=[pl.BlockSpec((2 * TD,), lambda i: (i,), memory_space=pltpu.SMEM),
                  pl.BlockSpec((TD, d), lambda i: (i, 0)),
                  pl.BlockSpec((TD, LANES), lambda i: (i, 0)),
                  _const_spec((1, d)),
                  pl.BlockSpec(memory_space=pl.ANY)],
        out_specs=pl.BlockSpec((TD, d), lambda i: (i, 0)),
        out_shape=jax.ShapeDtypeStruct((t, d), F32),
        scratch_shapes=[pltpu.VMEM((2 * TD, d), F32), pltpu.SemaphoreType.DMA(())],
        compiler_params=_cparams(("arbitrary",)),
        name="moe_combine",
    )(slot_flat, x2, rw, g_final, y)


def _norm_kernel(x_ref, g_ref, out_ref):
    out_ref[...] = _rms(x_ref[...], g_ref[...])


def _final_norm(x2, g):
    t, d = x2.shape
    row = pl.BlockSpec((TM, d), lambda i: (i, 0))
    return pl.pallas_call(
        _norm_kernel, grid=(t // TM,), in_specs=[row, _const_spec((1, d))], out_specs=row,
        out_shape=jax.ShapeDtypeStruct((t, d), F32), compiler_params=_cparams(("parallel",)),
        name="final_norm",
    )(x2, g)


def _tril_blocks(n, block, strict=False):
    r = jnp.arange(n)[:, None]
    c = jnp.arange(n)[None, :]
    keep = (c < r) if strict else (c <= r)
    keep = keep & ((r // block) == (c // block))
    return keep.astype(BF16)


def _moe_layer(x2, g_ffn, w_router, w1, w3, w2, g_final):
    t, d = x2.shape
    wr_pad = jnp.pad(w_router, ((0, 0), (0, LANES - N_EXPERTS)))
    h2, ri, rw, cnt = _router(x2, g_ffn.reshape(1, d), wr_pad, _tril_blocks(TM, TM, strict=True))

    counts = cnt[0, :N_EXPERTS].astype(I32)
    padded = ((counts + TMM - 1) // TMM) * TMM
    ends = jnp.cumsum(padded)
    starts = ends - padded
    slot_flat = (starts[ri[:, 0:2]] + ri[:, 2:4]).reshape(2 * t)
    n_tiles = (2 * t) // TMM + N_EXPERTS
    tile_start = jnp.arange(n_tiles, dtype=I32) * TMM
    tile_expert = jnp.minimum(jnp.searchsorted(ends, tile_start, side="right"), N_EXPERTS - 1).astype(I32)
    tile_active = (tile_start < ends[-1]).astype(I32)

    xs = _dispatch(slot_flat, h2, jnp.zeros((n_tiles * TMM, d), F32))
    y = _group_ffn(tile_expert, tile_active, xs, w1.astype(BF16), w3.astype(BF16), w2.astype(BF16))
    return _combine(slot_flat, x2, rw, g_final.reshape(1, d), y)


def kernel(x, g_mix, w_in, b_f, conv_w, w_a2, b_a, g_gla, w_pa, w_pb, w_pc, w_gate, b_gate, w_o,
           g_ffn, ffn_w1, ffn_w3, ffn_w2, w_router, moe_w1, moe_w3, moe_w2, g_final):
    b, s, d = x.shape
    depth = g_mix.shape[0]
    t = b * s
    assert t % TM == 0 and s % TQ == 0 and s % TL == 0 and t % TD == 0 and TL % GLA_CHUNK == 0
    x2 = x.reshape(t, d)
    tril_full = _tril_blocks(TM, TM)
    tril_chunk = _tril_blocks(TL, GLA_CHUNK)
    o_af = 3 * A_WIDTH
    o_b = o_af + A_HEADS
    o_c = o_b + 3 * B_WIDTH
    o_ca = o_c + 2 * C_KEY_WIDTH + C_VAL_WIDTH
    o_cr = o_ca + C_LOWRANK

    done_final = False
    for l in range(depth):
        w = w_in[l]
        wbig = jnp.concatenate([w[:, 0:o_af], w[:, o_b:o_c], w[:, o_c:o_ca], w[:, o_cr:]], axis=1).astype(BF16)
        wsmall = jnp.concatenate([w[:, o_af:o_b], w[:, o_ca:o_cr],
                                  jnp.zeros((d, LANES - A_HEADS - C_LOWRANK), F32)], axis=1).astype(BF16)
        gm = g_mix[l].reshape(1, d)
        qkv, conv_in, gla_in, small = _inproj(x2, gm, wbig, wsmall)

        bf_pad = jnp.pad(b_f[l], (0, LANES - A_HEADS)).reshape(1, LANES)
        f_tm, f_t = _fcum(small.reshape(b, s, LANES), bf_pad, tril_full)
        o_a = _attention(qkv, f_tm, f_t, b, s)

        convw8 = jnp.pad(conv_w[l], ((0, 8 - conv_w.shape[1]), (0, 0)))
        wa2p = jnp.zeros((LANES, C_KEY_WIDTH), F32).at[A_HEADS:A_HEADS + C_LOWRANK].set(w_a2[l]).astype(BF16)
        o_bm, o_cm = _mixers(conv_in, gla_in, small, convw8, wa2p, b_a[l].reshape(1, -1),
                             g_gla[l].reshape(1, -1), tril_chunk, b, s)

        x2 = _merge(x2, gm, o_a, o_bm, o_cm, w_gate[l].astype(BF16), b_gate[l].reshape(1, -1),
                    w_pa[l].astype(BF16), w_pb[l].astype(BF16), w_pc[l].astype(BF16), w_o[l].astype(BF16))

        i = l // 2
        if l % 2 == 0:
            x2 = _dense_ffn(x2, g_ffn[l].reshape(1, d), ffn_w1[i].astype(BF16), ffn_w3[i].astype(BF16),
                            ffn_w2[i].astype(BF16))
        elif l == depth - 1:
            x2 = _moe_layer(x2, g_ffn[l], w_router[i], moe_w1[i], moe_w3[i], moe_w2[i], g_final)
            done_final = True
        else:
            raise NotImplementedError("MoE layers are only supported as the last layer")
    if not done_final:
        x2 = _final_norm(x2, g_final.reshape(1, d))
    return x2.reshape(b, s, d)
```

```python
import functools
import math

import numpy as np
import jax
import jax.numpy as jnp
from jax import lax
from jax.experimental import pallas as pl
from jax.experimental.pallas import tpu as pltpu

F32 = jnp.float32
BF16 = jnp.bfloat16
I32 = jnp.int32

EPS = 1e-6
NEG = -1e30
LOG2E = math.log2(math.e)

A_HEADS, A_HEAD_DIM = 8, 64
A_WIDTH = A_HEADS * A_HEAD_DIM
B_WIDTH = 512
C_HEADS, C_KEY_DIM, C_VAL_DIM = 4, 64, 128
C_KEY_WIDTH = C_HEADS * C_KEY_DIM
C_VAL_WIDTH = C_HEADS * C_VAL_DIM
C_LOWRANK = 16
C_GATE_TEMP = 16.0
GLA_CHUNK = 64
N_EXPERTS = 8
GROUP_W = 1536

LANES = 128
V7X_VMEM_BYTES = 64 * 1024 * 1024
VMEM_LIMIT = V7X_VMEM_BYTES - 8 * 1024 * 1024

TM = 512
TQ = 512
ATT_HEADS = 4
TL = 512
FF_CHUNK = 256
TMM = 512
ROW_GRAN = 16
SORT_ROWS = 2 * TM + N_EXPERTS * ROW_GRAN


def _cparams(sem):
    return pltpu.CompilerParams(dimension_semantics=sem, vmem_limit_bytes=VMEM_LIMIT)


def _dot(a, b):
    return jnp.dot(a, b, preferred_element_type=F32)


def _dot_nt(a, b):
    return lax.dot_general(a, b, (((1,), (1,)), ((), ())), preferred_element_type=F32)


def _rms(x, g):
    return x * lax.rsqrt(jnp.mean(x * x, axis=-1, keepdims=True) + EPS) * g


def _log_sigmoid(z):
    return jnp.minimum(z, 0.0) - jnp.log(1.0 + jnp.exp(-jnp.abs(z)))


def _sigmoid(z):
    return 1.0 / (1.0 + jnp.exp(-z))


def _split3(x):
    hi = x.astype(BF16)
    r1 = x - hi.astype(F32)
    mid = r1.astype(BF16)
    lo = (r1 - mid.astype(F32)).astype(BF16)
    return hi, mid, lo


def _const_spec(shape):
    return pl.BlockSpec(shape, lambda *_: (0,) * len(shape))


def _inproj_kernel(x_ref, g_ref, wbig_ref, wvt_ref, wsmall_ref, qk_ref, vt_ref, conv_ref, gla_ref, small_ref):
    h = _rms(x_ref[...], g_ref[...]).astype(BF16)
    off = 0
    for ref in (qk_ref, conv_ref, gla_ref):
        width = ref.shape[1]
        for c in range(0, width, 512):
            ref[:, c:c + 512] = _dot(h, wbig_ref[:, off + c: off + c + 512]).astype(BF16)
        off += width
    vt_ref[...] = _dot_nt(wvt_ref[...], h).astype(BF16)
    small_ref[...] = _dot(h, wsmall_ref[...])


def _inproj(x2, g, wbig, wvt, wsmall):
    t, d = x2.shape
    row = lambda w: pl.BlockSpec((TM, w), lambda i: (i, 0))
    return pl.pallas_call(
        _inproj_kernel,
        grid=(t // TM,),
        in_specs=[row(d), _const_spec((1, d)), _const_spec(wbig.shape), _const_spec(wvt.shape),
                  _const_spec(wsmall.shape)],
        out_specs=[row(2 * A_WIDTH), pl.BlockSpec((A_WIDTH, TM), lambda i: (0, i)), row(GROUP_W), row(GROUP_W),
                   row(LANES)],
        out_shape=[jax.ShapeDtypeStruct((t, 2 * A_WIDTH), BF16), jax.ShapeDtypeStruct((A_WIDTH, t), BF16),
                   jax.ShapeDtypeStruct((t, GROUP_W), BF16), jax.ShapeDtypeStruct((t, GROUP_W), BF16),
                   jax.ShapeDtypeStruct((t, LANES), F32)],
        compiler_params=_cparams(("parallel",)),
        name="inproj",
    )(x2, g, wbig, wvt, wsmall)


def _fcum_kernel(small_ref, bf_ref, tril_ref, place_ref, qk_ref, qp_ref, kp_ref, carry_ref):
    @pl.when(pl.program_id(1) == 0)
    def _():
        carry_ref[...] = jnp.zeros_like(carry_ref)

    lf = LOG2E * _log_sigmoid(small_ref[0] + bf_ref[...])
    hi, mid, lo = _split3(lf)
    tril = tril_ref[...]
    cs = _dot(tril, hi) + _dot(tril, mid) + _dot(tril, lo) + carry_ref[...]
    carry_ref[...] = cs[TM - 1:TM, :]
    fterms = _dot(jnp.concatenate(_split3(cs), axis=1), place_ref[...])

    lane = lax.broadcasted_iota(I32, (1, LANES), 1)
    is_x = lane < A_HEAD_DIM
    is_a = (lane >= A_HEAD_DIM) & (lane < A_HEAD_DIM + 3)
    is_b = (lane >= A_HEAD_DIM + 3) & (lane < A_HEAD_DIM + 6)
    for p in range(A_HEADS // 2):
        q2 = qk_ref[:, p * LANES:(p + 1) * LANES].astype(F32)
        k2 = qk_ref[:, A_WIDTH + p * LANES: A_WIDTH + (p + 1) * LANES].astype(F32)
        halves = ((q2, k2), (pltpu.roll(q2, A_HEAD_DIM, 1), pltpu.roll(k2, A_HEAD_DIM, 1)))
        for j in range(2):
            h = 2 * p + j
            fh = fterms[:, h * LANES:(h + 1) * LANES]
            qh, kh = halves[j]
            qp_ref[:, h * LANES:(h + 1) * LANES] = jnp.where(
                is_x, qh, jnp.where(is_a, fh, jnp.where(is_b, 1.0, 0.0))).astype(BF16)
            kp_ref[:, h * LANES:(h + 1) * LANES] = jnp.where(
                is_x, kh, jnp.where(is_a, 1.0, jnp.where(is_b, -fh, 0.0))).astype(BF16)


def _aug_placement():
    place = np.zeros((3, LANES, A_HEADS * LANES), np.float32)
    for h in range(A_HEADS):
        for c in range(3):
            place[c, h, h * LANES + A_HEAD_DIM + c] = 1.0
            place[c, h, h * LANES + A_HEAD_DIM + 3 + c] = 1.0
    return jnp.asarray(place.reshape(3 * LANES, A_HEADS * LANES), BF16)


def _fcum(small3, bf_pad, tril, qk):
    b, s, _ = small3.shape
    ns = s // TM
    place = _aug_placement()
    tok = lambda bi, si: (bi * ns + si, 0)
    wide = pl.BlockSpec((TM, A_HEADS * LANES), tok)
    return pl.pallas_call(
        _fcum_kernel,
        grid=(b, ns),
        in_specs=[pl.BlockSpec((1, TM, LANES), lambda bi, si: (bi, si, 0)),
                  _const_spec((1, LANES)),
                  _const_spec((TM, TM)),
                  _const_spec(place.shape),
                  pl.BlockSpec((TM, 2 * A_WIDTH), tok)],
        out_specs=[wide, wide],
        out_shape=[jax.ShapeDtypeStruct((b * s, A_HEADS * LANES), BF16)] * 2,
        scratch_shapes=[pltpu.VMEM((1, LANES), F32)],
        compiler_params=_cparams(("parallel", "arbitrary")),
        name="fcum",
    )(small3, bf_pad, tril, place, qk)


def _attn_kernel(q_ref, k_ref, vt_ref, o_ref, m_ref, l_ref, acc_ref):
    qi = pl.program_id(2)
    ki = pl.program_id(3)

    @pl.when(ki == 0)
    def _init():
        m_ref[...] = jnp.full(m_ref.shape, NEG, F32)
        l_ref[...] = jnp.zeros_like(l_ref)
        acc_ref[...] = jnp.zeros_like(acc_ref)

    def _step(diagonal):
        if diagonal:
            key = lax.broadcasted_iota(I32, (TQ, TQ), 0)
            qry = lax.broadcasted_iota(I32, (TQ, TQ), 1)
            keep = key <= qry
        scores = [_dot_nt(k_ref[:, j * LANES:(j + 1) * LANES], q_ref[:, j * LANES:(j + 1) * LANES])
                  for j in range(ATT_HEADS)]
        for j in range(ATT_HEADS):
            st = scores[j]
            if diagonal:
                st = jnp.where(keep, st, NEG)
            m_prev = m_ref[j]
            m_new = jnp.maximum(m_prev, jnp.max(st, axis=0, keepdims=True))
            alpha = jnp.exp2(m_prev - m_new)
            pr = jnp.exp2(st - m_new)
            l_ref[j] = alpha * l_ref[j] + jnp.sum(pr, axis=0, keepdims=True)
            vt_j = vt_ref[j * A_HEAD_DIM:(j + 1) * A_HEAD_DIM, :]
            acc_ref[j] = alpha * acc_ref[j] + _dot(vt_j, pr.astype(BF16))
            m_ref[j] = m_new

    @pl.when(ki < qi)
    def _below():
        _step(False)

    @pl.when(ki == qi)
    def _diag():
        _step(True)
        ot = jnp.concatenate([acc_ref[j] / l_ref[j] for j in range(ATT_HEADS)], axis=0)
        o_ref[...] = ot.T.astype(BF16)


def _attention(qp, kp, vt, b, s):
    nq = s // TQ
    ng = A_HEADS // ATT_HEADS
    kv_blk = lambda bi, qi, ki: bi * nq + jnp.minimum(ki, qi)
    return pl.pallas_call(
        _attn_kernel,
        grid=(b, ng, nq, nq),
        in_specs=[pl.BlockSpec((TQ, ATT_HEADS * LANES), lambda bi, p, qi, ki: (bi * nq + qi, p)),
                  pl.BlockSpec((TQ, ATT_HEADS * LANES), lambda bi, p, qi, ki: (kv_blk(bi, qi, ki), p)),
                  pl.BlockSpec((ATT_HEADS * A_HEAD_DIM, TQ), lambda bi, p, qi, ki: (p, kv_blk(bi, qi, ki)))],
        out_specs=pl.BlockSpec((TQ, ATT_HEADS * A_HEAD_DIM), lambda bi, p, qi, ki: (bi * nq + qi, p)),
        out_shape=jax.ShapeDtypeStruct((b * s, A_WIDTH), BF16),
        scratch_shapes=[pltpu.VMEM((ATT_HEADS, 1, TQ), F32), pltpu.VMEM((ATT_HEADS, 1, TQ), F32),
                        pltpu.VMEM((ATT_HEADS, A_HEAD_DIM, TQ), F32)],
        compiler_params=_cparams(("parallel", "parallel", "parallel", "arbitrary")),
        name="fox_attention",
    )(qp, kp, vt)


def _mix_kernel(conv_ref, gla_ref, small_ref, convw_ref, wa2_ref, ba_ref, ggla_ref, tri_ref,
                ob_ref, oc_ref, zprev_ref, st_ref, sall_ref):
    @pl.when(pl.program_id(1) == 0)
    def _():
        zprev_ref[...] = jnp.zeros_like(zprev_ref)
        st_ref[...] = jnp.zeros_like(st_ref)

    nc = TL // GLA_CHUNK

    u = conv_ref[:, 0:B_WIDTH].astype(F32)
    bg = conv_ref[:, B_WIDTH:2 * B_WIDTH].astype(F32)
    cg = conv_ref[:, 2 * B_WIDTH:3 * B_WIDTH].astype(F32)
    z = cg * u
    row = lax.broadcasted_iota(I32, (TL, B_WIDTH), 0)
    zp = zprev_ref[...]
    z1 = jnp.where(row == 0, zp[7:8, :], pltpu.roll(z, 1, 0))
    z2 = jnp.where(row == 0, zp[6:7, :], jnp.where(row == 1, zp[7:8, :], pltpu.roll(z, 2, 0)))
    cw = convw_ref[...]
    ob_ref[...] = (bg * (cw[0:1, :] * z2 + cw[1:2, :] * z1 + cw[2:3, :] * z)).astype(BF16)
    zprev_ref[...] = z[TL - 8:TL, :]

    la = _dot(small_ref[...].astype(BF16), wa2_ref[...]) + ba_ref[...]
    la = _log_sigmoid(la) * (1.0 / C_GATE_TEMP)
    hi, mid, lo = _split3(la)
    tri = tri_ref[...]
    bcum = _dot(tri, hi) + _dot(tri, mid) + _dot(tri, lo)
    bl = bcum.reshape(nc, GLA_CHUNK, C_KEY_WIDTH)[:, GLA_CHUNK - 1:GLA_CHUNK, :]
    blast = jnp.broadcast_to(bl, (nc, GLA_CHUNK, C_KEY_WIDTH)).reshape(TL, C_KEY_WIDTH)

    q = gla_ref[:, 0:C_KEY_WIDTH].astype(F32)
    k = gla_ref[:, C_KEY_WIDTH:2 * C_KEY_WIDTH].astype(F32)
    q_t = q * (C_KEY_DIM ** -0.5) * jnp.exp(bcum)
    k_t = (k * jnp.exp(-bcum)).astype(BF16)
    ks_t = (k * jnp.exp(blast - bcum)).T
    dec_t = jnp.broadcast_to(jnp.exp(bl), (nc, LANES, C_KEY_WIDTH)).reshape(nc * LANES, C_KEY_WIDTH).T

    r_i = lax.broadcasted_iota(I32, (TL, TL), 0)
    c_i = lax.broadcasted_iota(I32, (TL, TL), 1)
    same = (r_i // GLA_CHUNK) == (c_i // GLA_CHUNK)
    intra = same & (c_i <= r_i)
    lane_k = lax.broadcasted_iota(I32, (1, C_KEY_WIDTH), 1)
    lane = lax.broadcasted_iota(I32, (1, LANES), 1)

    for h in range(C_HEADS):
        v_h = gla_ref[:, 2 * C_KEY_WIDTH + h * C_VAL_DIM: 2 * C_KEY_WIDTH + (h + 1) * C_VAL_DIM]
        q_h = jnp.where(lane_k // C_KEY_DIM == h, q_t, 0.0).astype(BF16)
        a = jnp.where(intra, _dot_nt(q_h, k_t), 0.0).astype(BF16)

        q2 = q_t[:, (h // 2) * LANES:(h // 2 + 1) * LANES]
        q2r = pltpu.roll(q2, C_KEY_DIM, 1)
        dup = jnp.where((lane < C_KEY_DIM) == (h % 2 == 0), q2, q2r)
        q_exp = jnp.where(same, jnp.concatenate([dup] * (TL // LANES), axis=1), 0.0).astype(BF16)

        ks_h = ks_t[h * C_KEY_DIM:(h + 1) * C_KEY_DIM, :]
        k_exp = jnp.where(same, jnp.concatenate([ks_h] * nc, axis=0), 0.0).astype(BF16)
        kv = _dot(k_exp, v_h)

        st = st_ref[h]
        for c in range(nc):
            sall_ref[h, c * GLA_CHUNK:(c + 1) * GLA_CHUNK, :] = st.astype(BF16)
            dec = dec_t[h * C_KEY_DIM:(h + 1) * C_KEY_DIM, c * LANES:(c + 1) * LANES]
            st = dec * st + kv[c * GLA_CHUNK:(c + 1) * GLA_CHUNK, :]
        st_ref[h] = st

        o = _dot(a, v_h) + _dot(q_exp, sall_ref[h])
        o = o * lax.rsqrt(jnp.mean(o * o, axis=-1, keepdims=True) + EPS)
        o = o * ggla_ref[:, h * C_VAL_DIM:(h + 1) * C_VAL_DIM]
        r = gla_ref[:, 4 * C_KEY_WIDTH + h * C_VAL_DIM: 4 * C_KEY_WIDTH + (h + 1) * C_VAL_DIM].astype(F32)
        oc_ref[:, h * C_VAL_DIM:(h + 1) * C_VAL_DIM] = (o * (r * _sigmoid(r))).astype(BF16)


def _mixers(conv_in, gla_in, small, convw8, wa2p, ba, ggla, tri_bd, b, s):
    ns = s // TL
    tok = lambda bi, si: (bi * ns + si, 0)
    return pl.pallas_call(
        _mix_kernel,
        grid=(b, ns),
        in_specs=[pl.BlockSpec((TL, GROUP_W), tok),
                  pl.BlockSpec((TL, GROUP_W), tok),
                  pl.BlockSpec((TL, LANES), tok),
                  _const_spec((8, B_WIDTH)),
                  _const_spec((LANES, C_KEY_WIDTH)),
                  _const_spec((1, C_KEY_WIDTH)),
                  _const_spec((1, C_VAL_WIDTH)),
                  _const_spec((TL, TL))],
        out_specs=[pl.BlockSpec((TL, B_WIDTH), tok), pl.BlockSpec((TL, C_VAL_WIDTH), tok)],
        out_shape=[jax.ShapeDtypeStruct((b * s, B_WIDTH), BF16),
                   jax.ShapeDtypeStruct((b * s, C_VAL_WIDTH), BF16)],
        scratch_shapes=[pltpu.VMEM((8, B_WIDTH), F32),
                        pltpu.VMEM((C_HEADS, C_KEY_DIM, C_VAL_DIM), F32),
                        pltpu.VMEM((C_HEADS, TL, C_VAL_DIM), BF16)],
        compiler_params=_cparams(("parallel", "arbitrary")),
        name="conv_gla",
    )(conv_in, gla_in, small, convw8, wa2p, ba, ggla, tri_bd)


def _merge_kernel(x_ref, g_ref, oa_ref, ob_ref, oc_ref, wg_ref, bgate_ref, wpa_ref, wpb_ref,
                  wpc_ref, wo_ref, out_ref):
    x = x_ref[...]
    d = x.shape[1]
    h = _rms(x, g_ref[...]).astype(BF16)
    merged = None
    for j, (o_ref, wp_ref) in enumerate(((oa_ref, wpa_ref), (ob_ref, wpb_ref), (oc_ref, wpc_ref))):
        gate = _sigmoid(_dot(h, wg_ref[:, j * d:(j + 1) * d]) + bgate_ref[:, j * d:(j + 1) * d])
        term = gate * _dot(o_ref[...], wp_ref[...])
        merged = term if merged is None else merged + term
    out_ref[...] = x + _dot(merged.astype(BF16), wo_ref[...])


def _merge(x2, g, oa, ob, oc, wg, bgate, wpa, wpb, wpc, wo):
    t, d = x2.shape
    row = lambda w: pl.BlockSpec((TM, w), lambda i: (i, 0))
    return pl.pallas_call(
        _merge_kernel,
        grid=(t // TM,),
        in_specs=[row(d), _const_spec((1, d)), row(A_WIDTH), row(B_WIDTH), row(C_VAL_WIDTH),
                  _const_spec(wg.shape), _const_spec(bgate.shape), _const_spec(wpa.shape),
                  _const_spec(wpb.shape), _const_spec(wpc.shape), _const_spec(wo.shape)],
        out_specs=row(d),
        out_shape=jax.ShapeDtypeStruct((t, d), F32),
        compiler_params=_cparams(("parallel",)),
        name="merge_outproj",
    )(x2, g, oa, ob, oc, wg, bgate, wpa, wpb, wpc, wo)


def _swiglu_tile(h, w1_ref, w3_ref, w2_ref):
    ff = w1_ref.shape[-1]
    acc = None
    for c in range(0, ff, FF_CHUNK):
        a = _dot(h, w1_ref[:, c:c + FF_CHUNK])
        b = _dot(h, w3_ref[:, c:c + FF_CHUNK])
        g = (a * _sigmoid(a) * b).astype(BF16)
        part = _dot(g, w2_ref[c:c + FF_CHUNK, :])
        acc = part if acc is None else acc + part
    return acc


def _ffn_kernel(x_ref, g_ref, w1_ref, w3_ref, w2_ref, out_ref):
    x = x_ref[...]
    h = _rms(x, g_ref[...]).astype(BF16)
    out_ref[...] = x + _swiglu_tile(h, w1_ref, w3_ref, w2_ref)


def _dense_ffn(x2, g, w1, w3, w2):
    t, d = x2.shape
    row = pl.BlockSpec((TM, d), lambda i: (i, 0))
    return pl.pallas_call(
        _ffn_kernel,
        grid=(t // TM,),
        in_specs=[row, _const_spec((1, d)), _const_spec(w1.shape), _const_spec(w3.shape),
                  _const_spec(w2.shape)],
        out_specs=row,
        out_shape=jax.ShapeDtypeStruct((t, d), F32),
        compiler_params=_cparams(("parallel",)),
        name="dense_ffn",
    )(x2, g, w1, w3, w2)


def _router_kernel(x_ref, g_ref, wr_ref, stril_ref, h_ref, rt_ref, seg_ref):
    h = _rms(x_ref[...], g_ref[...])
    h_hi = h.astype(BF16)
    h_ref[...] = h_hi
    h_lo = (h - h_hi.astype(F32)).astype(BF16)
    wr = wr_ref[...]
    w_hi = wr.astype(BF16)
    w_lo = (wr - w_hi.astype(F32)).astype(BF16)
    logits = _dot(h_hi, w_hi) + _dot(h_lo, w_hi) + _dot(h_hi, w_lo)

    lane = lax.broadcasted_iota(I32, (TM, LANES), 1)
    lane_f = lane.astype(F32)
    lg = jnp.where(lane < N_EXPERTS, logits, NEG)
    m1 = jnp.max(lg, axis=-1, keepdims=True)
    i1 = jnp.min(jnp.where(lg == m1, lane_f, float(LANES)), axis=-1, keepdims=True)
    oh1 = lane_f == i1
    lg2 = jnp.where(oh1, NEG, lg)
    m2 = jnp.max(lg2, axis=-1, keepdims=True)
    i2 = jnp.min(jnp.where(lg2 == m2, lane_f, float(LANES)), axis=-1, keepdims=True)
    oh2 = lane_f == i2
    e = jnp.exp(m2 - m1)
    w1 = 1.0 / (1.0 + e)
    w2 = e / (1.0 + e)

    sel = jnp.where(oh1 | oh2, 1.0, 0.0)
    before = _dot(stril_ref[...], sel.astype(BF16))
    cnt = jnp.sum(sel, axis=0, keepdims=True)
    pc = jnp.floor((cnt + (ROW_GRAN - 1)) * (1.0 / ROW_GRAN)) * ROW_GRAN
    lane1 = lax.broadcasted_iota(I32, (1, LANES), 1)
    incl = pc
    for sh in (1, 2, 4):
        incl = incl + jnp.where(lane1 >= sh, pltpu.roll(incl, sh, 1), 0.0)
    off = incl - pc
    pos = before + off
    d1 = jnp.sum(jnp.where(oh1, pos, 0.0), axis=-1, keepdims=True)
    d2 = jnp.sum(jnp.where(oh2, pos, 0.0), axis=-1, keepdims=True)
    rt_ref[...] = jnp.where(lane == 0, d1, jnp.where(lane == 1, d2, jnp.where(lane == 2, w1, jnp.where(lane == 3, w2, 0.0))))
    row8 = lax.broadcasted_iota(I32, (8, LANES), 0)
    seg_ref[0] = jnp.where(row8 == 0, pc, jnp.where(row8 == 1, off, 0.0))


def _router(x2, g, wr_pad, stril):
    t, d = x2.shape
    nb = t // TM
    row = lambda w: pl.BlockSpec((TM, w), lambda i: (i, 0))
    return pl.pallas_call(
        _router_kernel,
        grid=(nb,),
        in_specs=[row(d), _const_spec((1, d)), _const_spec(wr_pad.shape), _const_spec((TM, TM))],
        out_specs=[row(d), row(LANES), pl.BlockSpec((1, 8, LANES), lambda i: (i, 0, 0))],
        out_shape=[jax.ShapeDtypeStruct((t, d), BF16), jax.ShapeDtypeStruct((t, LANES), F32),
                   jax.ShapeDtypeStruct((nb, 8, LANES), F32)],
        compiler_params=_cparams(("parallel",)),
        name="moe_router",
    )(x2, g, wr_pad, stril)


def _chunk_copy(src_ref, src_row, dst_ref, dst_row, sem):
    return pltpu.make_async_copy(src_ref.at[pl.ds(pl.multiple_of(src_row, ROW_GRAN), ROW_GRAN)],
                                 dst_ref.at[pl.ds(pl.multiple_of(dst_row, ROW_GRAN), ROW_GRAN)], sem)


def _segment_copies(b, gstart_ref, nch_ref, off_ref, hbm_ref, buf_ref, sem, to_hbm):
    total = 0
    for e in range(N_EXPERTS):
        n = nch_ref[b * N_EXPERTS + e]
        g0 = gstart_ref[b * N_EXPERTS + e]
        o0 = off_ref[b * N_EXPERTS + e]

        def start(c, carry, g0=g0, o0=o0):
            if to_hbm:
                _chunk_copy(buf_ref, o0 + c * ROW_GRAN, hbm_ref, g0 + c * ROW_GRAN, sem).start()
            else:
                _chunk_copy(hbm_ref, g0 + c * ROW_GRAN, buf_ref, o0 + c * ROW_GRAN, sem).start()
            return carry

        lax.fori_loop(0, n, start, 0)
        total = total + n

    def wait(c, carry):
        _chunk_copy(buf_ref, 0, buf_ref, 0, sem).wait()
        return carry

    lax.fori_loop(0, total, wait, 0)


def _dispatch_kernel(gstart_ref, nch_ref, off_ref, h_ref, rt_ref, xs_in_ref, xs_ref, buf_ref, sem):
    del xs_in_ref
    b = pl.program_id(0)
    dest = rt_ref[...].T
    r = lax.broadcasted_iota(I32, (SORT_ROWS, TM), 0).astype(F32)
    perm = jnp.where((r == dest[0:1, :]) | (r == dest[1:2, :]), 1.0, 0.0).astype(BF16)
    buf_ref[...] = _dot(perm, h_ref[...]).astype(BF16)
    _segment_copies(b, gstart_ref, nch_ref, off_ref, xs_ref, buf_ref, sem, to_hbm=True)


def _dispatch(gstart, nch, off, h2, rt, xs_zero):
    t, d = h2.shape
    grid_spec = pltpu.PrefetchScalarGridSpec(
        num_scalar_prefetch=3,
        grid=(t // TM,),
        in_specs=[pl.BlockSpec((TM, d), lambda i, *_: (i, 0)),
                  pl.BlockSpec((TM, LANES), lambda i, *_: (i, 0)),
                  pl.BlockSpec(memory_space=pl.ANY)],
        out_specs=pl.BlockSpec(memory_space=pl.ANY),
        scratch_shapes=[pltpu.VMEM((SORT_ROWS, d), BF16), pltpu.SemaphoreType.DMA(())],
    )
    return pl.pallas_call(
        _dispatch_kernel,
        grid_spec=grid_spec,
        out_shape=jax.ShapeDtypeStruct(xs_zero.shape, xs_zero.dtype),
        input_output_aliases={5: 0},
        compiler_params=_cparams(("arbitrary",)),
        name="moe_dispatch",
    )(gstart, nch, off, h2, rt, xs_zero)


def _group_ffn_kernel(te_ref, ta_ref, ts_ref, xs_ref, w1_ref, w3_ref, w2_ref, y_ref):
    del te_ref, ts_ref
    active = ta_ref[pl.program_id(0)] == 1

    @pl.when(active)
    def _():
        y_ref[...] = _swiglu_tile(xs_ref[...], w1_ref.at[0], w3_ref.at[0], w2_ref.at[0]).astype(BF16)

    @pl.when(jnp.logical_not(active))
    def _():
        y_ref[...] = jnp.zeros_like(y_ref)


def _group_ffn(tile_expert, tile_active, tile_src, xs, w1, w3, w2):
    ns, d = xs.shape
    ff = w1.shape[-1]
    grid_spec = pltpu.PrefetchScalarGridSpec(
        num_scalar_prefetch=3,
        grid=(ns // TMM,),
        in_specs=[pl.BlockSpec((TMM, d), lambda i, te, ta, ts: (ts[i], 0)),
                  pl.BlockSpec((1, d, ff), lambda i, te, ta, ts: (te[i], 0, 0)),
                  pl.BlockSpec((1, d, ff), lambda i, te, ta, ts: (te[i], 0, 0)),
                  pl.BlockSpec((1, ff, d), lambda i, te, ta, ts: (te[i], 0, 0))],
        out_specs=pl.BlockSpec((TMM, d), lambda i, te, ta, ts: (i, 0)),
    )
    return pl.pallas_call(
        _group_ffn_kernel,
        grid_spec=grid_spec,
        out_shape=jax.ShapeDtypeStruct((ns, d), BF16),
        compiler_params=_cparams(("arbitrary",)),
        name="moe_group_ffn",
    )(tile_expert, tile_active, tile_src, xs, w1, w3, w2)


def _combine_kernel(gstart_ref, nch_ref, off_ref, x_ref, rt_ref, gf_ref, y_ref, out_ref, buf_ref, sem,
                    *, final_norm):
    b = pl.program_id(0)
    buf_ref[...] = jnp.zeros_like(buf_ref)
    _segment_copies(b, gstart_ref, nch_ref, off_ref, y_ref, buf_ref, sem, to_hbm=False)
    rt = rt_ref[...]
    r = lax.broadcasted_iota(I32, (TM, SORT_ROWS), 1).astype(F32)
    ybuf = buf_ref[...]
    y1 = _dot(jnp.where(r == rt[:, 0:1], 1.0, 0.0).astype(BF16), ybuf)
    y2 = _dot(jnp.where(r == rt[:, 1:2], 1.0, 0.0).astype(BF16), ybuf)
    x = x_ref[...] + rt[:, 2:3] * y1 + rt[:, 3:4] * y2
    out_ref[...] = _rms(x, gf_ref[...]) if final_norm else x


def _combine(gstart, nch, off, x2, rt, g_final, y, final_norm):
    t, d = x2.shape
    grid_spec = pltpu.PrefetchScalarGridSpec(
        num_scalar_prefetch=3,
        grid=(t // TM,),
        in_specs=[pl.BlockSpec((TM, d), lambda i, *_: (i, 0)),
                  pl.BlockSpec((TM, LANES), lambda i, *_: (i, 0)),
                  pl.BlockSpec((1, d), lambda i, *_: (0, 0)),
                  pl.BlockSpec(memory_space=pl.ANY)],
        out_specs=pl.BlockSpec((TM, d), lambda i, *_: (i, 0)),
        scratch_shapes=[pltpu.VMEM((SORT_ROWS, d), BF16), pltpu.SemaphoreType.DMA(())],
    )
    return pl.pallas_call(
        functools.partial(_combine_kernel, final_norm=final_norm),
        grid_spec=grid_spec,
        out_shape=jax.ShapeDtypeStruct((t, d), F32),
        compiler_params=_cparams(("arbitrary",)),
        name="moe_combine",
    )(gstart, nch, off, x2, rt, g_final, y)


def _norm_kernel(x_ref, g_ref, out_ref):
    out_ref[...] = _rms(x_ref[...], g_ref[...])


def _final_norm(x2, g):
    t, d = x2.shape
    row = pl.BlockSpec((TM, d), lambda i: (i, 0))
    return pl.pallas_call(
        _norm_kernel, grid=(t // TM,), in_specs=[row, _const_spec((1, d))], out_specs=row,
        out_shape=jax.ShapeDtypeStruct((t, d), F32), compiler_params=_cparams(("parallel",)),
        name="final_norm",
    )(x2, g)


def _tril_blocks(n, block, strict=False):
    r = jnp.arange(n)[:, None]
    c = jnp.arange(n)[None, :]
    keep = (c < r) if strict else (c <= r)
    keep = keep & ((r // block) == (c // block))
    return keep.astype(BF16)


def _moe_layer(x2, g_ffn, w_router, w1, w3, w2, g_final, final_norm):
    t, d = x2.shape
    nb = t // TM
    wr_pad = jnp.pad(w_router, ((0, 0), (0, LANES - N_EXPERTS)))
    h2, rt, seg = _router(x2, g_ffn.reshape(1, d), wr_pad, _tril_blocks(TM, TM, strict=True))

    pc = seg[:, 0, :N_EXPERTS].astype(I32)
    off = seg[:, 1, :N_EXPERTS].astype(I32)
    region = ((jnp.sum(pc, axis=0) + TMM - 1) // TMM) * TMM
    ends = jnp.cumsum(region)
    gstart = (ends - region)[None, :] + jnp.cumsum(pc, axis=0) - pc
    n_tiles = (2 * t + nb * N_EXPERTS * ROW_GRAN) // TMM + N_EXPERTS
    tile_start = jnp.arange(n_tiles, dtype=I32) * TMM
    tile_expert = jnp.minimum(jnp.sum(tile_start[:, None] >= ends[None, :], axis=1), N_EXPERTS - 1).astype(I32)
    tile_active = (tile_start < ends[-1]).astype(I32)
    last_active = jnp.maximum(ends[-1] // TMM - 1, 0)
    tile_src = jnp.minimum(jnp.arange(n_tiles, dtype=I32), last_active).astype(I32)
    seg_args = (gstart.reshape(-1).astype(I32), (pc // ROW_GRAN).reshape(-1), off.reshape(-1))

    xs = _dispatch(*seg_args, h2, rt, jnp.zeros((n_tiles * TMM, d), BF16))
    y = _group_ffn(tile_expert, tile_active, tile_src, xs, w1.astype(BF16), w3.astype(BF16), w2.astype(BF16))
    return _combine(*seg_args, x2, rt, g_final.reshape(1, d), y, final_norm)


def kernel(x, g_mix, w_in, b_f, conv_w, w_a2, b_a, g_gla, w_pa, w_pb, w_pc, w_gate, b_gate, w_o,
           g_ffn, ffn_w1, ffn_w3, ffn_w2, w_router, moe_w1, moe_w3, moe_w2, g_final):
    b, s, d = x.shape
    depth = g_mix.shape[0]
    t = b * s
    assert t % TM == 0 and s % TQ == 0 and s % TL == 0 and TL % GLA_CHUNK == 0
    x2 = x.reshape(t, d)
    tril_full = _tril_blocks(TM, TM)
    tril_chunk = _tril_blocks(TL, GLA_CHUNK)
    o_af = 3 * A_WIDTH
    o_b = o_af + A_HEADS
    o_c = o_b + 3 * B_WIDTH
    o_ca = o_c + 2 * C_KEY_WIDTH + C_VAL_WIDTH
    o_cr = o_ca + C_LOWRANK

    done_final = False
    for l in range(depth):
        w = w_in[l]
        wq = w[:, 0:A_WIDTH] * (A_HEAD_DIM ** -0.5 * LOG2E)
        wbig = jnp.concatenate([wq, w[:, A_WIDTH:2 * A_WIDTH], w[:, o_b:o_c], w[:, o_c:o_ca], w[:, o_cr:]],
                               axis=1).astype(BF16)
        wvt = w[:, 2 * A_WIDTH:o_af].T.astype(BF16)
        wsmall = jnp.concatenate([w[:, o_af:o_b], w[:, o_ca:o_cr],
                                  jnp.zeros((d, LANES - A_HEADS - C_LOWRANK), F32)], axis=1).astype(BF16)
        gm = g_mix[l].reshape(1, d)
        qk, vt, conv_in, gla_in, small = _inproj(x2, gm, wbig, wvt, wsmall)

        bf_pad = jnp.pad(b_f[l], (0, LANES - A_HEADS)).reshape(1, LANES)
        qp, kp = _fcum(small.reshape(b, s, LANES), bf_pad, tril_full, qk)
        o_a = _attention(qp, kp, vt, b, s)

        convw8 = jnp.pad(conv_w[l], ((0, 8 - conv_w.shape[1]), (0, 0)))
        wa2p = jnp.zeros((LANES, C_KEY_WIDTH), F32).at[A_HEADS:A_HEADS + C_LOWRANK].set(w_a2[l]).astype(BF16)
        o_bm, o_cm = _mixers(conv_in, gla_in, small, convw8, wa2p, b_a[l].reshape(1, -1),
                             g_gla[l].reshape(1, -1), tril_chunk, b, s)

        x2 = _merge(x2, gm, o_a, o_bm, o_cm, w_gate[l].astype(BF16), b_gate[l].reshape(1, -1),
                    w_pa[l].astype(BF16), w_pb[l].astype(BF16), w_pc[l].astype(BF16), w_o[l].astype(BF16))

        i = l // 2
        if l % 2 == 0:
            x2 = _dense_ffn(x2, g_ffn[l].reshape(1, d), ffn_w1[i].astype(BF16), ffn_w3[i].astype(BF16),
                            ffn_w2[i].astype(BF16))
        else:
            done_final = l == depth - 1
            x2 = _moe_layer(x2, g_ffn[l], w_router[i], moe_w1[i], moe_w3[i], moe_w2[i], g_final, done_final)
    if not done_final:
        x2 = _final_norm(x2, g_final.reshape(1, d))
    return x2.reshape(b, s, d)
```

```python
import functools
import math

import numpy as np
import jax
import jax.numpy as jnp
from jax import lax
from jax.experimental import pallas as pl
from jax.experimental.pallas import tpu as pltpu

F32 = jnp.float32
BF16 = jnp.bfloat16
I32 = jnp.int32

EPS = 1e-6
NEG = -1e30
LOG2E = math.log2(math.e)

A_HEADS, A_HEAD_DIM = 8, 64
A_WIDTH = A_HEADS * A_HEAD_DIM
B_WIDTH = 512
C_HEADS, C_KEY_DIM, C_VAL_DIM = 4, 64, 128
C_KEY_WIDTH = C_HEADS * C_KEY_DIM
C_VAL_WIDTH = C_HEADS * C_VAL_DIM
C_LOWRANK = 16
C_GATE_TEMP = 16.0
GLA_CHUNK = 64
N_EXPERTS = 8
GROUP_W = 1536

LANES = 128
V7X_VMEM_BYTES = 64 * 1024 * 1024
VMEM_LIMIT = V7X_VMEM_BYTES - 8 * 1024 * 1024

TM = 512
TQ = 512
ATT_HEADS = 4
ACC_ROWS = A_HEAD_DIM + 16
TL = 256
FF_CHUNK = 256
TMM = 512
ROW_GRAN = 16
SORT_ROWS = 2 * TM + N_EXPERTS * ROW_GRAN


def _cparams(sem):
    return pltpu.CompilerParams(dimension_semantics=sem, vmem_limit_bytes=VMEM_LIMIT)


def _dot(a, b):
    return jnp.dot(a, b, preferred_element_type=F32)


def _dot_nt(a, b):
    return lax.dot_general(a, b, (((1,), (1,)), ((), ())), preferred_element_type=F32)


def _rms(x, g):
    return x * lax.rsqrt(jnp.mean(x * x, axis=-1, keepdims=True) + EPS) * g


def _log_sigmoid(z):
    return jnp.minimum(z, 0.0) - jnp.log(1.0 + jnp.exp(-jnp.abs(z)))


def _sigmoid(z):
    return 1.0 / (1.0 + jnp.exp(-z))


def _split3(x):
    hi = x.astype(BF16)
    r1 = x - hi.astype(F32)
    mid = r1.astype(BF16)
    lo = (r1 - mid.astype(F32)).astype(BF16)
    return hi, mid, lo


def _const_spec(shape):
    return pl.BlockSpec(shape, lambda *_: (0,) * len(shape))


def _inproj_kernel(x_ref, g_ref, wbig_ref, wvt_ref, wsmall_ref, qk_ref, vt_ref, conv_ref, gla_ref, small_ref):
    h = _rms(x_ref[...], g_ref[...]).astype(BF16)
    off = 0
    for ref in (qk_ref, conv_ref, gla_ref):
        width = ref.shape[1]
        for c in range(0, width, 512):
            ref[:, c:c + 512] = _dot(h, wbig_ref[:, off + c: off + c + 512]).astype(BF16)
        off += width
    vt_ref[0] = _dot_nt(wvt_ref[...], h).astype(BF16)
    small_ref[...] = _dot(h, wsmall_ref[...])


def _inproj(x2, g, wbig, wvt, wsmall):
    t, d = x2.shape
    row = lambda w: pl.BlockSpec((TM, w), lambda i: (i, 0))
    return pl.pallas_call(
        _inproj_kernel,
        grid=(t // TM,),
        in_specs=[row(d), _const_spec((1, d)), _const_spec(wbig.shape), _const_spec(wvt.shape),
                  _const_spec(wsmall.shape)],
        out_specs=[row(2 * A_WIDTH), pl.BlockSpec((1, A_WIDTH, TM), lambda i: (i, 0, 0)), row(GROUP_W), row(GROUP_W),
                   row(LANES)],
        out_shape=[jax.ShapeDtypeStruct((t, 2 * A_WIDTH), BF16), jax.ShapeDtypeStruct((t // TM, A_WIDTH, TM), BF16),
                   jax.ShapeDtypeStruct((t, GROUP_W), BF16), jax.ShapeDtypeStruct((t, GROUP_W), BF16),
                   jax.ShapeDtypeStruct((t, LANES), F32)],
        compiler_params=_cparams(("parallel",)),
        name="inproj",
    )(x2, g, wbig, wvt, wsmall)


def _fcum_kernel(small_ref, bf_ref, tril_ref, place_ref, qk_ref, qp_ref, kp_ref, carry_ref):
    @pl.when(pl.program_id(1) == 0)
    def _():
        carry_ref[...] = jnp.zeros_like(carry_ref)

    lf = LOG2E * _log_sigmoid(small_ref[0] + bf_ref[...])
    hi, mid, lo = _split3(lf)
    tril = tril_ref[...]
    cs = _dot(tril, hi) + _dot(tril, mid) + _dot(tril, lo) + carry_ref[...]
    carry_ref[...] = cs[TM - 1:TM, :]
    fterms = _dot(jnp.concatenate(_split3(cs), axis=1), place_ref[...])

    lane = lax.broadcasted_iota(I32, (1, LANES), 1)
    is_x = lane < A_HEAD_DIM
    is_a = (lane >= A_HEAD_DIM) & (lane < A_HEAD_DIM + 3)
    is_b = (lane >= A_HEAD_DIM + 3) & (lane < A_HEAD_DIM + 6)
    for p in range(A_HEADS // 2):
        q2 = qk_ref[:, p * LANES:(p + 1) * LANES].astype(F32)
        k2 = qk_ref[:, A_WIDTH + p * LANES: A_WIDTH + (p + 1) * LANES].astype(F32)
        halves = ((q2, k2), (pltpu.roll(q2, A_HEAD_DIM, 1), pltpu.roll(k2, A_HEAD_DIM, 1)))
        for j in range(2):
            h = 2 * p + j
            fh = fterms[:, h * LANES:(h + 1) * LANES]
            qh, kh = halves[j]
            qp_ref[:, h * LANES:(h + 1) * LANES] = jnp.where(
                is_x, qh, jnp.where(is_a, fh, jnp.where(is_b, 1.0, 0.0))).astype(BF16)
            kp_ref[:, h * LANES:(h + 1) * LANES] = jnp.where(
                is_x, kh, jnp.where(is_a, 1.0, jnp.where(is_b, -fh, 0.0))).astype(BF16)


def _aug_placement():
    place = np.zeros((3, LANES, A_HEADS * LANES), np.float32)
    for h in range(A_HEADS):
        for c in range(3):
            place[c, h, h * LANES + A_HEAD_DIM + c] = 1.0
            place[c, h, h * LANES + A_HEAD_DIM + 3 + c] = 1.0
    return jnp.asarray(place.reshape(3 * LANES, A_HEADS * LANES), BF16)


def _fcum(small3, bf_pad, tril, qk):
    b, s, _ = small3.shape
    ns = s // TM
    place = _aug_placement()
    tok = lambda bi, si: (bi * ns + si, 0)
    wide = pl.BlockSpec((TM, A_HEADS * LANES), tok)
    return pl.pallas_call(
        _fcum_kernel,
        grid=(b, ns),
        in_specs=[pl.BlockSpec((1, TM, LANES), lambda bi, si: (bi, si, 0)),
                  _const_spec((1, LANES)),
                  _const_spec((TM, TM)),
                  _const_spec(place.shape),
                  pl.BlockSpec((TM, 2 * A_WIDTH), tok)],
        out_specs=[wide, wide],
        out_shape=[jax.ShapeDtypeStruct((b * s, A_HEADS * LANES), BF16)] * 2,
        scratch_shapes=[pltpu.VMEM((1, LANES), F32)],
        compiler_params=_cparams(("parallel", "arbitrary")),
        name="fcum",
    )(small3, bf_pad, tril, place, qk)


def _attn_kernel(q_ref, k_ref, vt_ref, o_ref, m_ref, acc_ref):
    qi = pl.program_id(2)
    m_ref[...] = jnp.full(m_ref.shape, NEG, F32)
    acc_ref[...] = jnp.zeros_like(acc_ref)
    ones = jnp.ones((ACC_ROWS - A_HEAD_DIM, TQ), BF16)

    def _step(ki, diagonal):
        if diagonal:
            key = lax.broadcasted_iota(I32, (TQ, TQ), 0)
            qry = lax.broadcasted_iota(I32, (TQ, TQ), 1)
            keep = key <= qry
        scores = [_dot_nt(k_ref[ki, :, j * LANES:(j + 1) * LANES], q_ref[:, j * LANES:(j + 1) * LANES])
                  for j in range(ATT_HEADS)]
        for j in range(ATT_HEADS):
            st = scores[j]
            if diagonal:
                st = jnp.where(keep, st, NEG)
            m_prev = m_ref[j]
            m_new = jnp.maximum(m_prev, jnp.max(st, axis=0, keepdims=True))
            alpha = jnp.exp2(m_prev - m_new)
            pr = jnp.exp2(st - m_new).astype(BF16)
            vt_j = jnp.concatenate([vt_ref[ki, j * A_HEAD_DIM:(j + 1) * A_HEAD_DIM, :], ones], axis=0)
            acc_ref[j] = alpha * acc_ref[j] + _dot(vt_j, pr)
            m_ref[j] = m_new

    def _below(ki, carry):
        _step(ki, False)
        return carry

    lax.fori_loop(0, qi, _below, 0)
    _step(qi, True)
    ot = jnp.concatenate([acc_ref[j, 0:A_HEAD_DIM, :] / acc_ref[j, A_HEAD_DIM:A_HEAD_DIM + 1, :]
                          for j in range(ATT_HEADS)], axis=0)
    o_ref[...] = ot.T.astype(BF16)


def _attention(qp, kp, vt, b, s):
    nq = s // TQ
    ng = A_HEADS // ATT_HEADS
    kp3 = kp.reshape(b * nq, TQ, A_HEADS * LANES)
    return pl.pallas_call(
        _attn_kernel,
        grid=(b, ng, nq),
        in_specs=[pl.BlockSpec((TQ, ATT_HEADS * LANES), lambda bi, p, qi: (bi * nq + qi, p)),
                  pl.BlockSpec((nq, TQ, ATT_HEADS * LANES), lambda bi, p, qi: (bi, 0, p)),
                  pl.BlockSpec((nq, ATT_HEADS * A_HEAD_DIM, TQ), lambda bi, p, qi: (bi, p, 0))],
        out_specs=pl.BlockSpec((TQ, ATT_HEADS * A_HEAD_DIM), lambda bi, p, qi: (bi * nq + qi, p)),
        out_shape=jax.ShapeDtypeStruct((b * s, A_WIDTH), BF16),
        scratch_shapes=[pltpu.VMEM((ATT_HEADS, 1, TQ), F32), pltpu.VMEM((ATT_HEADS, ACC_ROWS, TQ), F32)],
        compiler_params=_cparams(("parallel", "parallel", "arbitrary")),
        name="fox_attention",
    )(qp, kp3, vt)


def _mix_kernel(conv_ref, gla_ref, small_ref, convw_ref, wa2_ref, ba_ref, ggla_ref, tri_ref,
                ob_ref, oc_ref, zprev_ref, st_ref, sall_ref):
    @pl.when(pl.program_id(1) == 0)
    def _():
        zprev_ref[...] = jnp.zeros_like(zprev_ref)
        st_ref[...] = jnp.zeros_like(st_ref)

    nc = TL // GLA_CHUNK

    u = conv_ref[:, 0:B_WIDTH].astype(F32)
    bg = conv_ref[:, B_WIDTH:2 * B_WIDTH].astype(F32)
    cg = conv_ref[:, 2 * B_WIDTH:3 * B_WIDTH].astype(F32)
    z = cg * u
    row = lax.broadcasted_iota(I32, (TL, B_WIDTH), 0)
    zp = zprev_ref[...]
    z1 = jnp.where(row == 0, zp[7:8, :], pltpu.roll(z, 1, 0))
    z2 = jnp.where(row == 0, zp[6:7, :], jnp.where(row == 1, zp[7:8, :], pltpu.roll(z, 2, 0)))
    cw = convw_ref[...]
    ob_ref[...] = (bg * (cw[0:1, :] * z2 + cw[1:2, :] * z1 + cw[2:3, :] * z)).astype(BF16)
    zprev_ref[...] = z[TL - 8:TL, :]

    la = _dot(small_ref[...].astype(BF16), wa2_ref[...]) + ba_ref[...]
    la = _log_sigmoid(la) * (1.0 / C_GATE_TEMP)
    hi, mid, lo = _split3(la)
    tri = tri_ref[...]
    bcum = _dot(tri, hi) + _dot(tri, mid) + _dot(tri, lo)
    bl = bcum.reshape(nc, GLA_CHUNK, C_KEY_WIDTH)[:, GLA_CHUNK - 1:GLA_CHUNK, :]
    blast = jnp.broadcast_to(bl, (nc, GLA_CHUNK, C_KEY_WIDTH)).reshape(TL, C_KEY_WIDTH)

    q = gla_ref[:, 0:C_KEY_WIDTH].astype(F32)
    k = gla_ref[:, C_KEY_WIDTH:2 * C_KEY_WIDTH].astype(F32)
    q_t = q * (C_KEY_DIM ** -0.5) * jnp.exp(bcum)
    k_t = (k * jnp.exp(-bcum)).astype(BF16)
    ks_t = (k * jnp.exp(blast - bcum)).T
    dec_t = jnp.broadcast_to(jnp.exp(bl), (nc, LANES, C_KEY_WIDTH)).reshape(nc * LANES, C_KEY_WIDTH).T

    r_i = lax.broadcasted_iota(I32, (TL, TL), 0)
    c_i = lax.broadcasted_iota(I32, (TL, TL), 1)
    same = (r_i // GLA_CHUNK) == (c_i // GLA_CHUNK)
    intra = same & (c_i <= r_i)
    lane_k = lax.broadcasted_iota(I32, (1, C_KEY_WIDTH), 1)
    lane = lax.broadcasted_iota(I32, (1, LANES), 1)

    for h in range(C_HEADS):
        v_h = gla_ref[:, 2 * C_KEY_WIDTH + h * C_VAL_DIM: 2 * C_KEY_WIDTH + (h + 1) * C_VAL_DIM]
        q_h = jnp.where(lane_k // C_KEY_DIM == h, q_t, 0.0).astype(BF16)
        a = jnp.where(intra, _dot_nt(q_h, k_t), 0.0).astype(BF16)

        q2 = q_t[:, (h // 2) * LANES:(h // 2 + 1) * LANES]
        q2r = pltpu.roll(q2, C_KEY_DIM, 1)
        dup = jnp.where((lane < C_KEY_DIM) == (h % 2 == 0), q2, q2r)
        q_exp = jnp.where(same, jnp.concatenate([dup] * (TL // LANES), axis=1), 0.0).astype(BF16)

        ks_h = ks_t[h * C_KEY_DIM:(h + 1) * C_KEY_DIM, :]
        k_exp = jnp.where(same, jnp.concatenate([ks_h] * nc, axis=0), 0.0).astype(BF16)
        kv = _dot(k_exp, v_h)

        st = st_ref[h]
        for c in range(nc):
            sall_ref[h, c * GLA_CHUNK:(c + 1) * GLA_CHUNK, :] = st.astype(BF16)
            dec = dec_t[h * C_KEY_DIM:(h + 1) * C_KEY_DIM, c * LANES:(c + 1) * LANES]
            st = dec * st + kv[c * GLA_CHUNK:(c + 1) * GLA_CHUNK, :]
        st_ref[h] = st

        o = _dot(a, v_h) + _dot(q_exp, sall_ref[h])
        o = o * lax.rsqrt(jnp.mean(o * o, axis=-1, keepdims=True) + EPS)
        o = o * ggla_ref[:, h * C_VAL_DIM:(h + 1) * C_VAL_DIM]
        r = gla_ref[:, 4 * C_KEY_WIDTH + h * C_VAL_DIM: 4 * C_KEY_WIDTH + (h + 1) * C_VAL_DIM].astype(F32)
        oc_ref[:, h * C_VAL_DIM:(h + 1) * C_VAL_DIM] = (o * (r * _sigmoid(r))).astype(BF16)


def _mixers(conv_in, gla_in, small, convw8, wa2p, ba, ggla, tri_bd, b, s):
    ns = s // TL
    tok = lambda bi, si: (bi * ns + si, 0)
    return pl.pallas_call(
        _mix_kernel,
        grid=(b, ns),
        in_specs=[pl.BlockSpec((TL, GROUP_W), tok),
                  pl.BlockSpec((TL, GROUP_W), tok),
                  pl.BlockSpec((TL, LANES), tok),
                  _const_spec((8, B_WIDTH)),
                  _const_spec((LANES, C_KEY_WIDTH)),
                  _const_spec((1, C_KEY_WIDTH)),
                  _const_spec((1, C_VAL_WIDTH)),
                  _const_spec((TL, TL))],
        out_specs=[pl.BlockSpec((TL, B_WIDTH), tok), pl.BlockSpec((TL, C_VAL_WIDTH), tok)],
        out_shape=[jax.ShapeDtypeStruct((b * s, B_WIDTH), BF16),
                   jax.ShapeDtypeStruct((b * s, C_VAL_WIDTH), BF16)],
        scratch_shapes=[pltpu.VMEM((8, B_WIDTH), F32),
                        pltpu.VMEM((C_HEADS, C_KEY_DIM, C_VAL_DIM), F32),
                        pltpu.VMEM((C_HEADS, TL, C_VAL_DIM), BF16)],
        compiler_params=_cparams(("parallel", "arbitrary")),
        name="conv_gla",
    )(conv_in, gla_in, small, convw8, wa2p, ba, ggla, tri_bd)


def _merge_kernel(x_ref, g_ref, oa_ref, ob_ref, oc_ref, wg_ref, bgate_ref, wpa_ref, wpb_ref,
                  wpc_ref, wo_ref, out_ref):
    x = x_ref[...]
    d = x.shape[1]
    h = _rms(x, g_ref[...]).astype(BF16)
    merged = None
    for j, (o_ref, wp_ref) in enumerate(((oa_ref, wpa_ref), (ob_ref, wpb_ref), (oc_ref, wpc_ref))):
        gate = _sigmoid(_dot(h, wg_ref[:, j * d:(j + 1) * d]) + bgate_ref[:, j * d:(j + 1) * d])
        term = gate * _dot(o_ref[...], wp_ref[...])
        merged = term if merged is None else merged + term
    out_ref[...] = x + _dot(merged.astype(BF16), wo_ref[...])


def _merge(x2, g, oa, ob, oc, wg, bgate, wpa, wpb, wpc, wo):
    t, d = x2.shape
    row = lambda w: pl.BlockSpec((TM, w), lambda i: (i, 0))
    return pl.pallas_call(
        _merge_kernel,
        grid=(t // TM,),
        in_specs=[row(d), _const_spec((1, d)), row(A_WIDTH), row(B_WIDTH), row(C_VAL_WIDTH),
                  _const_spec(wg.shape), _const_spec(bgate.shape), _const_spec(wpa.shape),
                  _const_spec(wpb.shape), _const_spec(wpc.shape), _const_spec(wo.shape)],
        out_specs=row(d),
        out_shape=jax.ShapeDtypeStruct((t, d), F32),
        compiler_params=_cparams(("parallel",)),
        name="merge_outproj",
    )(x2, g, oa, ob, oc, wg, bgate, wpa, wpb, wpc, wo)


def _swiglu_tile(h, w1_ref, w3_ref, w2_ref):
    ff = w1_ref.shape[-1]
    acc = None
    for c in range(0, ff, FF_CHUNK):
        a = _dot(h, w1_ref[:, c:c + FF_CHUNK])
        b = _dot(h, w3_ref[:, c:c + FF_CHUNK])
        g = (a * _sigmoid(a) * b).astype(BF16)
        part = _dot(g, w2_ref[c:c + FF_CHUNK, :])
        acc = part if acc is None else acc + part
    return acc


def _ffn_kernel(x_ref, g_ref, w1_ref, w3_ref, w2_ref, out_ref):
    x = x_ref[...]
    h = _rms(x, g_ref[...]).astype(BF16)
    out_ref[...] = x + _swiglu_tile(h, w1_ref, w3_ref, w2_ref)


def _dense_ffn(x2, g, w1, w3, w2):
    t, d = x2.shape
    row = pl.BlockSpec((TM, d), lambda i: (i, 0))
    return pl.pallas_call(
        _ffn_kernel,
        grid=(t // TM,),
        in_specs=[row, _const_spec((1, d)), _const_spec(w1.shape), _const_spec(w3.shape),
                  _const_spec(w2.shape)],
        out_specs=row,
        out_shape=jax.ShapeDtypeStruct((t, d), F32),
        compiler_params=_cparams(("parallel",)),
        name="dense_ffn",
    )(x2, g, w1, w3, w2)


def _router_kernel(x_ref, g_ref, wr_ref, stril_ref, h_ref, rt_ref, seg_ref):
    h = _rms(x_ref[...], g_ref[...])
    h_hi = h.astype(BF16)
    h_ref[...] = h_hi
    h_lo = (h - h_hi.astype(F32)).astype(BF16)
    wr = wr_ref[...]
    w_hi = wr.astype(BF16)
    w_lo = (wr - w_hi.astype(F32)).astype(BF16)
    logits = _dot(h_hi, w_hi) + _dot(h_lo, w_hi) + _dot(h_hi, w_lo)

    lane = lax.broadcasted_iota(I32, (TM, LANES), 1)
    lane_f = lane.astype(F32)
    lg = jnp.where(lane < N_EXPERTS, logits, NEG)
    m1 = jnp.max(lg, axis=-1, keepdims=True)
    i1 = jnp.min(jnp.where(lg == m1, lane_f, float(LANES)), axis=-1, keepdims=True)
    oh1 = lane_f == i1
    lg2 = jnp.where(oh1, NEG, lg)
    m2 = jnp.max(lg2, axis=-1, keepdims=True)
    i2 = jnp.min(jnp.where(lg2 == m2, lane_f, float(LANES)), axis=-1, keepdims=True)
    oh2 = lane_f == i2
    e = jnp.exp(m2 - m1)
    w1 = 1.0 / (1.0 + e)
    w2 = e / (1.0 + e)

    sel = jnp.where(oh1 | oh2, 1.0, 0.0)
    before = _dot(stril_ref[...], sel.astype(BF16))
    cnt = jnp.sum(sel, axis=0, keepdims=True)
    pc = jnp.floor((cnt + (ROW_GRAN - 1)) * (1.0 / ROW_GRAN)) * ROW_GRAN
    lane1 = lax.broadcasted_iota(I32, (1, LANES), 1)
    incl = pc
    for sh in (1, 2, 4):
        incl = incl + jnp.where(lane1 >= sh, pltpu.roll(incl, sh, 1), 0.0)
    off = incl - pc
    pos = before + off
    d1 = jnp.sum(jnp.where(oh1, pos, 0.0), axis=-1, keepdims=True)
    d2 = jnp.sum(jnp.where(oh2, pos, 0.0), axis=-1, keepdims=True)
    rt_ref[...] = jnp.where(lane == 0, d1, jnp.where(lane == 1, d2, jnp.where(lane == 2, w1, jnp.where(lane == 3, w2, 0.0))))
    row8 = lax.broadcasted_iota(I32, (8, LANES), 0)
    seg_ref[0] = jnp.where(row8 == 0, pc, jnp.where(row8 == 1, off, 0.0))


def _router(x2, g, wr_pad, stril):
    t, d = x2.shape
    nb = t // TM
    row = lambda w: pl.BlockSpec((TM, w), lambda i: (i, 0))
    return pl.pallas_call(
        _router_kernel,
        grid=(nb,),
        in_specs=[row(d), _const_spec((1, d)), _const_spec(wr_pad.shape), _const_spec((TM, TM))],
        out_specs=[row(d), row(LANES), pl.BlockSpec((1, 8, LANES), lambda i: (i, 0, 0))],
        out_shape=[jax.ShapeDtypeStruct((t, d), BF16), jax.ShapeDtypeStruct((t, LANES), F32),
                   jax.ShapeDtypeStruct((nb, 8, LANES), F32)],
        compiler_params=_cparams(("parallel",)),
        name="moe_router",
    )(x2, g, wr_pad, stril)


def _chunk_copy(src_ref, src_row, dst_ref, dst_row, sem):
    return pltpu.make_async_copy(src_ref.at[pl.ds(pl.multiple_of(src_row, ROW_GRAN), ROW_GRAN)],
                                 dst_ref.at[pl.ds(pl.multiple_of(dst_row, ROW_GRAN), ROW_GRAN)], sem)


def _segment_starts(b, gstart_ref, nch_ref, off_ref, hbm_ref, buf_ref, sem, to_hbm):
    for e in range(N_EXPERTS):
        n = nch_ref[b * N_EXPERTS + e]
        g0 = gstart_ref[b * N_EXPERTS + e]
        o0 = off_ref[b * N_EXPERTS + e]

        def start(c, carry, g0=g0, o0=o0):
            if to_hbm:
                _chunk_copy(buf_ref, o0 + c * ROW_GRAN, hbm_ref, g0 + c * ROW_GRAN, sem).start()
            else:
                _chunk_copy(hbm_ref, g0 + c * ROW_GRAN, buf_ref, o0 + c * ROW_GRAN, sem).start()
            return carry

        lax.fori_loop(0, n, start, 0)


def _segment_waits(b, nch_ref, buf_ref, sem):
    total = 0
    for e in range(N_EXPERTS):
        total = total + nch_ref[b * N_EXPERTS + e]

    def wait(c, carry):
        _chunk_copy(buf_ref, 0, buf_ref, 0, sem).wait()
        return carry

    lax.fori_loop(0, total, wait, 0)


def _dispatch_kernel(gstart_ref, nch_ref, off_ref, h_ref, rt_ref, xs_in_ref, xs_ref, buf_ref, sem):
    del xs_in_ref
    b = pl.program_id(0)
    slot = b % 2
    dest = rt_ref[...].T
    r = lax.broadcasted_iota(I32, (SORT_ROWS, TM), 0).astype(F32)
    perm = jnp.where((r == dest[0:1, :]) | (r == dest[1:2, :]), 1.0, 0.0).astype(BF16)
    buf_ref[slot] = _dot(perm, h_ref[...]).astype(BF16)
    _segment_starts(b, gstart_ref, nch_ref, off_ref, xs_ref, buf_ref.at[slot], sem.at[slot], to_hbm=True)

    @pl.when(b > 0)
    def _():
        _segment_waits(b - 1, nch_ref, buf_ref.at[1 - slot], sem.at[1 - slot])

    @pl.when(b == pl.num_programs(0) - 1)
    def _():
        _segment_waits(b, nch_ref, buf_ref.at[slot], sem.at[slot])


def _dispatch(gstart, nch, off, h2, rt, xs_zero):
    t, d = h2.shape
    grid_spec = pltpu.PrefetchScalarGridSpec(
        num_scalar_prefetch=3,
        grid=(t // TM,),
        in_specs=[pl.BlockSpec((TM, d), lambda i, *_: (i, 0)),
                  pl.BlockSpec((TM, LANES), lambda i, *_: (i, 0)),
                  pl.BlockSpec(memory_space=pl.ANY)],
        out_specs=pl.BlockSpec(memory_space=pl.ANY),
        scratch_shapes=[pltpu.VMEM((2, SORT_ROWS, d), BF16), pltpu.SemaphoreType.DMA((2,))],
    )
    return pl.pallas_call(
        _dispatch_kernel,
        grid_spec=grid_spec,
        out_shape=jax.ShapeDtypeStruct(xs_zero.shape, xs_zero.dtype),
        input_output_aliases={5: 0},
        compiler_params=_cparams(("arbitrary",)),
        name="moe_dispatch",
    )(gstart, nch, off, h2, rt, xs_zero)


def _group_ffn_kernel(te_ref, ta_ref, ts_ref, xs_ref, w1_ref, w3_ref, w2_ref, y_ref):
    del te_ref, ts_ref
    active = ta_ref[pl.program_id(0)] == 1

    @pl.when(active)
    def _():
        y_ref[...] = _swiglu_tile(xs_ref[...], w1_ref.at[0], w3_ref.at[0], w2_ref.at[0]).astype(BF16)

    @pl.when(jnp.logical_not(active))
    def _():
        y_ref[...] = jnp.zeros_like(y_ref)


def _group_ffn(tile_expert, tile_active, tile_src, xs, w1, w3, w2):
    ns, d = xs.shape
    ff = w1.shape[-1]
    grid_spec = pltpu.PrefetchScalarGridSpec(
        num_scalar_prefetch=3,
        grid=(ns // TMM,),
        in_specs=[pl.BlockSpec((TMM, d), lambda i, te, ta, ts: (ts[i], 0)),
                  pl.BlockSpec((1, d, ff), lambda i, te, ta, ts: (te[i], 0, 0)),
                  pl.BlockSpec((1, d, ff), lambda i, te, ta, ts: (te[i], 0, 0)),
                  pl.BlockSpec((1, ff, d), lambda i, te, ta, ts: (te[i], 0, 0))],
        out_specs=pl.BlockSpec((TMM, d), lambda i, te, ta, ts: (i, 0)),
    )
    return pl.pallas_call(
        _group_ffn_kernel,
        grid_spec=grid_spec,
        out_shape=jax.ShapeDtypeStruct((ns, d), BF16),
        compiler_params=_cparams(("arbitrary",)),
        name="moe_group_ffn",
    )(tile_expert, tile_active, tile_src, xs, w1, w3, w2)


def _combine_kernel(gstart_ref, nch_ref, off_ref, x_ref, rt_ref, gf_ref, y_ref, out_ref, buf_ref, sem,
                    *, final_norm):
    b = pl.program_id(0)
    slot = b % 2

    def fetch(blk, s):
        buf_ref[s] = jnp.zeros(buf_ref.shape[1:], BF16)
        _segment_starts(blk, gstart_ref, nch_ref, off_ref, y_ref, buf_ref.at[s], sem.at[s], to_hbm=False)

    @pl.when(b == 0)
    def _():
        fetch(b, slot)

    @pl.when(b + 1 < pl.num_programs(0))
    def _():
        fetch(b + 1, 1 - slot)

    _segment_waits(b, nch_ref, buf_ref.at[slot], sem.at[slot])
    rt = rt_ref[...]
    r = lax.broadcasted_iota(I32, (TM, SORT_ROWS), 1).astype(F32)
    ybuf = buf_ref[slot]
    y1 = _dot(jnp.where(r == rt[:, 0:1], 1.0, 0.0).astype(BF16), ybuf)
    y2 = _dot(jnp.where(r == rt[:, 1:2], 1.0, 0.0).astype(BF16), ybuf)
    x = x_ref[...] + rt[:, 2:3] * y1 + rt[:, 3:4] * y2
    out_ref[...] = _rms(x, gf_ref[...]) if final_norm else x


def _combine(gstart, nch, off, x2, rt, g_final, y, final_norm):
    t, d = x2.shape
    grid_spec = pltpu.PrefetchScalarGridSpec(
        num_scalar_prefetch=3,
        grid=(t // TM,),
        in_specs=[pl.BlockSpec((TM, d), lambda i, *_: (i, 0)),
                  pl.BlockSpec((TM, LANES), lambda i, *_: (i, 0)),
                  pl.BlockSpec((1, d), lambda i, *_: (0, 0)),
                  pl.BlockSpec(memory_space=pl.ANY)],
        out_specs=pl.BlockSpec((TM, d), lambda i, *_: (i, 0)),
        scratch_shapes=[pltpu.VMEM((2, SORT_ROWS, d), BF16), pltpu.SemaphoreType.DMA((2,))],
    )
    return pl.pallas_call(
        functools.partial(_combine_kernel, final_norm=final_norm),
        grid_spec=grid_spec,
        out_shape=jax.ShapeDtypeStruct((t, d), F32),
        compiler_params=_cparams(("arbitrary",)),
        name="moe_combine",
    )(gstart, nch, off, x2, rt, g_final, y)


def _norm_kernel(x_ref, g_ref, out_ref):
    out_ref[...] = _rms(x_ref[...], g_ref[...])


def _final_norm(x2, g):
    t, d = x2.shape
    row = pl.BlockSpec((TM, d), lambda i: (i, 0))
    return pl.pallas_call(
        _norm_kernel, grid=(t // TM,), in_specs=[row, _const_spec((1, d))], out_specs=row,
        out_shape=jax.ShapeDtypeStruct((t, d), F32), compiler_params=_cparams(("parallel",)),
        name="final_norm",
    )(x2, g)


def _tril_blocks(n, block, strict=False):
    r = jnp.arange(n)[:, None]
    c = jnp.arange(n)[None, :]
    keep = (c < r) if strict else (c <= r)
    keep = keep & ((r // block) == (c // block))
    return keep.astype(BF16)


def _moe_layer(x2, g_ffn, w_router, w1, w3, w2, g_final, final_norm):
    t, d = x2.shape
    nb = t // TM
    wr_pad = jnp.pad(w_router, ((0, 0), (0, LANES - N_EXPERTS)))
    h2, rt, seg = _router(x2, g_ffn.reshape(1, d), wr_pad, _tril_blocks(TM, TM, strict=True))

    pc = seg[:, 0, :N_EXPERTS].astype(I32)
    off = seg[:, 1, :N_EXPERTS].astype(I32)
    region = ((jnp.sum(pc, axis=0) + TMM - 1) // TMM) * TMM
    ends = jnp.cumsum(region)
    gstart = (ends - region)[None, :] + jnp.cumsum(pc, axis=0) - pc
    n_tiles = (2 * t + nb * N_EXPERTS * ROW_GRAN) // TMM + N_EXPERTS
    tile_start = jnp.arange(n_tiles, dtype=I32) * TMM
    tile_expert = jnp.minimum(jnp.sum(tile_start[:, None] >= ends[None, :], axis=1), N_EXPERTS - 1).astype(I32)
    tile_active = (tile_start < ends[-1]).astype(I32)
    last_active = jnp.maximum(ends[-1] // TMM - 1, 0)
    tile_src = jnp.minimum(jnp.arange(n_tiles, dtype=I32), last_active).astype(I32)
    seg_args = (gstart.reshape(-1).astype(I32), (pc // ROW_GRAN).reshape(-1), off.reshape(-1))

    xs = _dispatch(*seg_args, h2, rt, jnp.zeros((n_tiles * TMM, d), BF16))
    y = _group_ffn(tile_expert, tile_active, tile_src, xs, w1.astype(BF16), w3.astype(BF16), w2.astype(BF16))
    return _combine(*seg_args, x2, rt, g_final.reshape(1, d), y, final_norm)


def kernel(x, g_mix, w_in, b_f, conv_w, w_a2, b_a, g_gla, w_pa, w_pb, w_pc, w_gate, b_gate, w_o,
           g_ffn, ffn_w1, ffn_w3, ffn_w2, w_router, moe_w1, moe_w3, moe_w2, g_final):
    b, s, d = x.shape
    depth = g_mix.shape[0]
    t = b * s
    assert t % TM == 0 and s % TQ == 0 and TQ == TM and s % TL == 0 and TL % GLA_CHUNK == 0
    x2 = x.reshape(t, d)
    tril_full = _tril_blocks(TM, TM)
    tril_chunk = _tril_blocks(TL, GLA_CHUNK)
    o_af = 3 * A_WIDTH
    o_b = o_af + A_HEADS
    o_c = o_b + 3 * B_WIDTH
    o_ca = o_c + 2 * C_KEY_WIDTH + C_VAL_WIDTH
    o_cr = o_ca + C_LOWRANK

    done_final = False
    for l in range(depth):
        w = w_in[l]
        wq = w[:, 0:A_WIDTH] * (A_HEAD_DIM ** -0.5 * LOG2E)
        wbig = jnp.concatenate([wq, w[:, A_WIDTH:2 * A_WIDTH], w[:, o_b:o_c], w[:, o_c:o_ca], w[:, o_cr:]],
                               axis=1).astype(BF16)
        wvt = w[:, 2 * A_WIDTH:o_af].T.astype(BF16)
        wsmall = jnp.concatenate([w[:, o_af:o_b], w[:, o_ca:o_cr],
                                  jnp.zeros((d, LANES - A_HEADS - C_LOWRANK), F32)], axis=1).astype(BF16)
        gm = g_mix[l].reshape(1, d)
        qk, vt, conv_in, gla_in, small = _inproj(x2, gm, wbig, wvt, wsmall)

        bf_pad = jnp.pad(b_f[l], (0, LANES - A_HEADS)).reshape(1, LANES)
        qp, kp = _fcum(small.reshape(b, s, LANES), bf_pad, tril_full, qk)
        o_a = _attention(qp, kp, vt, b, s)

        convw8 = jnp.pad(conv_w[l], ((0, 8 - conv_w.shape[1]), (0, 0)))
        wa2p = jnp.zeros((LANES, C_KEY_WIDTH), F32).at[A_HEADS:A_HEADS + C_LOWRANK].set(w_a2[l]).astype(BF16)
        o_bm, o_cm = _mixers(conv_in, gla_in, small, convw8, wa2p, b_a[l].reshape(1, -1),
                             g_gla[l].reshape(1, -1), tril_chunk, b, s)

        x2 = _merge(x2, gm, o_a, o_bm, o_cm, w_gate[l].astype(BF16), b_gate[l].reshape(1, -1),
                    w_pa[l].astype(BF16), w_pb[l].astype(BF16), w_pc[l].astype(BF16), w_o[l].astype(BF16))

        i = l // 2
        if l % 2 == 0:
            x2 = _dense_ffn(x2, g_ffn[l].reshape(1, d), ffn_w1[i].astype(BF16), ffn_w3[i].astype(BF16),
                            ffn_w2[i].astype(BF16))
        else:
            done_final = l == depth - 1
            x2 = _moe_layer(x2, g_ffn[l], w_router[i], moe_w1[i], moe_w3[i], moe_w2[i], g_final, done_final)
    if not done_final:
        x2 = _final_norm(x2, g_final.reshape(1, d))
    return x2.reshape(b, s, d)
```

```python
import functools
import math

import numpy as np
import jax
import jax.numpy as jnp
from jax import lax
from jax.experimental import pallas as pl
from jax.experimental.pallas import tpu as pltpu

F32 = jnp.float32
BF16 = jnp.bfloat16
I32 = jnp.int32

EPS = 1e-6
NEG = -1e30
LOG2E = math.log2(math.e)

A_HEADS, A_HEAD_DIM = 8, 64
A_WIDTH = A_HEADS * A_HEAD_DIM
B_WIDTH = 512
C_HEADS, C_KEY_DIM, C_VAL_DIM = 4, 64, 128
C_KEY_WIDTH = C_HEADS * C_KEY_DIM
C_VAL_WIDTH = C_HEADS * C_VAL_DIM
C_LOWRANK = 16
C_GATE_TEMP = 16.0
GLA_CHUNK = 64
N_EXPERTS = 8
GROUP_W = 1536

LANES = 128
V7X_VMEM_BYTES = 64 * 1024 * 1024
VMEM_LIMIT = V7X_VMEM_BYTES - 8 * 1024 * 1024

TM = 512
TQ = 512
ATT_HEADS = 8
ACC_ROWS = A_HEAD_DIM + 16
TL = 256
MIX_BATCH = 2
FF_CHUNK = 256
TMM = 512
ROW_GRAN = 16
SORT_ROWS = 2 * TM + N_EXPERTS * ROW_GRAN


def _cparams(sem):
    return pltpu.CompilerParams(dimension_semantics=sem, vmem_limit_bytes=VMEM_LIMIT)


def _dot(a, b):
    return jnp.dot(a, b, preferred_element_type=F32)


def _dot_nt(a, b):
    return lax.dot_general(a, b, (((1,), (1,)), ((), ())), preferred_element_type=F32)


def _rms(x, g):
    return x * lax.rsqrt(jnp.mean(x * x, axis=-1, keepdims=True) + EPS) * g


def _log_sigmoid(z):
    return jnp.minimum(z, 0.0) - jnp.log(1.0 + jnp.exp(-jnp.abs(z)))


def _sigmoid(z):
    return 1.0 / (1.0 + jnp.exp(-z))


def _split3(x):
    hi = x.astype(BF16)
    r1 = x - hi.astype(F32)
    mid = r1.astype(BF16)
    lo = (r1 - mid.astype(F32)).astype(BF16)
    return hi, mid, lo


def _const_spec(shape):
    return pl.BlockSpec(shape, lambda *_: (0,) * len(shape))


def _inproj_kernel(x_ref, g_ref, wbig_ref, wvt_ref, wsmall_ref, qk_ref, vt_ref, conv_ref, gla_ref, small_ref):
    h = _rms(x_ref[...], g_ref[...]).astype(BF16)
    off = 0
    for ref in (qk_ref, conv_ref, gla_ref):
        width = ref.shape[1]
        for c in range(0, width, 512):
            ref[:, c:c + 512] = _dot(h, wbig_ref[:, off + c: off + c + 512]).astype(BF16)
        off += width
    vt_ref[0] = _dot_nt(wvt_ref[...], h).astype(BF16)
    small_ref[...] = _dot(h, wsmall_ref[...])


def _inproj(x2, g, wbig, wvt, wsmall):
    t, d = x2.shape
    row = lambda w: pl.BlockSpec((TM, w), lambda i: (i, 0))
    return pl.pallas_call(
        _inproj_kernel,
        grid=(t // TM,),
        in_specs=[row(d), _const_spec((1, d)), _const_spec(wbig.shape), _const_spec(wvt.shape),
                  _const_spec(wsmall.shape)],
        out_specs=[row(2 * A_WIDTH), pl.BlockSpec((1, A_WIDTH, TM), lambda i: (i, 0, 0)), row(GROUP_W), row(GROUP_W),
                   row(LANES)],
        out_shape=[jax.ShapeDtypeStruct((t, 2 * A_WIDTH), BF16), jax.ShapeDtypeStruct((t // TM, A_WIDTH, TM), BF16),
                   jax.ShapeDtypeStruct((t, GROUP_W), BF16), jax.ShapeDtypeStruct((t, GROUP_W), BF16),
                   jax.ShapeDtypeStruct((t, LANES), F32)],
        compiler_params=_cparams(("parallel",)),
        name="inproj",
    )(x2, g, wbig, wvt, wsmall)


def _fcum_kernel(small_ref, bf_ref, tril_ref, place_ref, qk_ref, qp_ref, kp_ref, carry_ref):
    @pl.when(pl.program_id(1) == 0)
    def _():
        carry_ref[...] = jnp.zeros_like(carry_ref)

    lf = LOG2E * _log_sigmoid(small_ref[0] + bf_ref[...])
    hi, mid, lo = _split3(lf)
    tril = tril_ref[...]
    cs = _dot(tril, hi) + _dot(tril, mid) + _dot(tril, lo) + carry_ref[...]
    carry_ref[...] = cs[TM - 1:TM, :]
    fterms = _dot(jnp.concatenate(_split3(cs), axis=1), place_ref[...])

    lane = lax.broadcasted_iota(I32, (1, LANES), 1)
    is_x = lane < A_HEAD_DIM
    is_a = (lane >= A_HEAD_DIM) & (lane < A_HEAD_DIM + 3)
    is_b = (lane >= A_HEAD_DIM + 3) & (lane < A_HEAD_DIM + 6)
    for p in range(A_HEADS // 2):
        q2 = qk_ref[:, p * LANES:(p + 1) * LANES].astype(F32)
        k2 = qk_ref[:, A_WIDTH + p * LANES: A_WIDTH + (p + 1) * LANES].astype(F32)
        halves = ((q2, k2), (pltpu.roll(q2, A_HEAD_DIM, 1), pltpu.roll(k2, A_HEAD_DIM, 1)))
        for j in range(2):
            h = 2 * p + j
            fh = fterms[:, h * LANES:(h + 1) * LANES]
            qh, kh = halves[j]
            qp_ref[:, h * LANES:(h + 1) * LANES] = jnp.where(
                is_x, qh, jnp.where(is_a, fh, jnp.where(is_b, 1.0, 0.0))).astype(BF16)
            kp_ref[:, h * LANES:(h + 1) * LANES] = jnp.where(
                is_x, kh, jnp.where(is_a, 1.0, jnp.where(is_b, -fh, 0.0))).astype(BF16)


def _aug_placement():
    place = np.zeros((3, LANES, A_HEADS * LANES), np.float32)
    for h in range(A_HEADS):
        for c in range(3):
            place[c, h, h * LANES + A_HEAD_DIM + c] = 1.0
            place[c, h, h * LANES + A_HEAD_DIM + 3 + c] = 1.0
    return jnp.asarray(place.reshape(3 * LANES, A_HEADS * LANES), BF16)


def _fcum(small3, bf_pad, tril, qk):
    b, s, _ = small3.shape
    ns = s // TM
    place = _aug_placement()
    tok = lambda bi, si: (bi * ns + si, 0)
    wide = pl.BlockSpec((TM, A_HEADS * LANES), tok)
    return pl.pallas_call(
        _fcum_kernel,
        grid=(b, ns),
        in_specs=[pl.BlockSpec((1, TM, LANES), lambda bi, si: (bi, si, 0)),
                  _const_spec((1, LANES)),
                  _const_spec((TM, TM)),
                  _const_spec(place.shape),
                  pl.BlockSpec((TM, 2 * A_WIDTH), tok)],
        out_specs=[wide, wide],
        out_shape=[jax.ShapeDtypeStruct((b * s, A_HEADS * LANES), BF16)] * 2,
        scratch_shapes=[pltpu.VMEM((1, LANES), F32)],
        compiler_params=_cparams(("parallel", "arbitrary")),
        name="fcum",
    )(small3, bf_pad, tril, place, qk)


def _attn_kernel(q_ref, k_ref, vt_ref, o_ref, m_ref, acc_ref):
    qi = pl.program_id(2)
    m_ref[...] = jnp.full(m_ref.shape, NEG, F32)
    acc_ref[...] = jnp.zeros_like(acc_ref)
    ones = jnp.ones((ACC_ROWS - A_HEAD_DIM, TQ), BF16)

    def _step(ki, diagonal):
        if diagonal:
            key = lax.broadcasted_iota(I32, (TQ, TQ), 0)
            qry = lax.broadcasted_iota(I32, (TQ, TQ), 1)
            keep = key <= qry
        scores = [_dot_nt(k_ref[ki, :, j * LANES:(j + 1) * LANES], q_ref[:, j * LANES:(j + 1) * LANES])
                  for j in range(ATT_HEADS)]
        for j in range(ATT_HEADS):
            st = scores[j]
            if diagonal:
                st = jnp.where(keep, st, NEG)
            m_prev = m_ref[j]
            m_new = jnp.maximum(m_prev, jnp.max(st, axis=0, keepdims=True))
            alpha = jnp.exp2(m_prev - m_new)
            pr = jnp.exp2(st - m_new).astype(BF16)
            vt_j = jnp.concatenate([vt_ref[ki, j * A_HEAD_DIM:(j + 1) * A_HEAD_DIM, :], ones], axis=0)
            acc_ref[j] = alpha * acc_ref[j] + _dot(vt_j, pr)
            m_ref[j] = m_new

    def _below(ki, carry):
        _step(ki, False)
        return carry

    lax.fori_loop(0, qi, _below, 0)
    _step(qi, True)
    ot = jnp.concatenate([acc_ref[j, 0:A_HEAD_DIM, :] / acc_ref[j, A_HEAD_DIM:A_HEAD_DIM + 1, :]
                          for j in range(ATT_HEADS)], axis=0)
    o_ref[...] = ot.T.astype(BF16)


def _attention(qp, kp, vt, b, s):
    nq = s // TQ
    ng = A_HEADS // ATT_HEADS
    kp3 = kp.reshape(b * nq, TQ, A_HEADS * LANES)
    return pl.pallas_call(
        _attn_kernel,
        grid=(b, ng, nq),
        in_specs=[pl.BlockSpec((TQ, ATT_HEADS * LANES), lambda bi, p, qi: (bi * nq + qi, p)),
                  pl.BlockSpec((nq, TQ, ATT_HEADS * LANES), lambda bi, p, qi: (bi, 0, p)),
                  pl.BlockSpec((nq, ATT_HEADS * A_HEAD_DIM, TQ), lambda bi, p, qi: (bi, p, 0))],
        out_specs=pl.BlockSpec((TQ, ATT_HEADS * A_HEAD_DIM), lambda bi, p, qi: (bi * nq + qi, p)),
        out_shape=jax.ShapeDtypeStruct((b * s, A_WIDTH), BF16),
        scratch_shapes=[pltpu.VMEM((ATT_HEADS, 1, TQ), F32), pltpu.VMEM((ATT_HEADS, ACC_ROWS, TQ), F32)],
        compiler_params=_cparams(("parallel", "parallel", "arbitrary")),
        name="fox_attention",
    )(qp, kp3, vt)


def _gla_kernel(gla_ref, small_ref, wa2_ref, ba_ref, ggla_ref, tri_ref, oc_ref, st_ref, sall_ref):
    @pl.when(pl.program_id(1) == 0)
    def _():
        st_ref[...] = jnp.zeros_like(st_ref)

    nc = TL // GLA_CHUNK
    nb = MIX_BATCH

    tri = tri_ref[...]
    q_t, k_t, ks_t, dec_t = [], [], [], []
    for i in range(nb):
        la = _dot(small_ref[i].astype(BF16), wa2_ref[...]) + ba_ref[...]
        la = _log_sigmoid(la) * (1.0 / C_GATE_TEMP)
        hi, mid, lo = _split3(la)
        bcum = _dot(tri, hi) + _dot(tri, mid) + _dot(tri, lo)
        bl = bcum.reshape(nc, GLA_CHUNK, C_KEY_WIDTH)[:, GLA_CHUNK - 1:GLA_CHUNK, :]
        blast = jnp.broadcast_to(bl, (nc, GLA_CHUNK, C_KEY_WIDTH)).reshape(TL, C_KEY_WIDTH)
        q = gla_ref[i, :, 0:C_KEY_WIDTH].astype(F32)
        k = gla_ref[i, :, C_KEY_WIDTH:2 * C_KEY_WIDTH].astype(F32)
        q_t.append(q * (C_KEY_DIM ** -0.5) * jnp.exp(bcum))
        k_t.append((k * jnp.exp(-bcum)).astype(BF16))
        ks_t.append((k * jnp.exp(blast - bcum)).T)
        dec_t.append(jnp.broadcast_to(jnp.exp(bl), (nc, LANES, C_KEY_WIDTH)).reshape(nc * LANES, C_KEY_WIDTH).T)

    r_i = lax.broadcasted_iota(I32, (TL, TL), 0)
    c_i = lax.broadcasted_iota(I32, (TL, TL), 1)
    same = (r_i // GLA_CHUNK) == (c_i // GLA_CHUNK)
    intra = same & (c_i <= r_i)
    lane_k = lax.broadcasted_iota(I32, (1, C_KEY_WIDTH), 1)
    lane = lax.broadcasted_iota(I32, (1, LANES), 1)

    for h in range(C_HEADS):
        for i in range(nb):
            v_h = gla_ref[i, :, 2 * C_KEY_WIDTH + h * C_VAL_DIM: 2 * C_KEY_WIDTH + (h + 1) * C_VAL_DIM]
            q_h = jnp.where(lane_k // C_KEY_DIM == h, q_t[i], 0.0).astype(BF16)
            a = jnp.where(intra, _dot_nt(q_h, k_t[i]), 0.0).astype(BF16)

            q2 = q_t[i][:, (h // 2) * LANES:(h // 2 + 1) * LANES]
            q2r = pltpu.roll(q2, C_KEY_DIM, 1)
            dup = jnp.where((lane < C_KEY_DIM) == (h % 2 == 0), q2, q2r)
            q_exp = jnp.where(same, jnp.concatenate([dup] * (TL // LANES), axis=1), 0.0).astype(BF16)

            ks_h = ks_t[i][h * C_KEY_DIM:(h + 1) * C_KEY_DIM, :]
            k_exp = jnp.where(same, jnp.concatenate([ks_h] * nc, axis=0), 0.0).astype(BF16)
            kv = _dot(k_exp, v_h)

            st = st_ref[i, h]
            for c in range(nc):
                sall_ref[i, h, c * GLA_CHUNK:(c + 1) * GLA_CHUNK, :] = st.astype(BF16)
                dec = dec_t[i][h * C_KEY_DIM:(h + 1) * C_KEY_DIM, c * LANES:(c + 1) * LANES]
                st = dec * st + kv[c * GLA_CHUNK:(c + 1) * GLA_CHUNK, :]
            st_ref[i, h] = st

            o = _dot(a, v_h) + _dot(q_exp, sall_ref[i, h])
            o = o * lax.rsqrt(jnp.mean(o * o, axis=-1, keepdims=True) + EPS)
            o = o * ggla_ref[:, h * C_VAL_DIM:(h + 1) * C_VAL_DIM]
            r = gla_ref[i, :, 4 * C_KEY_WIDTH + h * C_VAL_DIM: 4 * C_KEY_WIDTH + (h + 1) * C_VAL_DIM].astype(F32)
            oc_ref[i, :, h * C_VAL_DIM:(h + 1) * C_VAL_DIM] = (o * (r * _sigmoid(r))).astype(BF16)


def _gla(gla_in, small, wa2p, ba, ggla, tri_bd, b, s):
    ns = s // TL
    blk = lambda w: pl.BlockSpec((MIX_BATCH, TL, w), lambda bi, si: (bi, si, 0))
    oc = pl.pallas_call(
        _gla_kernel,
        grid=(b // MIX_BATCH, ns),
        in_specs=[blk(GROUP_W), blk(LANES),
                  _const_spec((LANES, C_KEY_WIDTH)),
                  _const_spec((1, C_KEY_WIDTH)),
                  _const_spec((1, C_VAL_WIDTH)),
                  _const_spec((TL, TL))],
        out_specs=blk(C_VAL_WIDTH),
        out_shape=jax.ShapeDtypeStruct((b, s, C_VAL_WIDTH), BF16),
        scratch_shapes=[pltpu.VMEM((MIX_BATCH, C_HEADS, C_KEY_DIM, C_VAL_DIM), F32),
                        pltpu.VMEM((MIX_BATCH, C_HEADS, TL, C_VAL_DIM), BF16)],
        compiler_params=_cparams(("parallel", "arbitrary")),
        name="gla",
    )(gla_in.reshape(b, s, GROUP_W), small.reshape(b, s, LANES), wa2p, ba, ggla, tri_bd)
    return oc.reshape(b * s, C_VAL_WIDTH)


def _merge_kernel(x_ref, g_ref, oa_ref, conv_ref, convw_ref, oc_ref, wg_ref, bgate_ref, wpa_ref, wpb_ref,
                  wpc_ref, wo_ref, out_ref, zprev_ref, *, tiles_per_seq):
    @pl.when(pl.program_id(0) % tiles_per_seq == 0)
    def _():
        zprev_ref[...] = jnp.zeros_like(zprev_ref)

    u = conv_ref[:, 0:B_WIDTH].astype(F32)
    bg = conv_ref[:, B_WIDTH:2 * B_WIDTH].astype(F32)
    cg = conv_ref[:, 2 * B_WIDTH:3 * B_WIDTH].astype(F32)
    z = cg * u
    row = lax.broadcasted_iota(I32, (TM, B_WIDTH), 0)
    zp = zprev_ref[...]
    z1 = jnp.where(row == 0, zp[7:8, :], pltpu.roll(z, 1, 0))
    z2 = jnp.where(row == 0, zp[6:7, :], jnp.where(row == 1, zp[7:8, :], pltpu.roll(z, 2, 0)))
    cw = convw_ref[...]
    o_b = (bg * (cw[0:1, :] * z2 + cw[1:2, :] * z1 + cw[2:3, :] * z)).astype(BF16)
    zprev_ref[...] = z[TM - 8:TM, :]

    x = x_ref[...]
    d = x.shape[1]
    h = _rms(x, g_ref[...]).astype(BF16)
    merged = None
    for j, (o, wp_ref) in enumerate(((oa_ref[...], wpa_ref), (o_b, wpb_ref), (oc_ref[...], wpc_ref))):
        gate = _sigmoid(_dot(h, wg_ref[:, j * d:(j + 1) * d]) + bgate_ref[:, j * d:(j + 1) * d])
        term = gate * _dot(o, wp_ref[...])
        merged = term if merged is None else merged + term
    out_ref[...] = x + _dot(merged.astype(BF16), wo_ref[...])


def _merge(x2, g, oa, conv_in, convw8, oc, wg, bgate, wpa, wpb, wpc, wo, tiles_per_seq):
    t, d = x2.shape
    row = lambda w: pl.BlockSpec((TM, w), lambda i: (i, 0))
    return pl.pallas_call(
        functools.partial(_merge_kernel, tiles_per_seq=tiles_per_seq),
        grid=(t // TM,),
        in_specs=[row(d), _const_spec((1, d)), row(A_WIDTH), row(GROUP_W), _const_spec((8, B_WIDTH)),
                  row(C_VAL_WIDTH), _const_spec(wg.shape), _const_spec(bgate.shape), _const_spec(wpa.shape),
                  _const_spec(wpb.shape), _const_spec(wpc.shape), _const_spec(wo.shape)],
        out_specs=row(d),
        out_shape=jax.ShapeDtypeStruct((t, d), F32),
        scratch_shapes=[pltpu.VMEM((8, B_WIDTH), F32)],
        compiler_params=_cparams(("arbitrary",)),
        name="merge_outproj",
    )(x2, g, oa, conv_in, convw8, oc, wg, bgate, wpa, wpb, wpc, wo)


def _swiglu_tile(h, w1_ref, w3_ref, w2_ref):
    ff = w1_ref.shape[-1]
    acc = None
    for c in range(0, ff, FF_CHUNK):
        a = _dot(h, w1_ref[:, c:c + FF_CHUNK])
        b = _dot(h, w3_ref[:, c:c + FF_CHUNK])
        g = (a * _sigmoid(a) * b).astype(BF16)
        part = _dot(g, w2_ref[c:c + FF_CHUNK, :])
        acc = part if acc is None else acc + part
    return acc


def _ffn_kernel(x_ref, g_ref, w1_ref, w3_ref, w2_ref, out_ref):
    x = x_ref[...]
    h = _rms(x, g_ref[...]).astype(BF16)
    out_ref[...] = x + _swiglu_tile(h, w1_ref, w3_ref, w2_ref)


def _dense_ffn(x2, g, w1, w3, w2):
    t, d = x2.shape
    row = pl.BlockSpec((TM, d), lambda i: (i, 0))
    return pl.pallas_call(
        _ffn_kernel,
        grid=(t // TM,),
        in_specs=[row, _const_spec((1, d)), _const_spec(w1.shape), _const_spec(w3.shape),
                  _const_spec(w2.shape)],
        out_specs=row,
        out_shape=jax.ShapeDtypeStruct((t, d), F32),
        compiler_params=_cparams(("parallel",)),
        name="dense_ffn",
    )(x2, g, w1, w3, w2)


def _router_kernel(x_ref, g_ref, wr_ref, stril_ref, h_ref, rt_ref, seg_ref):
    h = _rms(x_ref[...], g_ref[...])
    h_hi = h.astype(BF16)
    h_ref[...] = h_hi
    h_lo = (h - h_hi.astype(F32)).astype(BF16)
    wr = wr_ref[...]
    w_hi = wr.astype(BF16)
    w_lo = (wr - w_hi.astype(F32)).astype(BF16)
    both = _dot(h_hi, jnp.concatenate([w_hi, w_lo], axis=1))
    logits = both[:, 0:LANES] + both[:, LANES:2 * LANES] + _dot(h_lo, w_hi)

    lane = lax.broadcasted_iota(I32, (TM, LANES), 1)
    lane_f = lane.astype(F32)
    lg = jnp.where(lane < N_EXPERTS, logits, NEG)
    m1 = jnp.max(lg, axis=-1, keepdims=True)
    i1 = jnp.min(jnp.where(lg == m1, lane_f, float(LANES)), axis=-1, keepdims=True)
    oh1 = lane_f == i1
    lg2 = jnp.where(oh1, NEG, lg)
    m2 = jnp.max(lg2, axis=-1, keepdims=True)
    i2 = jnp.min(jnp.where(lg2 == m2, lane_f, float(LANES)), axis=-1, keepdims=True)
    oh2 = lane_f == i2
    e = jnp.exp(m2 - m1)
    w1 = 1.0 / (1.0 + e)
    w2 = e / (1.0 + e)

    sel = jnp.where(oh1 | oh2, 1.0, 0.0)
    before = _dot(stril_ref[...], sel.astype(BF16))
    cnt = jnp.sum(sel, axis=0, keepdims=True)
    pc = jnp.floor((cnt + (ROW_GRAN - 1)) * (1.0 / ROW_GRAN)) * ROW_GRAN
    lane1 = lax.broadcasted_iota(I32, (1, LANES), 1)
    incl = pc
    for sh in (1, 2, 4):
        incl = incl + jnp.where(lane1 >= sh, pltpu.roll(incl, sh, 1), 0.0)
    off = incl - pc
    pos = before + off
    d1 = jnp.sum(jnp.where(oh1, pos, 0.0), axis=-1, keepdims=True)
    d2 = jnp.sum(jnp.where(oh2, pos, 0.0), axis=-1, keepdims=True)
    rt_ref[...] = jnp.where(lane == 0, d1, jnp.where(lane == 1, d2, jnp.where(lane == 2, w1, jnp.where(lane == 3, w2, 0.0))))
    row8 = lax.broadcasted_iota(I32, (8, LANES), 0)
    seg_ref[0] = jnp.where(row8 == 0, pc, jnp.where(row8 == 1, off, 0.0))


def _router(x2, g, wr_pad, stril):
    t, d = x2.shape
    nb = t // TM
    row = lambda w: pl.BlockSpec((TM, w), lambda i: (i, 0))
    return pl.pallas_call(
        _router_kernel,
        grid=(nb,),
        in_specs=[row(d), _const_spec((1, d)), _const_spec(wr_pad.shape), _const_spec((TM, TM))],
        out_specs=[row(d), row(LANES), pl.BlockSpec((1, 8, LANES), lambda i: (i, 0, 0))],
        out_shape=[jax.ShapeDtypeStruct((t, d), BF16), jax.ShapeDtypeStruct((t, LANES), F32),
                   jax.ShapeDtypeStruct((nb, 8, LANES), F32)],
        compiler_params=_cparams(("parallel",)),
        name="moe_router",
    )(x2, g, wr_pad, stril)


def _chunk_copy(src_ref, src_row, dst_ref, dst_row, sem):
    return pltpu.make_async_copy(src_ref.at[pl.ds(pl.multiple_of(src_row, ROW_GRAN), ROW_GRAN)],
                                 dst_ref.at[pl.ds(pl.multiple_of(dst_row, ROW_GRAN), ROW_GRAN)], sem)


def _segment_starts(b, gstart_ref, nch_ref, off_ref, hbm_ref, buf_ref, sem, to_hbm):
    for e in range(N_EXPERTS):
        n = nch_ref[b * N_EXPERTS + e]
        g0 = gstart_ref[b * N_EXPERTS + e]
        o0 = off_ref[b * N_EXPERTS + e]

        def start(c, carry, g0=g0, o0=o0):
            if to_hbm:
                _chunk_copy(buf_ref, o0 + c * ROW_GRAN, hbm_ref, g0 + c * ROW_GRAN, sem).start()
            else:
                _chunk_copy(hbm_ref, g0 + c * ROW_GRAN, buf_ref, o0 + c * ROW_GRAN, sem).start()
            return carry

        lax.fori_loop(0, n, start, 0)


def _segment_waits(b, nch_ref, buf_ref, sem):
    total = 0
    for e in range(N_EXPERTS):
        total = total + nch_ref[b * N_EXPERTS + e]

    def wait(c, carry):
        _chunk_copy(buf_ref, 0, buf_ref, 0, sem).wait()
        return carry

    lax.fori_loop(0, total, wait, 0)


def _dispatch_kernel(gstart_ref, nch_ref, off_ref, h_ref, rt_ref, xs_in_ref, xs_ref, buf_ref, sem):
    del xs_in_ref
    b = pl.program_id(0)
    slot = b % 2
    dest = rt_ref[...].T
    r = lax.broadcasted_iota(I32, (SORT_ROWS, TM), 0).astype(F32)
    perm = jnp.where((r == dest[0:1, :]) | (r == dest[1:2, :]), 1.0, 0.0).astype(BF16)
    buf_ref[slot] = _dot(perm, h_ref[...]).astype(BF16)
    _segment_starts(b, gstart_ref, nch_ref, off_ref, xs_ref, buf_ref.at[slot], sem.at[slot], to_hbm=True)

    @pl.when(b > 0)
    def _():
        _segment_waits(b - 1, nch_ref, buf_ref.at[1 - slot], sem.at[1 - slot])

    @pl.when(b == pl.num_programs(0) - 1)
    def _():
        _segment_waits(b, nch_ref, buf_ref.at[slot], sem.at[slot])


def _dispatch(gstart, nch, off, h2, rt, xs_zero):
    t, d = h2.shape
    grid_spec = pltpu.PrefetchScalarGridSpec(
        num_scalar_prefetch=3,
        grid=(t // TM,),
        in_specs=[pl.BlockSpec((TM, d), lambda i, *_: (i, 0)),
                  pl.BlockSpec((TM, LANES), lambda i, *_: (i, 0)),
                  pl.BlockSpec(memory_space=pl.ANY)],
        out_specs=pl.BlockSpec(memory_space=pl.ANY),
        scratch_shapes=[pltpu.VMEM((2, SORT_ROWS, d), BF16), pltpu.SemaphoreType.DMA((2,))],
    )
    return pl.pallas_call(
        _dispatch_kernel,
        grid_spec=grid_spec,
        out_shape=jax.ShapeDtypeStruct(xs_zero.shape, xs_zero.dtype),
        input_output_aliases={5: 0},
        compiler_params=_cparams(("arbitrary",)),
        name="moe_dispatch",
    )(gstart, nch, off, h2, rt, xs_zero)


def _group_ffn_kernel(te_ref, ta_ref, ts_ref, xs_ref, w1_ref, w3_ref, w2_ref, y_ref):
    del te_ref, ts_ref
    active = ta_ref[pl.program_id(0)] == 1

    @pl.when(active)
    def _():
        y_ref[...] = _swiglu_tile(xs_ref[...], w1_ref.at[0], w3_ref.at[0], w2_ref.at[0]).astype(BF16)

    @pl.when(jnp.logical_not(active))
    def _():
        y_ref[...] = jnp.zeros_like(y_ref)


def _group_ffn(tile_expert, tile_active, tile_src, xs, w1, w3, w2):
    ns, d = xs.shape
    ff = w1.shape[-1]
    grid_spec = pltpu.PrefetchScalarGridSpec(
        num_scalar_prefetch=3,
        grid=(ns // TMM,),
        in_specs=[pl.BlockSpec((TMM, d), lambda i, te, ta, ts: (ts[i], 0)),
                  pl.BlockSpec((1, d, ff), lambda i, te, ta, ts: (te[i], 0, 0)),
                  pl.BlockSpec((1, d, ff), lambda i, te, ta, ts: (te[i], 0, 0)),
                  pl.BlockSpec((1, ff, d), lambda i, te, ta, ts: (te[i], 0, 0))],
        out_specs=pl.BlockSpec((TMM, d), lambda i, te, ta, ts: (i, 0)),
    )
    return pl.pallas_call(
        _group_ffn_kernel,
        grid_spec=grid_spec,
        out_shape=jax.ShapeDtypeStruct((ns, d), BF16),
        compiler_params=_cparams(("arbitrary",)),
        name="moe_group_ffn",
    )(tile_expert, tile_active, tile_src, xs, w1, w3, w2)


def _combine_kernel(gstart_ref, nch_ref, off_ref, x_ref, rt_ref, gf_ref, y_ref, out_ref, buf_ref, sem,
                    *, final_norm):
    b = pl.program_id(0)
    slot = b % 2

    def fetch(blk, s):
        buf_ref[s] = jnp.zeros(buf_ref.shape[1:], BF16)
        _segment_starts(blk, gstart_ref, nch_ref, off_ref, y_ref, buf_ref.at[s], sem.at[s], to_hbm=False)

    @pl.when(b == 0)
    def _():
        fetch(b, slot)

    @pl.when(b + 1 < pl.num_programs(0))
    def _():
        fetch(b + 1, 1 - slot)

    _segment_waits(b, nch_ref, buf_ref.at[slot], sem.at[slot])
    rt = rt_ref[...]
    r = lax.broadcasted_iota(I32, (TM, SORT_ROWS), 1).astype(F32)
    ybuf = buf_ref[slot]
    y1 = _dot(jnp.where(r == rt[:, 0:1], 1.0, 0.0).astype(BF16), ybuf)
    y2 = _dot(jnp.where(r == rt[:, 1:2], 1.0, 0.0).astype(BF16), ybuf)
    x = x_ref[...] + rt[:, 2:3] * y1 + rt[:, 3:4] * y2
    out_ref[...] = _rms(x, gf_ref[...]) if final_norm else x


def _combine(gstart, nch, off, x2, rt, g_final, y, final_norm):
    t, d = x2.shape
    grid_spec = pltpu.PrefetchScalarGridSpec(
        num_scalar_prefetch=3,
        grid=(t // TM,),
        in_specs=[pl.BlockSpec((TM, d), lambda i, *_: (i, 0)),
                  pl.BlockSpec((TM, LANES), lambda i, *_: (i, 0)),
                  pl.BlockSpec((1, d), lambda i, *_: (0, 0)),
                  pl.BlockSpec(memory_space=pl.ANY)],
        out_specs=pl.BlockSpec((TM, d), lambda i, *_: (i, 0)),
        scratch_shapes=[pltpu.VMEM((2, SORT_ROWS, d), BF16), pltpu.SemaphoreType.DMA((2,))],
    )
    return pl.pallas_call(
        functools.partial(_combine_kernel, final_norm=final_norm),
        grid_spec=grid_spec,
        out_shape=jax.ShapeDtypeStruct((t, d), F32),
        compiler_params=_cparams(("arbitrary",)),
        name="moe_combine",
    )(gstart, nch, off, x2, rt, g_final, y)


def _norm_kernel(x_ref, g_ref, out_ref):
    out_ref[...] = _rms(x_ref[...], g_ref[...])


def _final_norm(x2, g):
    t, d = x2.shape
    row = pl.BlockSpec((TM, d), lambda i: (i, 0))
    return pl.pallas_call(
        _norm_kernel, grid=(t // TM,), in_specs=[row, _const_spec((1, d))], out_specs=row,
        out_shape=jax.ShapeDtypeStruct((t, d), F32), compiler_params=_cparams(("parallel",)),
        name="final_norm",
    )(x2, g)


def _tril_blocks(n, block, strict=False):
    r = jnp.arange(n)[:, None]
    c = jnp.arange(n)[None, :]
    keep = (c < r) if strict else (c <= r)
    keep = keep & ((r // block) == (c // block))
    return keep.astype(BF16)


def _moe_layer(x2, g_ffn, w_router, w1, w3, w2, g_final, final_norm):
    t, d = x2.shape
    nb = t // TM
    wr_pad = jnp.pad(w_router, ((0, 0), (0, LANES - N_EXPERTS)))
    h2, rt, seg = _router(x2, g_ffn.reshape(1, d), wr_pad, _tril_blocks(TM, TM, strict=True))

    pc = seg[:, 0, :N_EXPERTS].astype(I32)
    off = seg[:, 1, :N_EXPERTS].astype(I32)
    region = ((jnp.sum(pc, axis=0) + TMM - 1) // TMM) * TMM
    ends = jnp.cumsum(region)
    gstart = (ends - region)[None, :] + jnp.cumsum(pc, axis=0) - pc
    n_tiles = (2 * t + nb * N_EXPERTS * ROW_GRAN) // TMM + N_EXPERTS
    tile_start = jnp.arange(n_tiles, dtype=I32) * TMM
    tile_expert = jnp.minimum(jnp.sum(tile_start[:, None] >= ends[None, :], axis=1), N_EXPERTS - 1).astype(I32)
    tile_active = (tile_start < ends[-1]).astype(I32)
    last_active = jnp.maximum(ends[-1] // TMM - 1, 0)
    tile_src = jnp.minimum(jnp.arange(n_tiles, dtype=I32), last_active).astype(I32)
    seg_args = (gstart.reshape(-1).astype(I32), (pc // ROW_GRAN).reshape(-1), off.reshape(-1))

    xs = _dispatch(*seg_args, h2, rt, jnp.zeros((n_tiles * TMM, d), BF16))
    y = _group_ffn(tile_expert, tile_active, tile_src, xs, w1.astype(BF16), w3.astype(BF16), w2.astype(BF16))
    return _combine(*seg_args, x2, rt, g_final.reshape(1, d), y, final_norm)


def kernel(x, g_mix, w_in, b_f, conv_w, w_a2, b_a, g_gla, w_pa, w_pb, w_pc, w_gate, b_gate, w_o,
           g_ffn, ffn_w1, ffn_w3, ffn_w2, w_router, moe_w1, moe_w3, moe_w2, g_final):
    b, s, d = x.shape
    depth = g_mix.shape[0]
    t = b * s
    assert t % TM == 0 and s % TQ == 0 and TQ == TM and s % TL == 0 and TL % GLA_CHUNK == 0 and b % MIX_BATCH == 0
    x2 = x.reshape(t, d)
    tril_full = _tril_blocks(TM, TM)
    tril_chunk = _tril_blocks(TL, GLA_CHUNK)
    o_af = 3 * A_WIDTH
    o_b = o_af + A_HEADS
    o_c = o_b + 3 * B_WIDTH
    o_ca = o_c + 2 * C_KEY_WIDTH + C_VAL_WIDTH
    o_cr = o_ca + C_LOWRANK

    done_final = False
    for l in range(depth):
        w = w_in[l]
        wq = w[:, 0:A_WIDTH] * (A_HEAD_DIM ** -0.5 * LOG2E)
        wbig = jnp.concatenate([wq, w[:, A_WIDTH:2 * A_WIDTH], w[:, o_b:o_c], w[:, o_c:o_ca], w[:, o_cr:]],
                               axis=1).astype(BF16)
        wvt = w[:, 2 * A_WIDTH:o_af].T.astype(BF16)
        wsmall = jnp.concatenate([w[:, o_af:o_b], w[:, o_ca:o_cr],
                                  jnp.zeros((d, LANES - A_HEADS - C_LOWRANK), F32)], axis=1).astype(BF16)
        gm = g_mix[l].reshape(1, d)
        qk, vt, conv_in, gla_in, small = _inproj(x2, gm, wbig, wvt, wsmall)

        bf_pad = jnp.pad(b_f[l], (0, LANES - A_HEADS)).reshape(1, LANES)
        qp, kp = _fcum(small.reshape(b, s, LANES), bf_pad, tril_full, qk)
        o_a = _attention(qp, kp, vt, b, s)

        convw8 = jnp.pad(conv_w[l], ((0, 8 - conv_w.shape[1]), (0, 0)))
        wa2p = jnp.zeros((LANES, C_KEY_WIDTH), F32).at[A_HEADS:A_HEADS + C_LOWRANK].set(w_a2[l]).astype(BF16)
        out_c = _gla(gla_in, small, wa2p, b_a[l].reshape(1, -1), g_gla[l].reshape(1, -1), tril_chunk, b, s)

        x2 = _merge(x2, gm, o_a, conv_in, convw8, out_c, w_gate[l].astype(BF16), b_gate[l].reshape(1, -1),
                    w_pa[l].astype(BF16), w_pb[l].astype(BF16), w_pc[l].astype(BF16), w_o[l].astype(BF16),
                    s // TM)

        i = l // 2
        if l % 2 == 0:
            x2 = _dense_ffn(x2, g_ffn[l].reshape(1, d), ffn_w1[i].astype(BF16), ffn_w3[i].astype(BF16),
                            ffn_w2[i].astype(BF16))
        else:
            done_final = l == depth - 1
            x2 = _moe_layer(x2, g_ffn[l], w_router[i], moe_w1[i], moe_w3[i], moe_w2[i], g_final, done_final)
    if not done_final:
        x2 = _final_norm(x2, g_final.reshape(1, d))
    return x2.reshape(b, s, d)
```

```python
import functools
import math

import numpy as np
import jax
import jax.numpy as jnp
from jax import lax
from jax.experimental import pallas as pl
from jax.experimental.pallas import tpu as pltpu

F32 = jnp.float32
BF16 = jnp.bfloat16
I32 = jnp.int32

EPS = 1e-6
NEG = -1e30
LOG2E = math.log2(math.e)

A_HEADS, A_HEAD_DIM = 8, 64
A_WIDTH = A_HEADS * A_HEAD_DIM
B_WIDTH = 512
C_HEADS, C_KEY_DIM, C_VAL_DIM = 4, 64, 128
C_KEY_WIDTH = C_HEADS * C_KEY_DIM
C_VAL_WIDTH = C_HEADS * C_VAL_DIM
C_LOWRANK = 16
C_GATE_TEMP = 16.0
GLA_CHUNK = 64
N_EXPERTS = 8
GROUP_W = 1536

LANES = 128
V7X_VMEM_BYTES = 64 * 1024 * 1024
VMEM_LIMIT = V7X_VMEM_BYTES - 8 * 1024 * 1024

TM = 512
TQ = 512
ATT_HEADS = 8
ACC_ROWS = A_HEAD_DIM + 16
ATT_GUARD = 64.0
TL = 256
MIX_BATCH = 2
FF_CHUNK = 256
TMM = 512
ROW_GRAN = 16
SORT_ROWS = 2 * TM + N_EXPERTS * ROW_GRAN


def _cparams(sem):
    return pltpu.CompilerParams(dimension_semantics=sem, vmem_limit_bytes=VMEM_LIMIT)


def _dot(a, b):
    return jnp.dot(a, b, preferred_element_type=F32)


def _dot_nt(a, b):
    return lax.dot_general(a, b, (((1,), (1,)), ((), ())), preferred_element_type=F32)


def _rms(x, g):
    return x * lax.rsqrt(jnp.mean(x * x, axis=-1, keepdims=True) + EPS) * g


def _log_sigmoid(z):
    return jnp.minimum(z, 0.0) - jnp.log(1.0 + jnp.exp(-jnp.abs(z)))


def _sigmoid(z):
    return 1.0 / (1.0 + jnp.exp(-z))


def _split3(x):
    hi = x.astype(BF16)
    r1 = x - hi.astype(F32)
    mid = r1.astype(BF16)
    lo = (r1 - mid.astype(F32)).astype(BF16)
    return hi, mid, lo


def _const_spec(shape):
    return pl.BlockSpec(shape, lambda *_: (0,) * len(shape))


def _inproj_kernel(x_ref, g_ref, wbig_ref, wvt_ref, wsmall_ref, qk_ref, vt_ref, conv_ref, gla_ref, small_ref):
    h = _rms(x_ref[...], g_ref[...]).astype(BF16)
    off = 0
    for ref in (qk_ref, conv_ref, gla_ref):
        width = ref.shape[1]
        for c in range(0, width, 512):
            ref[:, c:c + 512] = _dot(h, wbig_ref[:, off + c: off + c + 512]).astype(BF16)
        off += width
    vt_ref[0] = _dot_nt(wvt_ref[...], h).astype(BF16)
    small_ref[...] = _dot(h, wsmall_ref[...])


def _inproj(x2, g, wbig, wvt, wsmall):
    t, d = x2.shape
    row = lambda w: pl.BlockSpec((TM, w), lambda i: (i, 0))
    return pl.pallas_call(
        _inproj_kernel,
        grid=(t // TM,),
        in_specs=[row(d), _const_spec((1, d)), _const_spec(wbig.shape), _const_spec(wvt.shape),
                  _const_spec(wsmall.shape)],
        out_specs=[row(2 * A_WIDTH), pl.BlockSpec((1, A_WIDTH, TM), lambda i: (i, 0, 0)), row(GROUP_W), row(GROUP_W),
                   row(LANES)],
        out_shape=[jax.ShapeDtypeStruct((t, 2 * A_WIDTH), BF16), jax.ShapeDtypeStruct((t // TM, A_WIDTH, TM), BF16),
                   jax.ShapeDtypeStruct((t, GROUP_W), BF16), jax.ShapeDtypeStruct((t, GROUP_W), BF16),
                   jax.ShapeDtypeStruct((t, LANES), F32)],
        compiler_params=_cparams(("parallel",)),
        name="inproj",
    )(x2, g, wbig, wvt, wsmall)


def _fcum_kernel(small_ref, bf_ref, tril_ref, place_ref, qk_ref, qp_ref, kp_ref, carry_ref):
    @pl.when(pl.program_id(1) == 0)
    def _():
        carry_ref[...] = jnp.zeros_like(carry_ref)

    lf = LOG2E * _log_sigmoid(small_ref[0] + bf_ref[...])
    hi, mid, lo = _split3(lf)
    tril = tril_ref[...]
    cs = _dot(tril, hi) + _dot(tril, mid) + _dot(tril, lo) + carry_ref[...]
    carry_ref[...] = cs[TM - 1:TM, :]
    fterms = _dot(jnp.concatenate(_split3(cs), axis=1), place_ref[...])

    lane = lax.broadcasted_iota(I32, (1, LANES), 1)
    is_x = lane < A_HEAD_DIM
    is_a = (lane >= A_HEAD_DIM) & (lane < A_HEAD_DIM + 3)
    is_b = (lane >= A_HEAD_DIM + 3) & (lane < A_HEAD_DIM + 6)
    for p in range(A_HEADS // 2):
        q2 = qk_ref[:, p * LANES:(p + 1) * LANES].astype(F32)
        k2 = qk_ref[:, A_WIDTH + p * LANES: A_WIDTH + (p + 1) * LANES].astype(F32)
        halves = ((q2, k2), (pltpu.roll(q2, A_HEAD_DIM, 1), pltpu.roll(k2, A_HEAD_DIM, 1)))
        for j in range(2):
            h = 2 * p + j
            fh = fterms[:, h * LANES:(h + 1) * LANES]
            qh, kh = halves[j]
            qp_ref[:, h * LANES:(h + 1) * LANES] = jnp.where(
                is_x, qh, jnp.where(is_a, fh, jnp.where(is_b, 1.0, 0.0))).astype(BF16)
            kp_ref[:, h * LANES:(h + 1) * LANES] = jnp.where(
                is_x, kh, jnp.where(is_a, 1.0, jnp.where(is_b, -fh, 0.0))).astype(BF16)


def _aug_placement():
    place = np.zeros((3, LANES, A_HEADS * LANES), np.float32)
    for h in range(A_HEADS):
        for c in range(3):
            place[c, h, h * LANES + A_HEAD_DIM + c] = 1.0
            place[c, h, h * LANES + A_HEAD_DIM + 3 + c] = 1.0
    return jnp.asarray(place.reshape(3 * LANES, A_HEADS * LANES), BF16)


def _fcum(small3, bf_pad, tril, qk):
    b, s, _ = small3.shape
    ns = s // TM
    place = _aug_placement()
    tok = lambda bi, si: (bi * ns + si, 0)
    wide = pl.BlockSpec((TM, A_HEADS * LANES), tok)
    return pl.pallas_call(
        _fcum_kernel,
        grid=(b, ns),
        in_specs=[pl.BlockSpec((1, TM, LANES), lambda bi, si: (bi, si, 0)),
                  _const_spec((1, LANES)),
                  _const_spec((TM, TM)),
                  _const_spec(place.shape),
                  pl.BlockSpec((TM, 2 * A_WIDTH), tok)],
        out_specs=[wide, wide],
        out_shape=[jax.ShapeDtypeStruct((b * s, A_HEADS * LANES), BF16)] * 2,
        scratch_shapes=[pltpu.VMEM((1, LANES), F32)],
        compiler_params=_cparams(("parallel", "arbitrary")),
        name="fcum",
    )(small3, bf_pad, tril, place, qk)


def _attn_kernel(q_ref, k_ref, vt_ref, o_ref, m_ref, acc_ref, mbak_ref, abak_ref):
    qi = pl.program_id(2)
    ones = jnp.ones((ACC_ROWS - A_HEAD_DIM, TQ), BF16)

    def _score(ki, j):
        return _dot_nt(k_ref[ki, :, j * LANES:(j + 1) * LANES], q_ref[:, j * LANES:(j + 1) * LANES])

    def _values(ki, j):
        return jnp.concatenate([vt_ref[ki, j * A_HEAD_DIM:(j + 1) * A_HEAD_DIM, :], ones], axis=0)

    def _exact_step(ki, first):
        if first:
            key = lax.broadcasted_iota(I32, (TQ, TQ), 0)
            qry = lax.broadcasted_iota(I32, (TQ, TQ), 1)
            keep = key <= qry
        scores = [_score(ki, j) for j in range(ATT_HEADS)]
        for j in range(ATT_HEADS):
            st = scores[j]
            if first:
                st = jnp.where(keep, st, NEG)
                m_new = jnp.max(st, axis=0, keepdims=True)
                acc_ref[j] = _dot(_values(ki, j), jnp.exp2(st - m_new).astype(BF16))
            else:
                m_prev = m_ref[j]
                m_new = jnp.maximum(m_prev, jnp.max(st, axis=0, keepdims=True))
                pr = jnp.exp2(st - m_new).astype(BF16)
                acc_ref[j] = jnp.exp2(m_prev - m_new) * acc_ref[j] + _dot(_values(ki, j), pr)
            m_ref[j] = m_new

    def _fast_step(ki):
        excess = []
        pending = _score(ki, 0)
        for j in range(ATT_HEADS):
            st = pending
            if j + 1 < ATT_HEADS:
                pending = _score(ki, j + 1)
            m_prev = m_ref[j]
            acc_prev = acc_ref[j]
            mbak_ref[j] = m_prev
            abak_ref[j] = acc_prev
            pr = jnp.exp2(st - m_prev).astype(BF16)
            tmax = jnp.max(st, axis=0, keepdims=True)
            excess.append(jnp.max(tmax - m_prev))
            m_new = jnp.maximum(m_prev, tmax)
            acc_ref[j] = (acc_prev + _dot(_values(ki, j), pr)) * jnp.exp2(m_prev - m_new)
            m_ref[j] = m_new
        return functools.reduce(jnp.maximum, excess)

    _exact_step(qi, True)

    def _below(ki, carry):
        worst = _fast_step(ki)

        @pl.when(worst > ATT_GUARD)
        def _():
            m_ref[...] = mbak_ref[...]
            acc_ref[...] = abak_ref[...]
            _exact_step(ki, False)

        return carry

    lax.fori_loop(0, qi, _below, 0)
    ot = jnp.concatenate([acc_ref[j, 0:A_HEAD_DIM, :] / acc_ref[j, A_HEAD_DIM:A_HEAD_DIM + 1, :]
                          for j in range(ATT_HEADS)], axis=0)
    o_ref[...] = ot.T.astype(BF16)


def _attention(qp, kp, vt, b, s):
    nq = s // TQ
    ng = A_HEADS // ATT_HEADS
    kp3 = kp.reshape(b * nq, TQ, A_HEADS * LANES)
    return pl.pallas_call(
        _attn_kernel,
        grid=(b, ng, nq),
        in_specs=[pl.BlockSpec((TQ, ATT_HEADS * LANES), lambda bi, p, qi: (bi * nq + qi, p)),
                  pl.BlockSpec((nq, TQ, ATT_HEADS * LANES), lambda bi, p, qi: (bi, 0, p)),
                  pl.BlockSpec((nq, ATT_HEADS * A_HEAD_DIM, TQ), lambda bi, p, qi: (bi, p, 0))],
        out_specs=pl.BlockSpec((TQ, ATT_HEADS * A_HEAD_DIM), lambda bi, p, qi: (bi * nq + qi, p)),
        out_shape=jax.ShapeDtypeStruct((b * s, A_WIDTH), BF16),
        scratch_shapes=[pltpu.VMEM((ATT_HEADS, 1, TQ), F32), pltpu.VMEM((ATT_HEADS, ACC_ROWS, TQ), F32)] * 2,
        compiler_params=_cparams(("parallel", "parallel", "arbitrary")),
        name="fox_attention",
    )(qp, kp3, vt)


def _gla_kernel(gla_ref, small_ref, wa2_ref, ba_ref, ggla_ref, tri_ref, oc_ref, st_ref, sall_ref):
    @pl.when(pl.program_id(1) == 0)
    def _():
        st_ref[...] = jnp.zeros_like(st_ref)

    nc = TL // GLA_CHUNK
    nb = MIX_BATCH

    tri = tri_ref[...]
    q_t, k_t, ks_t, dec_t = [], [], [], []
    for i in range(nb):
        la = _dot(small_ref[i].astype(BF16), wa2_ref[...]) + ba_ref[...]
        la = _log_sigmoid(la) * (1.0 / C_GATE_TEMP)
        hi, mid, lo = _split3(la)
        bcum = _dot(tri, hi) + _dot(tri, mid) + _dot(tri, lo)
        bl = bcum.reshape(nc, GLA_CHUNK, C_KEY_WIDTH)[:, GLA_CHUNK - 1:GLA_CHUNK, :]
        blast = jnp.broadcast_to(bl, (nc, GLA_CHUNK, C_KEY_WIDTH)).reshape(TL, C_KEY_WIDTH)
        q = gla_ref[i, :, 0:C_KEY_WIDTH].astype(F32)
        k = gla_ref[i, :, C_KEY_WIDTH:2 * C_KEY_WIDTH].astype(F32)
        q_t.append(q * (C_KEY_DIM ** -0.5) * jnp.exp(bcum))
        k_t.append((k * jnp.exp(-bcum)).astype(BF16))
        ks_t.append((k * jnp.exp(blast - bcum)).T)
        dec_t.append(jnp.broadcast_to(jnp.exp(bl), (nc, LANES, C_KEY_WIDTH)).reshape(nc * LANES, C_KEY_WIDTH).T)

    r_i = lax.broadcasted_iota(I32, (TL, TL), 0)
    c_i = lax.broadcasted_iota(I32, (TL, TL), 1)
    same = (r_i // GLA_CHUNK) == (c_i // GLA_CHUNK)
    intra = same & (c_i <= r_i)
    lane_k = lax.broadcasted_iota(I32, (1, C_KEY_WIDTH), 1)
    lane = lax.broadcasted_iota(I32, (1, LANES), 1)

    for h in range(C_HEADS):
        for i in range(nb):
            v_h = gla_ref[i, :, 2 * C_KEY_WIDTH + h * C_VAL_DIM: 2 * C_KEY_WIDTH + (h + 1) * C_VAL_DIM]
            q_h = jnp.where(lane_k // C_KEY_DIM == h, q_t[i], 0.0).astype(BF16)
            a = jnp.where(intra, _dot_nt(q_h, k_t[i]), 0.0).astype(BF16)

            q2 = q_t[i][:, (h // 2) * LANES:(h // 2 + 1) * LANES]
            q2r = pltpu.roll(q2, C_KEY_DIM, 1)
            dup = jnp.where((lane < C_KEY_DIM) == (h % 2 == 0), q2, q2r)
            q_exp = jnp.where(same, jnp.concatenate([dup] * (TL // LANES), axis=1), 0.0).astype(BF16)

            ks_h = ks_t[i][h * C_KEY_DIM:(h + 1) * C_KEY_DIM, :]
            k_exp = jnp.where(same, jnp.concatenate([ks_h] * nc, axis=0), 0.0).astype(BF16)
            kv = _dot(k_exp, v_h)

            st = st_ref[i, h]
            for c in range(nc):
                sall_ref[i, h, c * GLA_CHUNK:(c + 1) * GLA_CHUNK, :] = st.astype(BF16)
                dec = dec_t[i][h * C_KEY_DIM:(h + 1) * C_KEY_DIM, c * LANES:(c + 1) * LANES]
                st = dec * st + kv[c * GLA_CHUNK:(c + 1) * GLA_CHUNK, :]
            st_ref[i, h] = st

            o = _dot(a, v_h) + _dot(q_exp, sall_ref[i, h])
            o = o * lax.rsqrt(jnp.mean(o * o, axis=-1, keepdims=True) + EPS)
            o = o * ggla_ref[:, h * C_VAL_DIM:(h + 1) * C_VAL_DIM]
            r = gla_ref[i, :, 4 * C_KEY_WIDTH + h * C_VAL_DIM: 4 * C_KEY_WIDTH + (h + 1) * C_VAL_DIM].astype(F32)
            oc_ref[i, :, h * C_VAL_DIM:(h + 1) * C_VAL_DIM] = (o * (r * _sigmoid(r))).astype(BF16)


def _gla(gla_in, small, wa2p, ba, ggla, tri_bd, b, s):
    ns = s // TL
    blk = lambda w: pl.BlockSpec((MIX_BATCH, TL, w), lambda bi, si: (bi, si, 0))
    oc = pl.pallas_call(
        _gla_kernel,
        grid=(b // MIX_BATCH, ns),
        in_specs=[blk(GROUP_W), blk(LANES),
                  _const_spec((LANES, C_KEY_WIDTH)),
                  _const_spec((1, C_KEY_WIDTH)),
                  _const_spec((1, C_VAL_WIDTH)),
                  _const_spec((TL, TL))],
        out_specs=blk(C_VAL_WIDTH),
        out_shape=jax.ShapeDtypeStruct((b, s, C_VAL_WIDTH), BF16),
        scratch_shapes=[pltpu.VMEM((MIX_BATCH, C_HEADS, C_KEY_DIM, C_VAL_DIM), F32),
                        pltpu.VMEM((MIX_BATCH, C_HEADS, TL, C_VAL_DIM), BF16)],
        compiler_params=_cparams(("parallel", "arbitrary")),
        name="gla",
    )(gla_in.reshape(b, s, GROUP_W), small.reshape(b, s, LANES), wa2p, ba, ggla, tri_bd)
    return oc.reshape(b * s, C_VAL_WIDTH)


def _merge_kernel(x_ref, g_ref, oa_ref, conv_ref, convw_ref, oc_ref, wg_ref, bgate_ref, wpa_ref, wpb_ref,
                  wpc_ref, wo_ref, out_ref, zprev_ref, *, tiles_per_seq):
    @pl.when(pl.program_id(0) % tiles_per_seq == 0)
    def _():
        zprev_ref[...] = jnp.zeros_like(zprev_ref)

    u = conv_ref[:, 0:B_WIDTH].astype(F32)
    bg = conv_ref[:, B_WIDTH:2 * B_WIDTH].astype(F32)
    cg = conv_ref[:, 2 * B_WIDTH:3 * B_WIDTH].astype(F32)
    z = cg * u
    row = lax.broadcasted_iota(I32, (TM, B_WIDTH), 0)
    zp = zprev_ref[...]
    z1 = jnp.where(row == 0, zp[7:8, :], pltpu.roll(z, 1, 0))
    z2 = jnp.where(row == 0, zp[6:7, :], jnp.where(row == 1, zp[7:8, :], pltpu.roll(z, 2, 0)))
    cw = convw_ref[...]
    o_b = (bg * (cw[0:1, :] * z2 + cw[1:2, :] * z1 + cw[2:3, :] * z)).astype(BF16)
    zprev_ref[...] = z[TM - 8:TM, :]

    x = x_ref[...]
    d = x.shape[1]
    h = _rms(x, g_ref[...]).astype(BF16)
    merged = None
    for j, (o, wp_ref) in enumerate(((oa_ref[...], wpa_ref), (o_b, wpb_ref), (oc_ref[...], wpc_ref))):
        gate = _sigmoid(_dot(h, wg_ref[:, j * d:(j + 1) * d]) + bgate_ref[:, j * d:(j + 1) * d])
        term = gate * _dot(o, wp_ref[...])
        merged = term if merged is None else merged + term
    out_ref[...] = x + _dot(merged.astype(BF16), wo_ref[...])


def _merge(x2, g, oa, conv_in, convw8, oc, wg, bgate, wpa, wpb, wpc, wo, tiles_per_seq):
    t, d = x2.shape
    row = lambda w: pl.BlockSpec((TM, w), lambda i: (i, 0))
    return pl.pallas_call(
        functools.partial(_merge_kernel, tiles_per_seq=tiles_per_seq),
        grid=(t // TM,),
        in_specs=[row(d), _const_spec((1, d)), row(A_WIDTH), row(GROUP_W), _const_spec((8, B_WIDTH)),
                  row(C_VAL_WIDTH), _const_spec(wg.shape), _const_spec(bgate.shape), _const_spec(wpa.shape),
                  _const_spec(wpb.shape), _const_spec(wpc.shape), _const_spec(wo.shape)],
        out_specs=row(d),
        out_shape=jax.ShapeDtypeStruct((t, d), F32),
        scratch_shapes=[pltpu.VMEM((8, B_WIDTH), F32)],
        compiler_params=_cparams(("arbitrary",)),
        name="merge_outproj",
    )(x2, g, oa, conv_in, convw8, oc, wg, bgate, wpa, wpb, wpc, wo)


def _swiglu_tile(h, w1_ref, w3_ref, w2_ref):
    ff = w1_ref.shape[-1]
    acc = None
    for c in range(0, ff, FF_CHUNK):
        a = _dot(h, w1_ref[:, c:c + FF_CHUNK])
        b = _dot(h, w3_ref[:, c:c + FF_CHUNK])
        g = (a * _sigmoid(a) * b).astype(BF16)
        part = _dot(g, w2_ref[c:c + FF_CHUNK, :])
        acc = part if acc is None else acc + part
    return acc


def _ffn_kernel(x_ref, g_ref, w1_ref, w3_ref, w2_ref, out_ref):
    x = x_ref[...]
    h = _rms(x, g_ref[...]).astype(BF16)
    out_ref[...] = x + _swiglu_tile(h, w1_ref, w3_ref, w2_ref)


def _dense_ffn(x2, g, w1, w3, w2):
    t, d = x2.shape
    row = pl.BlockSpec((TM, d), lambda i: (i, 0))
    return pl.pallas_call(
        _ffn_kernel,
        grid=(t // TM,),
        in_specs=[row, _const_spec((1, d)), _const_spec(w1.shape), _const_spec(w3.shape),
                  _const_spec(w2.shape)],
        out_specs=row,
        out_shape=jax.ShapeDtypeStruct((t, d), F32),
        compiler_params=_cparams(("parallel",)),
        name="dense_ffn",
    )(x2, g, w1, w3, w2)


def _router_kernel(x_ref, g_ref, wr_ref, stril_ref, h_ref, rt_ref, seg_ref):
    h = _rms(x_ref[...], g_ref[...])
    h_hi = h.astype(BF16)
    h_ref[...] = h_hi
    h_lo = (h - h_hi.astype(F32)).astype(BF16)
    wr = wr_ref[...]
    w_hi = wr.astype(BF16)
    w_lo = (wr - w_hi.astype(F32)).astype(BF16)
    both = _dot(h_hi, jnp.concatenate([w_hi, w_lo], axis=1))
    logits = both[:, 0:LANES] + both[:, LANES:2 * LANES] + _dot(h_lo, w_hi)

    lane = lax.broadcasted_iota(I32, (TM, LANES), 1)
    lane_f = lane.astype(F32)
    lg = jnp.where(lane < N_EXPERTS, logits, NEG)
    m1 = jnp.max(lg, axis=-1, keepdims=True)
    i1 = jnp.min(jnp.where(lg == m1, lane_f, float(LANES)), axis=-1, keepdims=True)
    oh1 = lane_f == i1
    lg2 = jnp.where(oh1, NEG, lg)
    m2 = jnp.max(lg2, axis=-1, keepdims=True)
    i2 = jnp.min(jnp.where(lg2 == m2, lane_f, float(LANES)), axis=-1, keepdims=True)
    oh2 = lane_f == i2
    e = jnp.exp(m2 - m1)
    w1 = 1.0 / (1.0 + e)
    w2 = e / (1.0 + e)

    sel = jnp.where(oh1 | oh2, 1.0, 0.0)
    before = _dot(stril_ref[...], sel.astype(BF16))
    cnt = jnp.sum(sel, axis=0, keepdims=True)
    pc = jnp.floor((cnt + (ROW_GRAN - 1)) * (1.0 / ROW_GRAN)) * ROW_GRAN
    lane1 = lax.broadcasted_iota(I32, (1, LANES), 1)
    incl = pc
    for sh in (1, 2, 4):
        incl = incl + jnp.where(lane1 >= sh, pltpu.roll(incl, sh, 1), 0.0)
    off = incl - pc
    pos = before + off
    d1 = jnp.sum(jnp.where(oh1, pos, 0.0), axis=-1, keepdims=True)
    d2 = jnp.sum(jnp.where(oh2, pos, 0.0), axis=-1, keepdims=True)
    rt_ref[...] = jnp.where(lane == 0, d1, jnp.where(lane == 1, d2, jnp.where(lane == 2, w1, jnp.where(lane == 3, w2, 0.0))))
    row8 = lax.broadcasted_iota(I32, (8, LANES), 0)
    seg_ref[0] = jnp.where(row8 == 0, pc, jnp.where(row8 == 1, off, 0.0))


def _router(x2, g, wr_pad, stril):
    t, d = x2.shape
    nb = t // TM
    row = lambda w: pl.BlockSpec((TM, w), lambda i: (i, 0))
    return pl.pallas_call(
        _router_kernel,
        grid=(nb,),
        in_specs=[row(d), _const_spec((1, d)), _const_spec(wr_pad.shape), _const_spec((TM, TM))],
        out_specs=[row(d), row(LANES), pl.BlockSpec((1, 8, LANES), lambda i: (i, 0, 0))],
        out_shape=[jax.ShapeDtypeStruct((t, d), BF16), jax.ShapeDtypeStruct((t, LANES), F32),
                   jax.ShapeDtypeStruct((nb, 8, LANES), F32)],
        compiler_params=_cparams(("parallel",)),
        name="moe_router",
    )(x2, g, wr_pad, stril)


def _chunk_copy(src_ref, src_row, dst_ref, dst_row, sem):
    return pltpu.make_async_copy(src_ref.at[pl.ds(pl.multiple_of(src_row, ROW_GRAN), ROW_GRAN)],
                                 dst_ref.at[pl.ds(pl.multiple_of(dst_row, ROW_GRAN), ROW_GRAN)], sem)


def _segment_starts(b, gstart_ref, nch_ref, off_ref, hbm_ref, buf_ref, sem, to_hbm):
    for e in range(N_EXPERTS):
        n = nch_ref[b * N_EXPERTS + e]
        g0 = gstart_ref[b * N_EXPERTS + e]
        o0 = off_ref[b * N_EXPERTS + e]

        def start(c, carry, g0=g0, o0=o0):
            if to_hbm:
                _chunk_copy(buf_ref, o0 + c * ROW_GRAN, hbm_ref, g0 + c * ROW_GRAN, sem).start()
            else:
                _chunk_copy(hbm_ref, g0 + c * ROW_GRAN, buf_ref, o0 + c * ROW_GRAN, sem).start()
            return carry

        lax.fori_loop(0, n, start, 0)


def _segment_waits(b, nch_ref, buf_ref, sem):
    total = 0
    for e in range(N_EXPERTS):
        total = total + nch_ref[b * N_EXPERTS + e]

    def wait(c, carry):
        _chunk_copy(buf_ref, 0, buf_ref, 0, sem).wait()
        return carry

    lax.fori_loop(0, total, wait, 0)


def _dispatch_kernel(gstart_ref, nch_ref, off_ref, h_ref, rt_ref, xs_in_ref, xs_ref, buf_ref, sem):
    del xs_in_ref
    b = pl.program_id(0)
    slot = b % 2
    dest = rt_ref[...].T
    r = lax.broadcasted_iota(I32, (SORT_ROWS, TM), 0).astype(F32)
    perm = jnp.where((r == dest[0:1, :]) | (r == dest[1:2, :]), 1.0, 0.0).astype(BF16)
    buf_ref[slot] = _dot(perm, h_ref[...]).astype(BF16)
    _segment_starts(b, gstart_ref, nch_ref, off_ref, xs_ref, buf_ref.at[slot], sem.at[slot], to_hbm=True)

    @pl.when(b > 0)
    def _():
        _segment_waits(b - 1, nch_ref, buf_ref.at[1 - slot], sem.at[1 - slot])

    @pl.when(b == pl.num_programs(0) - 1)
    def _():
        _segment_waits(b, nch_ref, buf_ref.at[slot], sem.at[slot])


def _dispatch(gstart, nch, off, h2, rt, xs_zero):
    t, d = h2.shape
    grid_spec = pltpu.PrefetchScalarGridSpec(
        num_scalar_prefetch=3,
        grid=(t // TM,),
        in_specs=[pl.BlockSpec((TM, d), lambda i, *_: (i, 0)),
                  pl.BlockSpec((TM, LANES), lambda i, *_: (i, 0)),
                  pl.BlockSpec(memory_space=pl.ANY)],
        out_specs=pl.BlockSpec(memory_space=pl.ANY),
        scratch_shapes=[pltpu.VMEM((2, SORT_ROWS, d), BF16), pltpu.SemaphoreType.DMA((2,))],
    )
    return pl.pallas_call(
        _dispatch_kernel,
        grid_spec=grid_spec,
        out_shape=jax.ShapeDtypeStruct(xs_zero.shape, xs_zero.dtype),
        input_output_aliases={5: 0},
        compiler_params=_cparams(("arbitrary",)),
        name="moe_dispatch",
    )(gstart, nch, off, h2, rt, xs_zero)


def _group_ffn_kernel(te_ref, ta_ref, ts_ref, xs_ref, w1_ref, w3_ref, w2_ref, y_ref):
    del te_ref, ts_ref
    active = ta_ref[pl.program_id(0)] == 1

    @pl.when(active)
    def _():
        y_ref[...] = _swiglu_tile(xs_ref[...], w1_ref.at[0], w3_ref.at[0], w2_ref.at[0]).astype(BF16)

    @pl.when(jnp.logical_not(active))
    def _():
        y_ref[...] = jnp.zeros_like(y_ref)


def _group_ffn(tile_expert, tile_active, tile_src, xs, w1, w3, w2):
    ns, d = xs.shape
    ff = w1.shape[-1]
    grid_spec = pltpu.PrefetchScalarGridSpec(
        num_scalar_prefetch=3,
        grid=(ns // TMM,),
        in_specs=[pl.BlockSpec((TMM, d), lambda i, te, ta, ts: (ts[i], 0)),
                  pl.BlockSpec((1, d, ff), lambda i, te, ta, ts: (te[i], 0, 0)),
                  pl.BlockSpec((1, d, ff), lambda i, te, ta, ts: (te[i], 0, 0)),
                  pl.BlockSpec((1, ff, d), lambda i, te, ta, ts: (te[i], 0, 0))],
        out_specs=pl.BlockSpec((TMM, d), lambda i, te, ta, ts: (i, 0)),
    )
    return pl.pallas_call(
        _group_ffn_kernel,
        grid_spec=grid_spec,
        out_shape=jax.ShapeDtypeStruct((ns, d), BF16),
        compiler_params=_cparams(("arbitrary",)),
        name="moe_group_ffn",
    )(tile_expert, tile_active, tile_src, xs, w1, w3, w2)


def _combine_kernel(gstart_ref, nch_ref, off_ref, x_ref, rt_ref, gf_ref, y_ref, out_ref, buf_ref, sem,
                    *, final_norm):
    b = pl.program_id(0)
    slot = b % 2

    def fetch(blk, s):
        buf_ref[s] = jnp.zeros(buf_ref.shape[1:], BF16)
        _segment_starts(blk, gstart_ref, nch_ref, off_ref, y_ref, buf_ref.at[s], sem.at[s], to_hbm=False)

    @pl.when(b == 0)
    def _():
        fetch(b, slot)

    @pl.when(b + 1 < pl.num_programs(0))
    def _():
        fetch(b + 1, 1 - slot)

    _segment_waits(b, nch_ref, buf_ref.at[slot], sem.at[slot])
    rt = rt_ref[...]
    r = lax.broadcasted_iota(I32, (TM, SORT_ROWS), 1).astype(F32)
    ybuf = buf_ref[slot]
    y1 = _dot(jnp.where(r == rt[:, 0:1], 1.0, 0.0).astype(BF16), ybuf)
    y2 = _dot(jnp.where(r == rt[:, 1:2], 1.0, 0.0).astype(BF16), ybuf)
    x = x_ref[...] + rt[:, 2:3] * y1 + rt[:, 3:4] * y2
    out_ref[...] = _rms(x, gf_ref[...]) if final_norm else x


def _combine(gstart, nch, off, x2, rt, g_final, y, final_norm):
    t, d = x2.shape
    grid_spec = pltpu.PrefetchScalarGridSpec(
        num_scalar_prefetch=3,
        grid=(t // TM,),
        in_specs=[pl.BlockSpec((TM, d), lambda i, *_: (i, 0)),
                  pl.BlockSpec((TM, LANES), lambda i, *_: (i, 0)),
                  pl.BlockSpec((1, d), lambda i, *_: (0, 0)),
                  pl.BlockSpec(memory_space=pl.ANY)],
        out_specs=pl.BlockSpec((TM, d), lambda i, *_: (i, 0)),
        scratch_shapes=[pltpu.VMEM((2, SORT_ROWS, d), BF16), pltpu.SemaphoreType.DMA((2,))],
    )
    return pl.pallas_call(
        functools.partial(_combine_kernel, final_norm=final_norm),
        grid_spec=grid_spec,
        out_shape=jax.ShapeDtypeStruct((t, d), F32),
        compiler_params=_cparams(("arbitrary",)),
        name="moe_combine",
    )(gstart, nch, off, x2, rt, g_final, y)


def _norm_kernel(x_ref, g_ref, out_ref):
    out_ref[...] = _rms(x_ref[...], g_ref[...])


def _final_norm(x2, g):
    t, d = x2.shape
    row = pl.BlockSpec((TM, d), lambda i: (i, 0))
    return pl.pallas_call(
        _norm_kernel, grid=(t // TM,), in_specs=[row, _const_spec((1, d))], out_specs=row,
        out_shape=jax.ShapeDtypeStruct((t, d), F32), compiler_params=_cparams(("parallel",)),
        name="final_norm",
    )(x2, g)


def _tril_blocks(n, block, strict=False):
    r = jnp.arange(n)[:, None]
    c = jnp.arange(n)[None, :]
    keep = (c < r) if strict else (c <= r)
    keep = keep & ((r // block) == (c // block))
    return keep.astype(BF16)


def _moe_layer(x2, g_ffn, w_router, w1, w3, w2, g_final, final_norm):
    t, d = x2.shape
    nb = t // TM
    wr_pad = jnp.pad(w_router, ((0, 0), (0, LANES - N_EXPERTS)))
    h2, rt, seg = _router(x2, g_ffn.reshape(1, d), wr_pad, _tril_blocks(TM, TM, strict=True))

    pc = seg[:, 0, :N_EXPERTS].astype(I32)
    off = seg[:, 1, :N_EXPERTS].astype(I32)
    region = ((jnp.sum(pc, axis=0) + TMM - 1) // TMM) * TMM
    ends = jnp.cumsum(region)
    gstart = (ends - region)[None, :] + jnp.cumsum(pc, axis=0) - pc
    n_tiles = (2 * t + nb * N_EXPERTS * ROW_GRAN) // TMM + N_EXPERTS
    tile_start = jnp.arange(n_tiles, dtype=I32) * TMM
    tile_expert = jnp.minimum(jnp.sum(tile_start[:, None] >= ends[None, :], axis=1), N_EXPERTS - 1).astype(I32)
    tile_active = (tile_start < ends[-1]).astype(I32)
    last_active = jnp.maximum(ends[-1] // TMM - 1, 0)
    tile_src = jnp.minimum(jnp.arange(n_tiles, dtype=I32), last_active).astype(I32)
    seg_args = (gstart.reshape(-1).astype(I32), (pc // ROW_GRAN).reshape(-1), off.reshape(-1))

    xs = _dispatch(*seg_args, h2, rt, jnp.zeros((n_tiles * TMM, d), BF16))
    y = _group_ffn(tile_expert, tile_active, tile_src, xs, w1.astype(BF16), w3.astype(BF16), w2.astype(BF16))
    return _combine(*seg_args, x2, rt, g_final.reshape(1, d), y, final_norm)


def kernel(x, g_mix, w_in, b_f, conv_w, w_a2, b_a, g_gla, w_pa, w_pb, w_pc, w_gate, b_gate, w_o,
           g_ffn, ffn_w1, ffn_w3, ffn_w2, w_router, moe_w1, moe_w3, moe_w2, g_final):
    b, s, d = x.shape
    depth = g_mix.shape[0]
    t = b * s
    assert t % TM == 0 and s % TQ == 0 and TQ == TM and s % TL == 0 and TL % GLA_CHUNK == 0 and b % MIX_BATCH == 0
    x2 = x.reshape(t, d)
    tril_full = _tril_blocks(TM, TM)
    tril_chunk = _tril_blocks(TL, GLA_CHUNK)
    o_af = 3 * A_WIDTH
    o_b = o_af + A_HEADS
    o_c = o_b + 3 * B_WIDTH
    o_ca = o_c + 2 * C_KEY_WIDTH + C_VAL_WIDTH
    o_cr = o_ca + C_LOWRANK

    done_final = False
    for l in range(depth):
        w = w_in[l]
        wq = w[:, 0:A_WIDTH] * (A_HEAD_DIM ** -0.5 * LOG2E)
        wbig = jnp.concatenate([wq, w[:, A_WIDTH:2 * A_WIDTH], w[:, o_b:o_c], w[:, o_c:o_ca], w[:, o_cr:]],
                               axis=1).astype(BF16)
        wvt = w[:, 2 * A_WIDTH:o_af].T.astype(BF16)
        wsmall = jnp.concatenate([w[:, o_af:o_b], w[:, o_ca:o_cr],
                                  jnp.zeros((d, LANES - A_HEADS - C_LOWRANK), F32)], axis=1).astype(BF16)
        gm = g_mix[l].reshape(1, d)
        qk, vt, conv_in, gla_in, small = _inproj(x2, gm, wbig, wvt, wsmall)

        bf_pad = jnp.pad(b_f[l], (0, LANES - A_HEADS)).reshape(1, LANES)
        qp, kp = _fcum(small.reshape(b, s, LANES), bf_pad, tril_full, qk)
        o_a = _attention(qp, kp, vt, b, s)

        convw8 = jnp.pad(conv_w[l], ((0, 8 - conv_w.shape[1]), (0, 0)))
        wa2p = jnp.zeros((LANES, C_KEY_WIDTH), F32).at[A_HEADS:A_HEADS + C_LOWRANK].set(w_a2[l]).astype(BF16)
        out_c = _gla(gla_in, small, wa2p, b_a[l].reshape(1, -1), g_gla[l].reshape(1, -1), tril_chunk, b, s)

        x2 = _merge(x2, gm, o_a, conv_in, convw8, out_c, w_gate[l].astype(BF16), b_gate[l].reshape(1, -1),
                    w_pa[l].astype(BF16), w_pb[l].astype(BF16), w_pc[l].astype(BF16), w_o[l].astype(BF16),
                    s // TM)

        i = l // 2
        if l % 2 == 0:
            x2 = _dense_ffn(x2, g_ffn[l].reshape(1, d), ffn_w1[i].astype(BF16), ffn_w3[i].astype(BF16),
                            ffn_w2[i].astype(BF16))
        else:
            done_final = l == depth - 1
            x2 = _moe_layer(x2, g_ffn[l], w_router[i], moe_w1[i], moe_w3[i], moe_w2[i], g_final, done_final)
    if not done_final:
        x2 = _final_norm(x2, g_final.reshape(1, d))
    return x2.reshape(b, s, d)
```

```python
import functools
import math

import numpy as np
import jax
import jax.numpy as jnp
from jax import lax
from jax.experimental import pallas as pl
from jax.experimental.pallas import tpu as pltpu

F32 = jnp.float32
BF16 = jnp.bfloat16
I32 = jnp.int32

EPS = 1e-6
NEG = -1e30
LOG2E = math.log2(math.e)

A_HEADS, A_HEAD_DIM = 8, 64
A_WIDTH = A_HEADS * A_HEAD_DIM
B_WIDTH = 512
C_HEADS, C_KEY_DIM, C_VAL_DIM = 4, 64, 128
C_KEY_WIDTH = C_HEADS * C_KEY_DIM
C_VAL_WIDTH = C_HEADS * C_VAL_DIM
C_LOWRANK = 16
C_GATE_TEMP = 16.0
GLA_CHUNK = 64
N_EXPERTS = 8
GROUP_W = 1536

LANES = 128
V7X_VMEM_BYTES = 64 * 1024 * 1024
VMEM_LIMIT = V7X_VMEM_BYTES - 8 * 1024 * 1024

TM = 512
TQ = 512
ATT_HEADS = 8
ACC_ROWS = A_HEAD_DIM + 16
ATT_GUARD = 64.0
TL = 256
MIX_BATCH = 2
FF_CHUNK = 256
TMM = 512
ROW_GRAN = 16
SORT_ROWS = 2 * TM + N_EXPERTS * ROW_GRAN


def _cparams(sem):
    return pltpu.CompilerParams(dimension_semantics=sem, vmem_limit_bytes=VMEM_LIMIT)


def _dot(a, b):
    return jnp.dot(a, b, preferred_element_type=F32)


def _dot_nt(a, b):
    return lax.dot_general(a, b, (((1,), (1,)), ((), ())), preferred_element_type=F32)


def _rms(x, g):
    return x * lax.rsqrt(jnp.mean(x * x, axis=-1, keepdims=True) + EPS) * g


def _log_sigmoid(z):
    return jnp.minimum(z, 0.0) - jnp.log(1.0 + jnp.exp(-jnp.abs(z)))


def _sigmoid(z):
    return 1.0 / (1.0 + jnp.exp(-z))


def _split3(x):
    hi = x.astype(BF16)
    r1 = x - hi.astype(F32)
    mid = r1.astype(BF16)
    lo = (r1 - mid.astype(F32)).astype(BF16)
    return hi, mid, lo


def _const_spec(shape):
    return pl.BlockSpec(shape, lambda *_: (0,) * len(shape))


def _inproj_kernel(x_ref, g_ref, wbig_ref, wvt_ref, wsmall_ref, bf_ref, tril_ref, place_ref,
                   qp_ref, kp_ref, vt_ref, conv_ref, gla_ref, small_ref, carry_ref, *, tiles_per_seq):
    @pl.when(pl.program_id(0) % tiles_per_seq == 0)
    def _():
        carry_ref[...] = jnp.zeros_like(carry_ref)

    h = _rms(x_ref[...], g_ref[...]).astype(BF16)
    small = _dot(h, wsmall_ref[...])
    small_ref[...] = small
    q_all = _dot(h, wbig_ref[:, 0:A_WIDTH])
    k_all = _dot(h, wbig_ref[:, A_WIDTH:2 * A_WIDTH])
    off = 2 * A_WIDTH
    for ref in (conv_ref, gla_ref):
        width = ref.shape[1]
        for c in range(0, width, 512):
            ref[:, c:c + 512] = _dot(h, wbig_ref[:, off + c: off + c + 512]).astype(BF16)
        off += width
    vt_ref[0] = _dot_nt(wvt_ref[...], h).astype(BF16)

    lf = LOG2E * _log_sigmoid(small + bf_ref[...])
    hi, mid, lo = _split3(lf)
    tril = tril_ref[...]
    cs = _dot(tril, hi) + _dot(tril, mid) + _dot(tril, lo) + carry_ref[...]
    carry_ref[...] = cs[TM - 1:TM, :]
    fterms = _dot(jnp.concatenate(_split3(cs), axis=1), place_ref[...])

    lane = lax.broadcasted_iota(I32, (1, LANES), 1)
    is_x = lane < A_HEAD_DIM
    is_a = (lane >= A_HEAD_DIM) & (lane < A_HEAD_DIM + 3)
    is_b = (lane >= A_HEAD_DIM + 3) & (lane < A_HEAD_DIM + 6)
    for p in range(A_HEADS // 2):
        q2 = q_all[:, p * LANES:(p + 1) * LANES].astype(BF16).astype(F32)
        k2 = k_all[:, p * LANES:(p + 1) * LANES].astype(BF16).astype(F32)
        halves = ((q2, k2), (pltpu.roll(q2, A_HEAD_DIM, 1), pltpu.roll(k2, A_HEAD_DIM, 1)))
        for j in range(2):
            hd = 2 * p + j
            fh = pltpu.roll(fterms, (A_HEAD_DIM - 8 * hd) % LANES, 1)
            qh, kh = halves[j]
            qp_ref[:, hd * LANES:(hd + 1) * LANES] = jnp.where(
                is_x, qh, jnp.where(is_a, fh, jnp.where(is_b, 1.0, 0.0))).astype(BF16)
            kp_ref[:, hd * LANES:(hd + 1) * LANES] = jnp.where(
                is_x, kh, jnp.where(is_a, 1.0, jnp.where(is_b, -fh, 0.0))).astype(BF16)


def _aug_placement():
    place = np.zeros((3, LANES, LANES), np.float32)
    for h in range(A_HEADS):
        for c in range(3):
            place[c, h, 8 * h + c] = 1.0
            place[c, h, 8 * h + 3 + c] = 1.0
    return jnp.asarray(place.reshape(3 * LANES, LANES), BF16)


def _inproj(x2, g, wbig, wvt, wsmall, bf_pad, tril, tiles_per_seq):
    t, d = x2.shape
    place = _aug_placement()
    row = lambda w: pl.BlockSpec((TM, w), lambda i: (i, 0))
    return pl.pallas_call(
        functools.partial(_inproj_kernel, tiles_per_seq=tiles_per_seq),
        grid=(t // TM,),
        in_specs=[row(d), _const_spec((1, d)), _const_spec(wbig.shape), _const_spec(wvt.shape),
                  _const_spec(wsmall.shape), _const_spec((1, LANES)), _const_spec((TM, TM)),
                  _const_spec(place.shape)],
        out_specs=[row(A_HEADS * LANES), row(A_HEADS * LANES),
                   pl.BlockSpec((1, A_WIDTH, TM), lambda i: (i, 0, 0)), row(GROUP_W), row(GROUP_W), row(LANES)],
        out_shape=[jax.ShapeDtypeStruct((t, A_HEADS * LANES), BF16), jax.ShapeDtypeStruct((t, A_HEADS * LANES), BF16),
                   jax.ShapeDtypeStruct((t // TM, A_WIDTH, TM), BF16),
                   jax.ShapeDtypeStruct((t, GROUP_W), BF16), jax.ShapeDtypeStruct((t, GROUP_W), BF16),
                   jax.ShapeDtypeStruct((t, LANES), F32)],
        scratch_shapes=[pltpu.VMEM((1, LANES), F32)],
        compiler_params=_cparams(("arbitrary",)),
        name="inproj",
    )(x2, g, wbig, wvt, wsmall, bf_pad, tril, place)


def _attn_kernel(q_ref, k_ref, vt_ref, o_ref, m_ref, acc_ref, exc_ref):
    qi = pl.program_id(2)
    ones = jnp.ones((ACC_ROWS - A_HEAD_DIM, TQ), BF16)

    def _score(ki, j):
        return _dot_nt(k_ref[ki, :, j * LANES:(j + 1) * LANES], q_ref[:, j * LANES:(j + 1) * LANES])

    def _values(ki, j):
        return jnp.concatenate([vt_ref[ki, j * A_HEAD_DIM:(j + 1) * A_HEAD_DIM, :], ones], axis=0)

    def _exact_step(ki, first):
        if first:
            key = lax.broadcasted_iota(I32, (TQ, TQ), 0)
            qry = lax.broadcasted_iota(I32, (TQ, TQ), 1)
            keep = key <= qry
        scores = [_score(ki, j) for j in range(ATT_HEADS)]
        for j in range(ATT_HEADS):
            st = scores[j]
            if first:
                st = jnp.where(keep, st, NEG)
                m_new = jnp.max(st, axis=0, keepdims=True)
                acc_ref[j] = _dot(_values(ki, j), jnp.exp2(st - m_new).astype(BF16))
            else:
                m_prev = m_ref[j]
                m_new = jnp.maximum(m_prev, jnp.max(st, axis=0, keepdims=True))
                pr = jnp.exp2(st - m_new).astype(BF16)
                acc_ref[j] = jnp.exp2(m_prev - m_new) * acc_ref[j] + _dot(_values(ki, j), pr)
            m_ref[j] = m_new

    def _fast_step(ki, carry):
        excess = exc_ref[...]
        pending = _score(ki, 0)
        for j in range(ATT_HEADS):
            st = pending
            if j + 1 < ATT_HEADS:
                pending = _score(ki, j + 1)
            m_prev = m_ref[j]
            pr = jnp.exp2(st - m_prev).astype(BF16)
            tmax = jnp.max(st, axis=0, keepdims=True)
            excess = jnp.maximum(excess, tmax - m_prev)
            m_new = jnp.maximum(m_prev, tmax)
            acc_ref[j] = (acc_ref[j] + _dot(_values(ki, j), pr)) * jnp.exp2(m_prev - m_new)
            m_ref[j] = m_new
        exc_ref[...] = excess
        return carry

    def _exact_below(ki, carry):
        _exact_step(ki, False)
        return carry

    exc_ref[...] = jnp.zeros_like(exc_ref)
    _exact_step(qi, True)
    lax.fori_loop(0, qi, _fast_step, 0)

    @pl.when(jnp.max(exc_ref[...]) > ATT_GUARD)
    def _():
        _exact_step(qi, True)
        lax.fori_loop(0, qi, _exact_below, 0)

    ot = jnp.concatenate([acc_ref[j, 0:A_HEAD_DIM, :] / acc_ref[j, A_HEAD_DIM:A_HEAD_DIM + 1, :]
                          for j in range(ATT_HEADS)], axis=0)
    o_ref[...] = ot.T.astype(BF16)


def _attention(qp, kp, vt, b, s):
    nq = s // TQ
    ng = A_HEADS // ATT_HEADS
    kp3 = kp.reshape(b * nq, TQ, A_HEADS * LANES)
    return pl.pallas_call(
        _attn_kernel,
        grid=(b, ng, nq),
        in_specs=[pl.BlockSpec((TQ, ATT_HEADS * LANES), lambda bi, p, qi: (bi * nq + qi, p)),
                  pl.BlockSpec((nq, TQ, ATT_HEADS * LANES), lambda bi, p, qi: (bi, 0, p)),
                  pl.BlockSpec((nq, ATT_HEADS * A_HEAD_DIM, TQ), lambda bi, p, qi: (bi, p, 0))],
        out_specs=pl.BlockSpec((TQ, ATT_HEADS * A_HEAD_DIM), lambda bi, p, qi: (bi * nq + qi, p)),
        out_shape=jax.ShapeDtypeStruct((b * s, A_WIDTH), BF16),
        scratch_shapes=[pltpu.VMEM((ATT_HEADS, 1, TQ), F32), pltpu.VMEM((ATT_HEADS, ACC_ROWS, TQ), F32),
                        pltpu.VMEM((1, TQ), F32)],
        compiler_params=_cparams(("parallel", "parallel", "arbitrary")),
        name="fox_attention",
    )(qp, kp3, vt)


def _gla_kernel(gla_ref, small_ref, wa2_ref, ba_ref, ggla_ref, tri_ref, oc_ref, st_ref, sall_ref):
    @pl.when(pl.program_id(1) == 0)
    def _():
        st_ref[...] = jnp.zeros_like(st_ref)

    nc = TL // GLA_CHUNK
    nb = MIX_BATCH

    tri = tri_ref[...]
    q_t, k_t, ks_t, dec_t = [], [], [], []
    for i in range(nb):
        la = _dot(small_ref[i].astype(BF16), wa2_ref[...]) + ba_ref[...]
        la = _log_sigmoid(la) * (1.0 / C_GATE_TEMP)
        hi, mid, lo = _split3(la)
        bcum = _dot(tri, hi) + _dot(tri, mid) + _dot(tri, lo)
        bl = bcum.reshape(nc, GLA_CHUNK, C_KEY_WIDTH)[:, GLA_CHUNK - 1:GLA_CHUNK, :]
        blast = jnp.broadcast_to(bl, (nc, GLA_CHUNK, C_KEY_WIDTH)).reshape(TL, C_KEY_WIDTH)
        q = gla_ref[i, :, 0:C_KEY_WIDTH].astype(F32)
        k = gla_ref[i, :, C_KEY_WIDTH:2 * C_KEY_WIDTH].astype(F32)
        q_t.append(q * (C_KEY_DIM ** -0.5) * jnp.exp(bcum))
        k_t.append((k * jnp.exp(-bcum)).astype(BF16))
        ks_t.append((k * jnp.exp(blast - bcum)).T)
        dec_t.append(jnp.broadcast_to(jnp.exp(bl), (nc, LANES, C_KEY_WIDTH)).reshape(nc * LANES, C_KEY_WIDTH).T)

    r_i = lax.broadcasted_iota(I32, (TL, TL), 0)
    c_i = lax.broadcasted_iota(I32, (TL, TL), 1)
    same = (r_i // GLA_CHUNK) == (c_i // GLA_CHUNK)
    intra = same & (c_i <= r_i)
    lane_k = lax.broadcasted_iota(I32, (1, C_KEY_WIDTH), 1)
    lane = lax.broadcasted_iota(I32, (1, LANES), 1)

    for h in range(C_HEADS):
        for i in range(nb):
            v_h = gla_ref[i, :, 2 * C_KEY_WIDTH + h * C_VAL_DIM: 2 * C_KEY_WIDTH + (h + 1) * C_VAL_DIM]
            q_h = jnp.where(lane_k // C_KEY_DIM == h, q_t[i], 0.0).astype(BF16)
            a = jnp.where(intra, _dot_nt(q_h, k_t[i]), 0.0).astype(BF16)

            q2 = q_t[i][:, (h // 2) * LANES:(h // 2 + 1) * LANES]
            q2r = pltpu.roll(q2, C_KEY_DIM, 1)
            dup = jnp.where((lane < C_KEY_DIM) == (h % 2 == 0), q2, q2r)
            q_exp = jnp.where(same, jnp.concatenate([dup] * (TL // LANES), axis=1), 0.0).astype(BF16)

            ks_h = ks_t[i][h * C_KEY_DIM:(h + 1) * C_KEY_DIM, :]
            k_exp = jnp.where(same, jnp.concatenate([ks_h] * nc, axis=0), 0.0).astype(BF16)
            kv = _dot(k_exp, v_h)

            st = st_ref[i, h]
            for c in range(nc):
                sall_ref[i, h, c * GLA_CHUNK:(c + 1) * GLA_CHUNK, :] = st.astype(BF16)
                dec = dec_t[i][h * C_KEY_DIM:(h + 1) * C_KEY_DIM, c * LANES:(c + 1) * LANES]
                st = dec * st + kv[c * GLA_CHUNK:(c + 1) * GLA_CHUNK, :]
            st_ref[i, h] = st

            o = _dot(a, v_h) + _dot(q_exp, sall_ref[i, h])
            o = o * lax.rsqrt(jnp.mean(o * o, axis=-1, keepdims=True) + EPS)
            o = o * ggla_ref[:, h * C_VAL_DIM:(h + 1) * C_VAL_DIM]
            r = gla_ref[i, :, 4 * C_KEY_WIDTH + h * C_VAL_DIM: 4 * C_KEY_WIDTH + (h + 1) * C_VAL_DIM].astype(F32)
            oc_ref[i, :, h * C_VAL_DIM:(h + 1) * C_VAL_DIM] = (o * (r * _sigmoid(r))).astype(BF16)


def _gla(gla_in, small, wa2p, ba, ggla, tri_bd, b, s):
    ns = s // TL
    blk = lambda w: pl.BlockSpec((MIX_BATCH, TL, w), lambda bi, si: (bi, si, 0))
    oc = pl.pallas_call(
        _gla_kernel,
        grid=(b // MIX_BATCH, ns),
        in_specs=[blk(GROUP_W), blk(LANES),
                  _const_spec((LANES, C_KEY_WIDTH)),
                  _const_spec((1, C_KEY_WIDTH)),
                  _const_spec((1, C_VAL_WIDTH)),
                  _const_spec((TL, TL))],
        out_specs=blk(C_VAL_WIDTH),
        out_shape=jax.ShapeDtypeStruct((b, s, C_VAL_WIDTH), BF16),
        scratch_shapes=[pltpu.VMEM((MIX_BATCH, C_HEADS, C_KEY_DIM, C_VAL_DIM), F32),
                        pltpu.VMEM((MIX_BATCH, C_HEADS, TL, C_VAL_DIM), BF16)],
        compiler_params=_cparams(("parallel", "arbitrary")),
        name="gla",
    )(gla_in.reshape(b, s, GROUP_W), small.reshape(b, s, LANES), wa2p, ba, ggla, tri_bd)
    return oc.reshape(b * s, C_VAL_WIDTH)


def _merge_kernel(x_ref, g_ref, oa_ref, conv_ref, convw_ref, oc_ref, wg_ref, bgate_ref, wpa_ref, wpb_ref,
                  wpc_ref, wo_ref, out_ref, zprev_ref, *, tiles_per_seq):
    @pl.when(pl.program_id(0) % tiles_per_seq == 0)
    def _():
        zprev_ref[...] = jnp.zeros_like(zprev_ref)

    u = conv_ref[:, 0:B_WIDTH].astype(F32)
    bg = conv_ref[:, B_WIDTH:2 * B_WIDTH].astype(F32)
    cg = conv_ref[:, 2 * B_WIDTH:3 * B_WIDTH].astype(F32)
    z = cg * u
    row = lax.broadcasted_iota(I32, (TM, B_WIDTH), 0)
    zp = zprev_ref[...]
    z1 = jnp.where(row == 0, zp[7:8, :], pltpu.roll(z, 1, 0))
    z2 = jnp.where(row == 0, zp[6:7, :], jnp.where(row == 1, zp[7:8, :], pltpu.roll(z, 2, 0)))
    cw = convw_ref[...]
    o_b = (bg * (cw[0:1, :] * z2 + cw[1:2, :] * z1 + cw[2:3, :] * z)).astype(BF16)
    zprev_ref[...] = z[TM - 8:TM, :]

    x = x_ref[...]
    d = x.shape[1]
    h = _rms(x, g_ref[...]).astype(BF16)
    merged = None
    for j, (o, wp_ref) in enumerate(((oa_ref[...], wpa_ref), (o_b, wpb_ref), (oc_ref[...], wpc_ref))):
        gate = _sigmoid(_dot(h, wg_ref[:, j * d:(j + 1) * d]) + bgate_ref[:, j * d:(j + 1) * d])
        term = gate * _dot(o, wp_ref[...])
        merged = term if merged is None else merged + term
    out_ref[...] = x + _dot(merged.astype(BF16), wo_ref[...])


def _merge(x2, g, oa, conv_in, convw8, oc, wg, bgate, wpa, wpb, wpc, wo, tiles_per_seq):
    t, d = x2.shape
    row = lambda w: pl.BlockSpec((TM, w), lambda i: (i, 0))
    return pl.pallas_call(
        functools.partial(_merge_kernel, tiles_per_seq=tiles_per_seq),
        grid=(t // TM,),
        in_specs=[row(d), _const_spec((1, d)), row(A_WIDTH), row(GROUP_W), _const_spec((8, B_WIDTH)),
                  row(C_VAL_WIDTH), _const_spec(wg.shape), _const_spec(bgate.shape), _const_spec(wpa.shape),
                  _const_spec(wpb.shape), _const_spec(wpc.shape), _const_spec(wo.shape)],
        out_specs=row(d),
        out_shape=jax.ShapeDtypeStruct((t, d), F32),
        scratch_shapes=[pltpu.VMEM((8, B_WIDTH), F32)],
        compiler_params=_cparams(("arbitrary",)),
        name="merge_outproj",
    )(x2, g, oa, conv_in, convw8, oc, wg, bgate, wpa, wpb, wpc, wo)


def _swiglu_tile(h, w1_ref, w3_ref, w2_ref):
    ff = w1_ref.shape[-1]
    acc = None
    for c in range(0, ff, FF_CHUNK):
        a = _dot(h, w1_ref[:, c:c + FF_CHUNK])
        b = _dot(h, w3_ref[:, c:c + FF_CHUNK])
        g = (a * _sigmoid(a) * b).astype(BF16)
        part = _dot(g, w2_ref[c:c + FF_CHUNK, :])
        acc = part if acc is None else acc + part
    return acc


def _ffn_kernel(x_ref, g_ref, w1_ref, w3_ref, w2_ref, out_ref):
    x = x_ref[...]
    h = _rms(x, g_ref[...]).astype(BF16)
    out_ref[...] = x + _swiglu_tile(h, w1_ref, w3_ref, w2_ref)


def _dense_ffn(x2, g, w1, w3, w2):
    t, d = x2.shape
    row = pl.BlockSpec((TM, d), lambda i: (i, 0))
    return pl.pallas_call(
        _ffn_kernel,
        grid=(t // TM,),
        in_specs=[row, _const_spec((1, d)), _const_spec(w1.shape), _const_spec(w3.shape),
                  _const_spec(w2.shape)],
        out_specs=row,
        out_shape=jax.ShapeDtypeStruct((t, d), F32),
        compiler_params=_cparams(("parallel",)),
        name="dense_ffn",
    )(x2, g, w1, w3, w2)


def _router_kernel(x_ref, g_ref, wr_ref, stril_ref, h_ref, rt_ref, seg_ref):
    h = _rms(x_ref[...], g_ref[...])
    h_hi = h.astype(BF16)
    h_ref[...] = h_hi
    h_lo = (h - h_hi.astype(F32)).astype(BF16)
    wr = wr_ref[...]
    w_hi = wr.astype(BF16)
    w_lo = (wr - w_hi.astype(F32)).astype(BF16)
    both = _dot(h_hi, jnp.concatenate([w_hi, w_lo], axis=1))
    logits = both[:, 0:LANES] + both[:, LANES:2 * LANES] + _dot(h_lo, w_hi)

    lane = lax.broadcasted_iota(I32, (TM, LANES), 1)
    lane_f = lane.astype(F32)
    lg = jnp.where(lane < N_EXPERTS, logits, NEG)
    m1 = jnp.max(lg, axis=-1, keepdims=True)
    i1 = jnp.min(jnp.where(lg == m1, lane_f, float(LANES)), axis=-1, keepdims=True)
    oh1 = lane_f == i1
    lg2 = jnp.where(oh1, NEG, lg)
    m2 = jnp.max(lg2, axis=-1, keepdims=True)
    i2 = jnp.min(jnp.where(lg2 == m2, lane_f, float(LANES)), axis=-1, keepdims=True)
    oh2 = lane_f == i2
    e = jnp.exp(m2 - m1)
    w1 = 1.0 / (1.0 + e)
    w2 = e / (1.0 + e)

    sel = jnp.where(oh1 | oh2, 1.0, 0.0)
    before = _dot(stril_ref[...], sel.astype(BF16))
    cnt = jnp.sum(sel, axis=0, keepdims=True)
    pc = jnp.floor((cnt + (ROW_GRAN - 1)) * (1.0 / ROW_GRAN)) * ROW_GRAN
    lane1 = lax.broadcasted_iota(I32, (1, LANES), 1)
    incl = pc
    for sh in (1, 2, 4):
        incl = incl + jnp.where(lane1 >= sh, pltpu.roll(incl, sh, 1), 0.0)
    off = incl - pc
    pos = before + off
    d1 = jnp.sum(jnp.where(oh1, pos, 0.0), axis=-1, keepdims=True)
    d2 = jnp.sum(jnp.where(oh2, pos, 0.0), axis=-1, keepdims=True)
    rt_ref[...] = jnp.where(lane == 0, d1, jnp.where(lane == 1, d2, jnp.where(lane == 2, w1, jnp.where(lane == 3, w2, 0.0))))
    row8 = lax.broadcasted_iota(I32, (8, LANES), 0)
    seg_ref[0] = jnp.where(row8 == 0, pc, jnp.where(row8 == 1, off, 0.0))


def _router(x2, g, wr_pad, stril):
    t, d = x2.shape
    nb = t // TM
    row = lambda w: pl.BlockSpec((TM, w), lambda i: (i, 0))
    return pl.pallas_call(
        _router_kernel,
        grid=(nb,),
        in_specs=[row(d), _const_spec((1, d)), _const_spec(wr_pad.shape), _const_spec((TM, TM))],
        out_specs=[row(d), row(LANES), pl.BlockSpec((1, 8, LANES), lambda i: (i, 0, 0))],
        out_shape=[jax.ShapeDtypeStruct((t, d), BF16), jax.ShapeDtypeStruct((t, LANES), F32),
                   jax.ShapeDtypeStruct((nb, 8, LANES), F32)],
        compiler_params=_cparams(("parallel",)),
        name="moe_router",
    )(x2, g, wr_pad, stril)


def _chunk_copy(src_ref, src_row, dst_ref, dst_row, sem):
    return pltpu.make_async_copy(src_ref.at[pl.ds(pl.multiple_of(src_row, ROW_GRAN), ROW_GRAN)],
                                 dst_ref.at[pl.ds(pl.multiple_of(dst_row, ROW_GRAN), ROW_GRAN)], sem)


def _segment_starts(b, gstart_ref, nch_ref, off_ref, hbm_ref, buf_ref, sem, to_hbm):
    for e in range(N_EXPERTS):
        n = nch_ref[b * N_EXPERTS + e]
        g0 = gstart_ref[b * N_EXPERTS + e]
        o0 = off_ref[b * N_EXPERTS + e]

        def start(c, carry, g0=g0, o0=o0):
            if to_hbm:
                _chunk_copy(buf_ref, o0 + c * ROW_GRAN, hbm_ref, g0 + c * ROW_GRAN, sem).start()
            else:
                _chunk_copy(hbm_ref, g0 + c * ROW_GRAN, buf_ref, o0 + c * ROW_GRAN, sem).start()
            return carry

        lax.fori_loop(0, n, start, 0)


def _segment_waits(b, nch_ref, buf_ref, sem):
    total = 0
    for e in range(N_EXPERTS):
        total = total + nch_ref[b * N_EXPERTS + e]

    def wait(c, carry):
        _chunk_copy(buf_ref, 0, buf_ref, 0, sem).wait()
        return carry

    lax.fori_loop(0, total, wait, 0)


def _dispatch_kernel(gstart_ref, nch_ref, off_ref, h_ref, rt_ref, xs_in_ref, xs_ref, buf_ref, sem):
    del xs_in_ref
    b = pl.program_id(0)
    slot = b % 2
    dest = rt_ref[...].T
    r = lax.broadcasted_iota(I32, (SORT_ROWS, TM), 0).astype(F32)
    perm = jnp.where((r == dest[0:1, :]) | (r == dest[1:2, :]), 1.0, 0.0).astype(BF16)
    buf_ref[slot] = _dot(perm, h_ref[...]).astype(BF16)
    _segment_starts(b, gstart_ref, nch_ref, off_ref, xs_ref, buf_ref.at[slot], sem.at[slot], to_hbm=True)

    @pl.when(b > 0)
    def _():
        _segment_waits(b - 1, nch_ref, buf_ref.at[1 - slot], sem.at[1 - slot])

    @pl.when(b == pl.num_programs(0) - 1)
    def _():
        _segment_waits(b, nch_ref, buf_ref.at[slot], sem.at[slot])


def _dispatch(gstart, nch, off, h2, rt, xs_zero):
    t, d = h2.shape
    grid_spec = pltpu.PrefetchScalarGridSpec(
        num_scalar_prefetch=3,
        grid=(t // TM,),
        in_specs=[pl.BlockSpec((TM, d), lambda i, *_: (i, 0)),
                  pl.BlockSpec((TM, LANES), lambda i, *_: (i, 0)),
                  pl.BlockSpec(memory_space=pl.ANY)],
        out_specs=pl.BlockSpec(memory_space=pl.ANY),
        scratch_shapes=[pltpu.VMEM((2, SORT_ROWS, d), BF16), pltpu.SemaphoreType.DMA((2,))],
    )
    return pl.pallas_call(
        _dispatch_kernel,
        grid_spec=grid_spec,
        out_shape=jax.ShapeDtypeStruct(xs_zero.shape, xs_zero.dtype),
        input_output_aliases={5: 0},
        compiler_params=_cparams(("arbitrary",)),
        name="moe_dispatch",
    )(gstart, nch, off, h2, rt, xs_zero)


def _group_ffn_kernel(te_ref, ta_ref, ts_ref, xs_ref, w1_ref, w3_ref, w2_ref, y_ref):
    del te_ref, ts_ref
    active = ta_ref[pl.program_id(0)] == 1

    @pl.when(active)
    def _():
        y_ref[...] = _swiglu_tile(xs_ref[...], w1_ref.at[0], w3_ref.at[0], w2_ref.at[0]).astype(BF16)

    @pl.when(jnp.logical_not(active))
    def _():
        y_ref[...] = jnp.zeros_like(y_ref)


def _group_ffn(tile_expert, tile_active, tile_src, xs, w1, w3, w2):
    ns, d = xs.shape
    ff = w1.shape[-1]
    grid_spec = pltpu.PrefetchScalarGridSpec(
        num_scalar_prefetch=3,
        grid=(ns // TMM,),
        in_specs=[pl.BlockSpec((TMM, d), lambda i, te, ta, ts: (ts[i], 0)),
                  pl.BlockSpec((1, d, ff), lambda i, te, ta, ts: (te[i], 0, 0)),
                  pl.BlockSpec((1, d, ff), lambda i, te, ta, ts: (te[i], 0, 0)),
                  pl.BlockSpec((1, ff, d), lambda i, te, ta, ts: (te[i], 0, 0))],
        out_specs=pl.BlockSpec((TMM, d), lambda i, te, ta, ts: (i, 0)),
    )
    return pl.pallas_call(
        _group_ffn_kernel,
        grid_spec=grid_spec,
        out_shape=jax.ShapeDtypeStruct((ns, d), BF16),
        compiler_params=_cparams(("arbitrary",)),
        name="moe_group_ffn",
    )(tile_expert, tile_active, tile_src, xs, w1, w3, w2)


def _combine_kernel(gstart_ref, nch_ref, off_ref, x_ref, rt_ref, gf_ref, y_ref, out_ref, buf_ref, sem,
                    *, final_norm):
    b = pl.program_id(0)
    slot = b % 2

    def fetch(blk, s):
        buf_ref[s] = jnp.zeros(buf_ref.shape[1:], BF16)
        _segment_starts(blk, gstart_ref, nch_ref, off_ref, y_ref, buf_ref.at[s], sem.at[s], to_hbm=False)

    @pl.when(b == 0)
    def _():
        fetch(b, slot)

    @pl.when(b + 1 < pl.num_programs(0))
    def _():
        fetch(b + 1, 1 - slot)

    _segment_waits(b, nch_ref, buf_ref.at[slot], sem.at[slot])
    rt = rt_ref[...]
    r = lax.broadcasted_iota(I32, (TM, SORT_ROWS), 1).astype(F32)
    ybuf = buf_ref[slot]
    y1 = _dot(jnp.where(r == rt[:, 0:1], 1.0, 0.0).astype(BF16), ybuf)
    y2 = _dot(jnp.where(r == rt[:, 1:2], 1.0, 0.0).astype(BF16), ybuf)
    x = x_ref[...] + rt[:, 2:3] * y1 + rt[:, 3:4] * y2
    out_ref[...] = _rms(x, gf_ref[...]) if final_norm else x


def _combine(gstart, nch, off, x2, rt, g_final, y, final_norm):
    t, d = x2.shape
    grid_spec = pltpu.PrefetchScalarGridSpec(
        num_scalar_prefetch=3,
        grid=(t // TM,),
        in_specs=[pl.BlockSpec((TM, d), lambda i, *_: (i, 0)),
                  pl.BlockSpec((TM, LANES), lambda i, *_: (i, 0)),
                  pl.BlockSpec((1, d), lambda i, *_: (0, 0)),
                  pl.BlockSpec(memory_space=pl.ANY)],
        out_specs=pl.BlockSpec((TM, d), lambda i, *_: (i, 0)),
        scratch_shapes=[pltpu.VMEM((2, SORT_ROWS, d), BF16), pltpu.SemaphoreType.DMA((2,))],
    )
    return pl.pallas_call(
        functools.partial(_combine_kernel, final_norm=final_norm),
        grid_spec=grid_spec,
        out_shape=jax.ShapeDtypeStruct((t, d), F32),
        compiler_params=_cparams(("arbitrary",)),
        name="moe_combine",
    )(gstart, nch, off, x2, rt, g_final, y)


def _norm_kernel(x_ref, g_ref, out_ref):
    out_ref[...] = _rms(x_ref[...], g_ref[...])


def _final_norm(x2, g):
    t, d = x2.shape
    row = pl.BlockSpec((TM, d), lambda i: (i, 0))
    return pl.pallas_call(
        _norm_kernel, grid=(t // TM,), in_specs=[row, _const_spec((1, d))], out_specs=row,
        out_shape=jax.ShapeDtypeStruct((t, d), F32), compiler_params=_cparams(("parallel",)),
        name="final_norm",
    )(x2, g)


def _tril_blocks(n, block, strict=False):
    r = jnp.arange(n)[:, None]
    c = jnp.arange(n)[None, :]
    keep = (c < r) if strict else (c <= r)
    keep = keep & ((r // block) == (c // block))
    return keep.astype(BF16)


def _moe_layer(x2, g_ffn, w_router, w1, w3, w2, g_final, final_norm):
    t, d = x2.shape
    nb = t // TM
    wr_pad = jnp.pad(w_router, ((0, 0), (0, LANES - N_EXPERTS)))
    h2, rt, seg = _router(x2, g_ffn.reshape(1, d), wr_pad, _tril_blocks(TM, TM, strict=True))

    pc = seg[:, 0, :N_EXPERTS].astype(I32)
    off = seg[:, 1, :N_EXPERTS].astype(I32)
    region = ((jnp.sum(pc, axis=0) + TMM - 1) // TMM) * TMM
    ends = jnp.cumsum(region)
    gstart = (ends - region)[None, :] + jnp.cumsum(pc, axis=0) - pc
    n_tiles = (2 * t + nb * N_EXPERTS * ROW_GRAN) // TMM + N_EXPERTS
    tile_start = jnp.arange(n_tiles, dtype=I32) * TMM
    tile_expert = jnp.minimum(jnp.sum(tile_start[:, None] >= ends[None, :], axis=1), N_EXPERTS - 1).astype(I32)
    tile_active = (tile_start < ends[-1]).astype(I32)
    last_active = jnp.maximum(ends[-1] // TMM - 1, 0)
    tile_src = jnp.minimum(jnp.arange(n_tiles, dtype=I32), last_active).astype(I32)
    seg_args = (gstart.reshape(-1).astype(I32), (pc // ROW_GRAN).reshape(-1), off.reshape(-1))

    xs = _dispatch(*seg_args, h2, rt, jnp.zeros((n_tiles * TMM, d), BF16))
    y = _group_ffn(tile_expert, tile_active, tile_src, xs, w1.astype(BF16), w3.astype(BF16), w2.astype(BF16))
    return _combine(*seg_args, x2, rt, g_final.reshape(1, d), y, final_norm)


def kernel(x, g_mix, w_in, b_f, conv_w, w_a2, b_a, g_gla, w_pa, w_pb, w_pc, w_gate, b_gate, w_o,
           g_ffn, ffn_w1, ffn_w3, ffn_w2, w_router, moe_w1, moe_w3, moe_w2, g_final):
    b, s, d = x.shape
    depth = g_mix.shape[0]
    t = b * s
    assert t % TM == 0 and s % TQ == 0 and TQ == TM and s % TL == 0 and TL % GLA_CHUNK == 0 and b % MIX_BATCH == 0
    x2 = x.reshape(t, d)
    tril_full = _tril_blocks(TM, TM)
    tril_chunk = _tril_blocks(TL, GLA_CHUNK)
    o_af = 3 * A_WIDTH
    o_b = o_af + A_HEADS
    o_c = o_b + 3 * B_WIDTH
    o_ca = o_c + 2 * C_KEY_WIDTH + C_VAL_WIDTH
    o_cr = o_ca + C_LOWRANK

    done_final = False
    for l in range(depth):
        w = w_in[l]
        wq = w[:, 0:A_WIDTH] * (A_HEAD_DIM ** -0.5 * LOG2E)
        wbig = jnp.concatenate([wq, w[:, A_WIDTH:2 * A_WIDTH], w[:, o_b:o_c], w[:, o_c:o_ca], w[:, o_cr:]],
                               axis=1).astype(BF16)
        wvt = w[:, 2 * A_WIDTH:o_af].T.astype(BF16)
        wsmall = jnp.concatenate([w[:, o_af:o_b], w[:, o_ca:o_cr],
                                  jnp.zeros((d, LANES - A_HEADS - C_LOWRANK), F32)], axis=1).astype(BF16)
        gm = g_mix[l].reshape(1, d)
        bf_pad = jnp.pad(b_f[l], (0, LANES - A_HEADS)).reshape(1, LANES)
        qp, kp, vt, conv_in, gla_in, small = _inproj(x2, gm, wbig, wvt, wsmall, bf_pad, tril_full, s // TM)
        o_a = _attention(qp, kp, vt, b, s)

        convw8 = jnp.pad(conv_w[l], ((0, 8 - conv_w.shape[1]), (0, 0)))
        wa2p = jnp.zeros((LANES, C_KEY_WIDTH), F32).at[A_HEADS:A_HEADS + C_LOWRANK].set(w_a2[l]).astype(BF16)
        out_c = _gla(gla_in, small, wa2p, b_a[l].reshape(1, -1), g_gla[l].reshape(1, -1), tril_chunk, b, s)

        x2 = _merge(x2, gm, o_a, conv_in, convw8, out_c, w_gate[l].astype(BF16), b_gate[l].reshape(1, -1),
                    w_pa[l].astype(BF16), w_pb[l].astype(BF16), w_pc[l].astype(BF16), w_o[l].astype(BF16),
                    s // TM)

        i = l // 2
        if l % 2 == 0:
            x2 = _dense_ffn(x2, g_ffn[l].reshape(1, d), ffn_w1[i].astype(BF16), ffn_w3[i].astype(BF16),
                            ffn_w2[i].astype(BF16))
        else:
            done_final = l == depth - 1
            x2 = _moe_layer(x2, g_ffn[l], w_router[i], moe_w1[i], moe_w3[i], moe_w2[i], g_final, done_final)
    if not done_final:
        x2 = _final_norm(x2, g_final.reshape(1, d))
    return x2.reshape(b, s, d)
```

```python
import functools
import math

import numpy as np
import jax
import jax.numpy as jnp
from jax import lax
from jax.experimental import pallas as pl
from jax.experimental.pallas import tpu as pltpu

F32 = jnp.float32
BF16 = jnp.bfloat16
I32 = jnp.int32

EPS = 1e-6
NEG = -1e30
LOG2E = math.log2(math.e)

A_HEADS, A_HEAD_DIM = 8, 64
A_WIDTH = A_HEADS * A_HEAD_DIM
B_WIDTH = 512
C_HEADS, C_KEY_DIM, C_VAL_DIM = 4, 64, 128
C_KEY_WIDTH = C_HEADS * C_KEY_DIM
C_VAL_WIDTH = C_HEADS * C_VAL_DIM
C_LOWRANK = 16
C_GATE_TEMP = 16.0
GLA_CHUNK = 64
N_EXPERTS = 8
GROUP_W = 1536

LANES = 128
V7X_VMEM_BYTES = 64 * 1024 * 1024
VMEM_LIMIT = V7X_VMEM_BYTES - 8 * 1024 * 1024

TM = 512
TQ = 512
ATT_HEADS = 8
ACC_ROWS = A_HEAD_DIM + 16
ATT_GUARD = 64.0
TL = 256
MIX_BATCH = 2
FF_CHUNK = 256
TMM = 512
ROW_GRAN = 16
SORT_ROWS = 2 * TM + N_EXPERTS * ROW_GRAN


def _cparams(sem):
    return pltpu.CompilerParams(dimension_semantics=sem, vmem_limit_bytes=VMEM_LIMIT)


def _dot(a, b):
    return jnp.dot(a, b, preferred_element_type=F32)


def _dot_nt(a, b):
    return lax.dot_general(a, b, (((1,), (1,)), ((), ())), preferred_element_type=F32)


def _rms(x, g):
    return x * lax.rsqrt(jnp.mean(x * x, axis=-1, keepdims=True) + EPS) * g


def _log_sigmoid(z):
    return jnp.minimum(z, 0.0) - jnp.log(1.0 + jnp.exp(-jnp.abs(z)))


def _sigmoid(z):
    return 1.0 / (1.0 + jnp.exp(-z))


def _split3(x):
    hi = x.astype(BF16)
    r1 = x - hi.astype(F32)
    mid = r1.astype(BF16)
    lo = (r1 - mid.astype(F32)).astype(BF16)
    return hi, mid, lo


def _const_spec(shape):
    return pl.BlockSpec(shape, lambda *_: (0,) * len(shape))


def _inproj_kernel(x_ref, g_ref, wbig_ref, wvt_ref, wsmall_ref, bf_ref, tril_ref, place_ref,
                   qp_ref, kp_ref, vt_ref, conv_ref, gla_ref, small_ref, carry_ref, *, tiles_per_seq):
    @pl.when(pl.program_id(0) % tiles_per_seq == 0)
    def _():
        carry_ref[...] = jnp.zeros_like(carry_ref)

    h = _rms(x_ref[...], g_ref[...]).astype(BF16)
    small = _dot(h, wsmall_ref[...])
    small_ref[...] = small
    q_all = _dot(h, wbig_ref[:, 0:A_WIDTH])
    k_all = _dot(h, wbig_ref[:, A_WIDTH:2 * A_WIDTH])
    off = 2 * A_WIDTH
    for ref in (conv_ref, gla_ref):
        width = ref.shape[1]
        for c in range(0, width, 512):
            ref[:, c:c + 512] = _dot(h, wbig_ref[:, off + c: off + c + 512]).astype(BF16)
        off += width
    vt_ref[0] = _dot_nt(wvt_ref[...], h).astype(BF16)

    lf = LOG2E * _log_sigmoid(small + bf_ref[...])
    hi, mid, lo = _split3(lf)
    tril = tril_ref[...]
    cs = _dot(tril, hi) + _dot(tril, mid) + _dot(tril, lo) + carry_ref[...]
    carry_ref[...] = cs[TM - 1:TM, :]
    fterms = _dot(jnp.concatenate(_split3(cs), axis=1), place_ref[...])

    lane = lax.broadcasted_iota(I32, (1, LANES), 1)
    is_x = lane < A_HEAD_DIM
    is_a = (lane >= A_HEAD_DIM) & (lane < A_HEAD_DIM + 3)
    is_b = (lane >= A_HEAD_DIM + 3) & (lane < A_HEAD_DIM + 6)
    for p in range(A_HEADS // 2):
        q2 = q_all[:, p * LANES:(p + 1) * LANES].astype(BF16).astype(F32)
        k2 = k_all[:, p * LANES:(p + 1) * LANES].astype(BF16).astype(F32)
        halves = ((q2, k2), (pltpu.roll(q2, A_HEAD_DIM, 1), pltpu.roll(k2, A_HEAD_DIM, 1)))
        for j in range(2):
            hd = 2 * p + j
            fh = pltpu.roll(fterms, (A_HEAD_DIM - 8 * hd) % LANES, 1)
            qh, kh = halves[j]
            qp_ref[:, hd * LANES:(hd + 1) * LANES] = jnp.where(
                is_x, qh, jnp.where(is_a, fh, jnp.where(is_b, 1.0, 0.0))).astype(BF16)
            kp_ref[:, hd * LANES:(hd + 1) * LANES] = jnp.where(
                is_x, kh, jnp.where(is_a, 1.0, jnp.where(is_b, -fh, 0.0))).astype(BF16)


def _aug_placement():
    place = np.zeros((3, LANES, LANES), np.float32)
    for h in range(A_HEADS):
        for c in range(3):
            place[c, h, 8 * h + c] = 1.0
            place[c, h, 8 * h + 3 + c] = 1.0
    return jnp.asarray(place.reshape(3 * LANES, LANES), BF16)


def _inproj(x2, g, wbig, wvt, wsmall, bf_pad, tril, tiles_per_seq):
    t, d = x2.shape
    place = _aug_placement()
    row = lambda w: pl.BlockSpec((TM, w), lambda i: (i, 0))
    return pl.pallas_call(
        functools.partial(_inproj_kernel, tiles_per_seq=tiles_per_seq),
        grid=(t // TM,),
        in_specs=[row(d), _const_spec((1, d)), _const_spec(wbig.shape), _const_spec(wvt.shape),
                  _const_spec(wsmall.shape), _const_spec((1, LANES)), _const_spec((TM, TM)),
                  _const_spec(place.shape)],
        out_specs=[row(A_HEADS * LANES), row(A_HEADS * LANES),
                   pl.BlockSpec((1, A_WIDTH, TM), lambda i: (i, 0, 0)), row(GROUP_W), row(GROUP_W), row(LANES)],
        out_shape=[jax.ShapeDtypeStruct((t, A_HEADS * LANES), BF16), jax.ShapeDtypeStruct((t, A_HEADS * LANES), BF16),
                   jax.ShapeDtypeStruct((t // TM, A_WIDTH, TM), BF16),
                   jax.ShapeDtypeStruct((t, GROUP_W), BF16), jax.ShapeDtypeStruct((t, GROUP_W), BF16),
                   jax.ShapeDtypeStruct((t, LANES), F32)],
        scratch_shapes=[pltpu.VMEM((1, LANES), F32)],
        compiler_params=_cparams(("arbitrary",)),
        name="inproj",
    )(x2, g, wbig, wvt, wsmall, bf_pad, tril, place)


def _attn_kernel(q_ref, k_ref, vt_ref, o_ref, m_ref, acc_ref, exc_ref):
    qi = pl.program_id(2)
    ones = jnp.ones((ACC_ROWS - A_HEAD_DIM, TQ), BF16)

    def _score(ki, j):
        return _dot_nt(k_ref[ki, :, j * LANES:(j + 1) * LANES], q_ref[:, j * LANES:(j + 1) * LANES])

    def _values(ki, j):
        return jnp.concatenate([vt_ref[ki, j * A_HEAD_DIM:(j + 1) * A_HEAD_DIM, :], ones], axis=0)

    def _diag_step(ki):
        key = lax.broadcasted_iota(I32, (TQ, TQ), 0)
        qry = lax.broadcasted_iota(I32, (TQ, TQ), 1)
        keep = key <= qry
        scores = [_score(ki, j) for j in range(ATT_HEADS)]
        for j in range(ATT_HEADS):
            st = jnp.where(keep, scores[j], NEG)
            m_new = jnp.max(st, axis=0, keepdims=True)
            acc_ref[j] = _dot(_values(ki, j), jnp.exp2(st - m_new).astype(BF16))
            m_ref[j] = m_new

    def _exact_step(ki):
        scores = [_score(ki, j) for j in range(ATT_HEADS)]
        for j in range(ATT_HEADS):
            st = scores[j]
            m_prev = m_ref[j]
            m_new = jnp.maximum(m_prev, jnp.max(st, axis=0, keepdims=True))
            pr = jnp.exp2(st - m_new).astype(BF16)
            acc_ref[j] = jnp.exp2(m_prev - m_new) * acc_ref[j] + _dot(_values(ki, j), pr)
            m_ref[j] = m_new

    def _fast_step(ki, carry):
        excess = exc_ref[...]
        pending = _score(ki, 0)
        for j in range(ATT_HEADS):
            st = pending
            if j + 1 < ATT_HEADS:
                pending = _score(ki, j + 1)
            m_prev = m_ref[j]
            pr = jnp.exp2(st - m_prev).astype(BF16)
            tmax = jnp.max(st, axis=0, keepdims=True)
            excess = jnp.maximum(excess, tmax - m_prev)
            m_new = jnp.maximum(m_prev, tmax)
            acc_ref[j] = (acc_ref[j] + _dot(_values(ki, j), pr)) * jnp.exp2(m_prev - m_new)
            m_ref[j] = m_new
        exc_ref[...] = excess
        return carry

    def _exact_below(ki, carry):
        _exact_step(ki)
        return carry

    exc_ref[...] = jnp.zeros_like(exc_ref)
    _diag_step(qi)
    lax.fori_loop(0, qi, _fast_step, 0)

    @pl.when(jnp.max(exc_ref[...]) > ATT_GUARD)
    def _():
        _diag_step(qi)
        lax.fori_loop(0, qi, _exact_below, 0)

    ot = jnp.concatenate([acc_ref[j, 0:A_HEAD_DIM, :] / acc_ref[j, A_HEAD_DIM:A_HEAD_DIM + 1, :]
                          for j in range(ATT_HEADS)], axis=0)
    o_ref[...] = ot.T.astype(BF16)


def _attention(qp, kp, vt, b, s):
    nq = s // TQ
    ng = A_HEADS // ATT_HEADS
    kp3 = kp.reshape(b * nq, TQ, A_HEADS * LANES)
    return pl.pallas_call(
        _attn_kernel,
        grid=(b, ng, nq),
        in_specs=[pl.BlockSpec((TQ, ATT_HEADS * LANES), lambda bi, p, qi: (bi * nq + qi, p)),
                  pl.BlockSpec((nq, TQ, ATT_HEADS * LANES), lambda bi, p, qi: (bi, 0, p)),
                  pl.BlockSpec((nq, ATT_HEADS * A_HEAD_DIM, TQ), lambda bi, p, qi: (bi, p, 0))],
        out_specs=pl.BlockSpec((TQ, ATT_HEADS * A_HEAD_DIM), lambda bi, p, qi: (bi * nq + qi, p)),
        out_shape=jax.ShapeDtypeStruct((b * s, A_WIDTH), BF16),
        scratch_shapes=[pltpu.VMEM((ATT_HEADS, 1, TQ), F32), pltpu.VMEM((ATT_HEADS, ACC_ROWS, TQ), F32),
                        pltpu.VMEM((1, TQ), F32)],
        compiler_params=_cparams(("parallel", "parallel", "arbitrary")),
        name="fox_attention",
    )(qp, kp3, vt)


def _gla_kernel(gla_ref, small_ref, wa2_ref, ba_ref, ggla_ref, tri_ref, oc_ref, st_ref, sall_ref):
    @pl.when(pl.program_id(1) == 0)
    def _():
        st_ref[...] = jnp.zeros_like(st_ref)

    nc = TL // GLA_CHUNK
    nb = MIX_BATCH

    tri = tri_ref[...]
    q_t, k_t, ks_t, dec_t = [], [], [], []
    for i in range(nb):
        la = _dot(small_ref[i].astype(BF16), wa2_ref[...]) + ba_ref[...]
        la = _log_sigmoid(la) * (1.0 / C_GATE_TEMP)
        hi, mid, lo = _split3(la)
        bcum = _dot(tri, hi) + _dot(tri, mid) + _dot(tri, lo)
        bl = bcum.reshape(nc, GLA_CHUNK, C_KEY_WIDTH)[:, GLA_CHUNK - 1:GLA_CHUNK, :]
        blast = jnp.broadcast_to(bl, (nc, GLA_CHUNK, C_KEY_WIDTH)).reshape(TL, C_KEY_WIDTH)
        q = gla_ref[i, :, 0:C_KEY_WIDTH].astype(F32)
        k = gla_ref[i, :, C_KEY_WIDTH:2 * C_KEY_WIDTH].astype(F32)
        q_t.append(q * (C_KEY_DIM ** -0.5) * jnp.exp(bcum))
        k_t.append((k * jnp.exp(-bcum)).astype(BF16))
        ks_t.append((k * jnp.exp(blast - bcum)).T)
        dec_t.append(jnp.broadcast_to(jnp.exp(bl), (nc, LANES, C_KEY_WIDTH)).reshape(nc * LANES, C_KEY_WIDTH).T)

    r_i = lax.broadcasted_iota(I32, (TL, TL), 0)
    c_i = lax.broadcasted_iota(I32, (TL, TL), 1)
    same = (r_i // GLA_CHUNK) == (c_i // GLA_CHUNK)
    intra = same & (c_i <= r_i)
    lane_k = lax.broadcasted_iota(I32, (1, C_KEY_WIDTH), 1)
    lane = lax.broadcasted_iota(I32, (1, LANES), 1)

    for h in range(C_HEADS):
        for i in range(nb):
            v_h = gla_ref[i, :, 2 * C_KEY_WIDTH + h * C_VAL_DIM: 2 * C_KEY_WIDTH + (h + 1) * C_VAL_DIM]
            q_h = jnp.where(lane_k // C_KEY_DIM == h, q_t[i], 0.0).astype(BF16)
            a = jnp.where(intra, _dot_nt(q_h, k_t[i]), 0.0).astype(BF16)

            q2 = q_t[i][:, (h // 2) * LANES:(h // 2 + 1) * LANES]
            q2r = pltpu.roll(q2, C_KEY_DIM, 1)
            dup = jnp.where((lane < C_KEY_DIM) == (h % 2 == 0), q2, q2r)
            q_exp = jnp.where(same, jnp.concatenate([dup] * (TL // LANES), axis=1), 0.0).astype(BF16)

            ks_h = ks_t[i][h * C_KEY_DIM:(h + 1) * C_KEY_DIM, :]
            k_exp = jnp.where(same, jnp.concatenate([ks_h] * nc, axis=0), 0.0).astype(BF16)
            kv = _dot(k_exp, v_h)

            st = st_ref[i, h]
            for c in range(nc):
                sall_ref[i, h, c * GLA_CHUNK:(c + 1) * GLA_CHUNK, :] = st.astype(BF16)
                dec = dec_t[i][h * C_KEY_DIM:(h + 1) * C_KEY_DIM, c * LANES:(c + 1) * LANES]
                st = dec * st + kv[c * GLA_CHUNK:(c + 1) * GLA_CHUNK, :]
            st_ref[i, h] = st

            o = _dot(a, v_h) + _dot(q_exp, sall_ref[i, h])
            o = o * lax.rsqrt(jnp.mean(o * o, axis=-1, keepdims=True) + EPS)
            o = o * ggla_ref[:, h * C_VAL_DIM:(h + 1) * C_VAL_DIM]
            r = gla_ref[i, :, 4 * C_KEY_WIDTH + h * C_VAL_DIM: 4 * C_KEY_WIDTH + (h + 1) * C_VAL_DIM].astype(F32)
            oc_ref[i, :, h * C_VAL_DIM:(h + 1) * C_VAL_DIM] = (o * (r * _sigmoid(r))).astype(BF16)


def _gla(gla_in, small, wa2p, ba, ggla, tri_bd, b, s):
    ns = s // TL
    blk = lambda w: pl.BlockSpec((MIX_BATCH, TL, w), lambda bi, si: (bi, si, 0))
    oc = pl.pallas_call(
        _gla_kernel,
        grid=(b // MIX_BATCH, ns),
        in_specs=[blk(GROUP_W), blk(LANES),
                  _const_spec((LANES, C_KEY_WIDTH)),
                  _const_spec((1, C_KEY_WIDTH)),
                  _const_spec((1, C_VAL_WIDTH)),
                  _const_spec((TL, TL))],
        out_specs=blk(C_VAL_WIDTH),
        out_shape=jax.ShapeDtypeStruct((b, s, C_VAL_WIDTH), BF16),
        scratch_shapes=[pltpu.VMEM((MIX_BATCH, C_HEADS, C_KEY_DIM, C_VAL_DIM), F32),
                        pltpu.VMEM((MIX_BATCH, C_HEADS, TL, C_VAL_DIM), BF16)],
        compiler_params=_cparams(("parallel", "arbitrary")),
        name="gla",
    )(gla_in.reshape(b, s, GROUP_W), small.reshape(b, s, LANES), wa2p, ba, ggla, tri_bd)
    return oc.reshape(b * s, C_VAL_WIDTH)


def _merge_kernel(x_ref, g_ref, oa_ref, conv_ref, convw_ref, oc_ref, wg_ref, bgate_ref, wpa_ref, wpb_ref,
                  wpc_ref, wo_ref, out_ref, zprev_ref, *, tiles_per_seq):
    @pl.when(pl.program_id(0) % tiles_per_seq == 0)
    def _():
        zprev_ref[...] = jnp.zeros_like(zprev_ref)

    u = conv_ref[:, 0:B_WIDTH].astype(F32)
    bg = conv_ref[:, B_WIDTH:2 * B_WIDTH].astype(F32)
    cg = conv_ref[:, 2 * B_WIDTH:3 * B_WIDTH].astype(F32)
    z = cg * u
    row = lax.broadcasted_iota(I32, (TM, B_WIDTH), 0)
    zp = zprev_ref[...]
    z1 = jnp.where(row == 0, zp[7:8, :], pltpu.roll(z, 1, 0))
    z2 = jnp.where(row == 0, zp[6:7, :], jnp.where(row == 1, zp[7:8, :], pltpu.roll(z, 2, 0)))
    cw = convw_ref[...]
    o_b = (bg * (cw[0:1, :] * z2 + cw[1:2, :] * z1 + cw[2:3, :] * z)).astype(BF16)
    zprev_ref[...] = z[TM - 8:TM, :]

    x = x_ref[...]
    d = x.shape[1]
    h = _rms(x, g_ref[...]).astype(BF16)
    merged = None
    for j, (o, wp_ref) in enumerate(((oa_ref[...], wpa_ref), (o_b, wpb_ref), (oc_ref[...], wpc_ref))):
        gate = _sigmoid(_dot(h, wg_ref[:, j * d:(j + 1) * d]) + bgate_ref[:, j * d:(j + 1) * d])
        term = gate * _dot(o, wp_ref[...])
        merged = term if merged is None else merged + term
    out_ref[...] = x + _dot(merged.astype(BF16), wo_ref[...])


def _merge(x2, g, oa, conv_in, convw8, oc, wg, bgate, wpa, wpb, wpc, wo, tiles_per_seq):
    t, d = x2.shape
    row = lambda w: pl.BlockSpec((TM, w), lambda i: (i, 0))
    return pl.pallas_call(
        functools.partial(_merge_kernel, tiles_per_seq=tiles_per_seq),
        grid=(t // TM,),
        in_specs=[row(d), _const_spec((1, d)), row(A_WIDTH), row(GROUP_W), _const_spec((8, B_WIDTH)),
                  row(C_VAL_WIDTH), _const_spec(wg.shape), _const_spec(bgate.shape), _const_spec(wpa.shape),
                  _const_spec(wpb.shape), _const_spec(wpc.shape), _const_spec(wo.shape)],
        out_specs=row(d),
        out_shape=jax.ShapeDtypeStruct((t, d), F32),
        scratch_shapes=[pltpu.VMEM((8, B_WIDTH), F32)],
        compiler_params=_cparams(("arbitrary",)),
        name="merge_outproj",
    )(x2, g, oa, conv_in, convw8, oc, wg, bgate, wpa, wpb, wpc, wo)


def _swiglu_tile(h, w1_ref, w3_ref, w2_ref):
    ff = w1_ref.shape[-1]
    acc = None
    for c in range(0, ff, FF_CHUNK):
        a = _dot(h, w1_ref[:, c:c + FF_CHUNK])
        b = _dot(h, w3_ref[:, c:c + FF_CHUNK])
        g = (a * _sigmoid(a) * b).astype(BF16)
        part = _dot(g, w2_ref[c:c + FF_CHUNK, :])
        acc = part if acc is None else acc + part
    return acc


def _ffn_kernel(x_ref, g_ref, w1_ref, w3_ref, w2_ref, *rest):
    if len(rest) == 1:
        (out_ref,) = rest
    else:
        m1_ref, m3_ref, m2_ref, out_ref, c1_ref, c3_ref, c2_ref, zero_ref = rest
        c1_ref[...] = m1_ref[...].astype(BF16)
        c3_ref[...] = m3_ref[...].astype(BF16)
        c2_ref[...] = m2_ref[...].astype(BF16)
        zero_ref[...] = jnp.zeros_like(zero_ref)
    x = x_ref[...]
    h = _rms(x, g_ref[...]).astype(BF16)
    out_ref[...] = x + _swiglu_tile(h, w1_ref, w3_ref, w2_ref)


def _dense_ffn(x2, g, w1, w3, w2, moe_prep=None):
    t, d = x2.shape
    steps = t // TM
    row = pl.BlockSpec((TM, d), lambda i: (i, 0))
    in_specs = [row, _const_spec((1, d)), _const_spec(w1.shape), _const_spec(w3.shape), _const_spec(w2.shape)]
    out_specs = [row]
    out_shape = [jax.ShapeDtypeStruct((t, d), F32)]
    args = [x2, g, w1, w3, w2]
    if moe_prep is not None:
        m1, m3, m2, sorted_rows = moe_prep
        flat = [m1.reshape(-1, m1.shape[-1]), m3.reshape(-1, m3.shape[-1]), m2.reshape(-1, m2.shape[-1])]
        shapes = [a.shape for a in flat] + [(sorted_rows, d)]
        assert all(r % (steps * ROW_GRAN) == 0 for r, _ in shapes)
        slab = lambda shp: pl.BlockSpec((shp[0] // steps, shp[1]), lambda i: (i, 0))
        in_specs += [slab(s) for s in shapes[:3]]
        out_specs += [slab(s) for s in shapes]
        out_shape += [jax.ShapeDtypeStruct(s, BF16) for s in shapes]
        args += flat
    outs = pl.pallas_call(
        _ffn_kernel,
        grid=(steps,),
        in_specs=in_specs,
        out_specs=out_specs,
        out_shape=out_shape,
        compiler_params=_cparams(("parallel",)),
        name="dense_ffn",
    )(*args)
    if moe_prep is None:
        return outs[0], None
    m1, m3, m2, _ = moe_prep
    return outs[0], (outs[1].reshape(m1.shape), outs[2].reshape(m3.shape), outs[3].reshape(m2.shape), outs[4])


def _router_kernel(x_ref, g_ref, wr_ref, stril_ref, h_ref, rt_ref, seg_ref):
    h = _rms(x_ref[...], g_ref[...])
    h_hi = h.astype(BF16)
    h_ref[...] = h_hi
    h_lo = (h - h_hi.astype(F32)).astype(BF16)
    wr = wr_ref[...]
    w_hi = wr.astype(BF16)
    w_lo = (wr - w_hi.astype(F32)).astype(BF16)
    both = _dot(h_hi, jnp.concatenate([w_hi, w_lo], axis=1))
    logits = both[:, 0:LANES] + both[:, LANES:2 * LANES] + _dot(h_lo, w_hi)

    lane = lax.broadcasted_iota(I32, (TM, LANES), 1)
    lane_f = lane.astype(F32)
    lg = jnp.where(lane < N_EXPERTS, logits, NEG)
    m1 = jnp.max(lg, axis=-1, keepdims=True)
    i1 = jnp.min(jnp.where(lg == m1, lane_f, float(LANES)), axis=-1, keepdims=True)
    oh1 = lane_f == i1
    lg2 = jnp.where(oh1, NEG, lg)
    m2 = jnp.max(lg2, axis=-1, keepdims=True)
    i2 = jnp.min(jnp.where(lg2 == m2, lane_f, float(LANES)), axis=-1, keepdims=True)
    oh2 = lane_f == i2
    e = jnp.exp(m2 - m1)
    w1 = 1.0 / (1.0 + e)
    w2 = e / (1.0 + e)

    sel = jnp.where(oh1 | oh2, 1.0, 0.0)
    before = _dot(stril_ref[...], sel.astype(BF16))
    cnt = jnp.sum(sel, axis=0, keepdims=True)
    pc = jnp.floor((cnt + (ROW_GRAN - 1)) * (1.0 / ROW_GRAN)) * ROW_GRAN
    lane1 = lax.broadcasted_iota(I32, (1, LANES), 1)
    incl = pc
    for sh in (1, 2, 4):
        incl = incl + jnp.where(lane1 >= sh, pltpu.roll(incl, sh, 1), 0.0)
    off = incl - pc
    pos = before + off
    d1 = jnp.sum(jnp.where(oh1, pos, 0.0), axis=-1, keepdims=True)
    d2 = jnp.sum(jnp.where(oh2, pos, 0.0), axis=-1, keepdims=True)
    rt_ref[...] = jnp.where(lane == 0, d1, jnp.where(lane == 1, d2, jnp.where(lane == 2, w1, jnp.where(lane == 3, w2, 0.0))))
    row8 = lax.broadcasted_iota(I32, (8, LANES), 0)
    seg_ref[0] = jnp.where(row8 == 0, pc, jnp.where(row8 == 1, off, 0.0))


def _router(x2, g, wr_pad, stril):
    t, d = x2.shape
    nb = t // TM
    row = lambda w: pl.BlockSpec((TM, w), lambda i: (i, 0))
    return pl.pallas_call(
        _router_kernel,
        grid=(nb,),
        in_specs=[row(d), _const_spec((1, d)), _const_spec(wr_pad.shape), _const_spec((TM, TM))],
        out_specs=[row(d), row(LANES), pl.BlockSpec((1, 8, LANES), lambda i: (i, 0, 0))],
        out_shape=[jax.ShapeDtypeStruct((t, d), BF16), jax.ShapeDtypeStruct((t, LANES), F32),
                   jax.ShapeDtypeStruct((nb, 8, LANES), F32)],
        compiler_params=_cparams(("parallel",)),
        name="moe_router",
    )(x2, g, wr_pad, stril)


def _chunk_copy(src_ref, src_row, dst_ref, dst_row, sem):
    return pltpu.make_async_copy(src_ref.at[pl.ds(pl.multiple_of(src_row, ROW_GRAN), ROW_GRAN)],
                                 dst_ref.at[pl.ds(pl.multiple_of(dst_row, ROW_GRAN), ROW_GRAN)], sem)


def _segment_starts(b, gstart_ref, nch_ref, off_ref, hbm_ref, buf_ref, sem, to_hbm):
    for e in range(N_EXPERTS):
        n = nch_ref[b * N_EXPERTS + e]
        g0 = gstart_ref[b * N_EXPERTS + e]
        o0 = off_ref[b * N_EXPERTS + e]

        def start(c, carry, g0=g0, o0=o0):
            if to_hbm:
                _chunk_copy(buf_ref, o0 + c * ROW_GRAN, hbm_ref, g0 + c * ROW_GRAN, sem).start()
            else:
                _chunk_copy(hbm_ref, g0 + c * ROW_GRAN, buf_ref, o0 + c * ROW_GRAN, sem).start()
            return carry

        lax.fori_loop(0, n, start, 0)


def _segment_waits(b, nch_ref, buf_ref, sem):
    total = 0
    for e in range(N_EXPERTS):
        total = total + nch_ref[b * N_EXPERTS + e]

    def wait(c, carry):
        _chunk_copy(buf_ref, 0, buf_ref, 0, sem).wait()
        return carry

    lax.fori_loop(0, total, wait, 0)


def _dispatch_kernel(gstart_ref, nch_ref, off_ref, h_ref, rt_ref, xs_in_ref, xs_ref, buf_ref, sem):
    del xs_in_ref
    b = pl.program_id(0)
    slot = b % 2
    dest = rt_ref[...].T
    r = lax.broadcasted_iota(I32, (SORT_ROWS, TM), 0).astype(F32)
    perm = jnp.where((r == dest[0:1, :]) | (r == dest[1:2, :]), 1.0, 0.0).astype(BF16)
    buf_ref[slot] = _dot(perm, h_ref[...]).astype(BF16)
    _segment_starts(b, gstart_ref, nch_ref, off_ref, xs_ref, buf_ref.at[slot], sem.at[slot], to_hbm=True)

    @pl.when(b > 0)
    def _():
        _segment_waits(b - 1, nch_ref, buf_ref.at[1 - slot], sem.at[1 - slot])

    @pl.when(b == pl.num_programs(0) - 1)
    def _():
        _segment_waits(b, nch_ref, buf_ref.at[slot], sem.at[slot])


def _dispatch(gstart, nch, off, h2, rt, xs_zero):
    t, d = h2.shape
    grid_spec = pltpu.PrefetchScalarGridSpec(
        num_scalar_prefetch=3,
        grid=(t // TM,),
        in_specs=[pl.BlockSpec((TM, d), lambda i, *_: (i, 0)),
                  pl.BlockSpec((TM, LANES), lambda i, *_: (i, 0)),
                  pl.BlockSpec(memory_space=pl.ANY)],
        out_specs=pl.BlockSpec(memory_space=pl.ANY),
        scratch_shapes=[pltpu.VMEM((2, SORT_ROWS, d), BF16), pltpu.SemaphoreType.DMA((2,))],
    )
    return pl.pallas_call(
        _dispatch_kernel,
        grid_spec=grid_spec,
        out_shape=jax.ShapeDtypeStruct(xs_zero.shape, xs_zero.dtype),
        input_output_aliases={5: 0},
        compiler_params=_cparams(("arbitrary",)),
        name="moe_dispatch",
    )(gstart, nch, off, h2, rt, xs_zero)


def _group_ffn_kernel(te_ref, ta_ref, ts_ref, xs_ref, w1_ref, w3_ref, w2_ref, y_ref):
    del te_ref, ts_ref
    active = ta_ref[pl.program_id(0)] == 1

    @pl.when(active)
    def _():
        y_ref[...] = _swiglu_tile(xs_ref[...], w1_ref.at[0], w3_ref.at[0], w2_ref.at[0]).astype(BF16)

    @pl.when(jnp.logical_not(active))
    def _():
        y_ref[...] = jnp.zeros_like(y_ref)


def _group_ffn(tile_expert, tile_active, tile_src, xs, w1, w3, w2):
    ns, d = xs.shape
    ff = w1.shape[-1]
    grid_spec = pltpu.PrefetchScalarGridSpec(
        num_scalar_prefetch=3,
        grid=(ns // TMM,),
        in_specs=[pl.BlockSpec((TMM, d), lambda i, te, ta, ts: (ts[i], 0)),
                  pl.BlockSpec((1, d, ff), lambda i, te, ta, ts: (te[i], 0, 0)),
                  pl.BlockSpec((1, d, ff), lambda i, te, ta, ts: (te[i], 0, 0)),
                  pl.BlockSpec((1, ff, d), lambda i, te, ta, ts: (te[i], 0, 0))],
        out_specs=pl.BlockSpec((TMM, d), lambda i, te, ta, ts: (i, 0)),
    )
    return pl.pallas_call(
        _group_ffn_kernel,
        grid_spec=grid_spec,
        out_shape=jax.ShapeDtypeStruct((ns, d), BF16),
        compiler_params=_cparams(("arbitrary",)),
        name="moe_group_ffn",
    )(tile_expert, tile_active, tile_src, xs, w1, w3, w2)


def _combine_kernel(gstart_ref, nch_ref, off_ref, x_ref, rt_ref, gf_ref, y_ref, out_ref, buf_ref, sem,
                    *, final_norm):
    b = pl.program_id(0)
    slot = b % 2

    def fetch(blk, s):
        buf_ref[s] = jnp.zeros(buf_ref.shape[1:], BF16)
        _segment_starts(blk, gstart_ref, nch_ref, off_ref, y_ref, buf_ref.at[s], sem.at[s], to_hbm=False)

    @pl.when(b == 0)
    def _():
        fetch(b, slot)

    @pl.when(b + 1 < pl.num_programs(0))
    def _():
        fetch(b + 1, 1 - slot)

    _segment_waits(b, nch_ref, buf_ref.at[slot], sem.at[slot])
    rt = rt_ref[...]
    r = lax.broadcasted_iota(I32, (TM, SORT_ROWS), 1).astype(F32)
    ybuf = buf_ref[slot]
    y1 = _dot(jnp.where(r == rt[:, 0:1], 1.0, 0.0).astype(BF16), ybuf)
    y2 = _dot(jnp.where(r == rt[:, 1:2], 1.0, 0.0).astype(BF16), ybuf)
    x = x_ref[...] + rt[:, 2:3] * y1 + rt[:, 3:4] * y2
    out_ref[...] = _rms(x, gf_ref[...]) if final_norm else x


def _combine(gstart, nch, off, x2, rt, g_final, y, final_norm):
    t, d = x2.shape
    grid_spec = pltpu.PrefetchScalarGridSpec(
        num_scalar_prefetch=3,
        grid=(t // TM,),
        in_specs=[pl.BlockSpec((TM, d), lambda i, *_: (i, 0)),
                  pl.BlockSpec((TM, LANES), lambda i, *_: (i, 0)),
                  pl.BlockSpec((1, d), lambda i, *_: (0, 0)),
                  pl.BlockSpec(memory_space=pl.ANY)],
        out_specs=pl.BlockSpec((TM, d), lambda i, *_: (i, 0)),
        scratch_shapes=[pltpu.VMEM((2, SORT_ROWS, d), BF16), pltpu.SemaphoreType.DMA((2,))],
    )
    return pl.pallas_call(
        functools.partial(_combine_kernel, final_norm=final_norm),
        grid_spec=grid_spec,
        out_shape=jax.ShapeDtypeStruct((t, d), F32),
        compiler_params=_cparams(("arbitrary",)),
        name="moe_combine",
    )(gstart, nch, off, x2, rt, g_final, y)


def _norm_kernel(x_ref, g_ref, out_ref):
    out_ref[...] = _rms(x_ref[...], g_ref[...])


def _final_norm(x2, g):
    t, d = x2.shape
    row = pl.BlockSpec((TM, d), lambda i: (i, 0))
    return pl.pallas_call(
        _norm_kernel, grid=(t // TM,), in_specs=[row, _const_spec((1, d))], out_specs=row,
        out_shape=jax.ShapeDtypeStruct((t, d), F32), compiler_params=_cparams(("parallel",)),
        name="final_norm",
    )(x2, g)


def _tril_blocks(n, block, strict=False):
    r = jnp.arange(n)[:, None]
    c = jnp.arange(n)[None, :]
    keep = (c < r) if strict else (c <= r)
    keep = keep & ((r // block) == (c // block))
    return keep.astype(BF16)


def _sorted_rows(t):
    return ((2 * t + (t // TM) * N_EXPERTS * ROW_GRAN) // TMM + N_EXPERTS) * TMM


def _moe_layer(x2, g_ffn, w_router, w1, w3, w2, xs_zero, g_final, final_norm):
    t, d = x2.shape
    nb = t // TM
    wr_pad = jnp.pad(w_router, ((0, 0), (0, LANES - N_EXPERTS)))
    h2, rt, seg = _router(x2, g_ffn.reshape(1, d), wr_pad, _tril_blocks(TM, TM, strict=True))

    pc = seg[:, 0, :N_EXPERTS].astype(I32)
    off = seg[:, 1, :N_EXPERTS].astype(I32)
    region = ((jnp.sum(pc, axis=0) + TMM - 1) // TMM) * TMM
    ends = jnp.cumsum(region)
    gstart = (ends - region)[None, :] + jnp.cumsum(pc, axis=0) - pc
    n_tiles = _sorted_rows(t) // TMM
    tile_start = jnp.arange(n_tiles, dtype=I32) * TMM
    tile_expert = jnp.minimum(jnp.sum(tile_start[:, None] >= ends[None, :], axis=1), N_EXPERTS - 1).astype(I32)
    tile_active = (tile_start < ends[-1]).astype(I32)
    last_active = jnp.maximum(ends[-1] // TMM - 1, 0)
    tile_src = jnp.minimum(jnp.arange(n_tiles, dtype=I32), last_active).astype(I32)
    seg_args = (gstart.reshape(-1).astype(I32), (pc // ROW_GRAN).reshape(-1), off.reshape(-1))

    xs = _dispatch(*seg_args, h2, rt, xs_zero)
    y = _group_ffn(tile_expert, tile_active, tile_src, xs, w1, w3, w2)
    return _combine(*seg_args, x2, rt, g_final.reshape(1, d), y, final_norm)


def kernel(x, g_mix, w_in, b_f, conv_w, w_a2, b_a, g_gla, w_pa, w_pb, w_pc, w_gate, b_gate, w_o,
           g_ffn, ffn_w1, ffn_w3, ffn_w2, w_router, moe_w1, moe_w3, moe_w2, g_final):
    b, s, d = x.shape
    depth = g_mix.shape[0]
    t = b * s
    assert t % TM == 0 and s % TQ == 0 and TQ == TM and s % TL == 0 and TL % GLA_CHUNK == 0 and b % MIX_BATCH == 0
    x2 = x.reshape(t, d)
    tril_full = _tril_blocks(TM, TM)
    tril_chunk = _tril_blocks(TL, GLA_CHUNK)
    o_af = 3 * A_WIDTH
    o_b = o_af + A_HEADS
    o_c = o_b + 3 * B_WIDTH
    o_ca = o_c + 2 * C_KEY_WIDTH + C_VAL_WIDTH
    o_cr = o_ca + C_LOWRANK

    done_final = False
    for l in range(depth):
        w = w_in[l]
        wq = w[:, 0:A_WIDTH] * (A_HEAD_DIM ** -0.5 * LOG2E)
        wbig = jnp.concatenate([wq, w[:, A_WIDTH:2 * A_WIDTH], w[:, o_b:o_c], w[:, o_c:o_ca], w[:, o_cr:]],
                               axis=1).astype(BF16)
        wvt = w[:, 2 * A_WIDTH:o_af].T.astype(BF16)
        wsmall = jnp.concatenate([w[:, o_af:o_b], w[:, o_ca:o_cr],
                                  jnp.zeros((d, LANES - A_HEADS - C_LOWRANK), F32)], axis=1).astype(BF16)
        gm = g_mix[l].reshape(1, d)
        bf_pad = jnp.pad(b_f[l], (0, LANES - A_HEADS)).reshape(1, LANES)
        qp, kp, vt, conv_in, gla_in, small = _inproj(x2, gm, wbig, wvt, wsmall, bf_pad, tril_full, s // TM)
        o_a = _attention(qp, kp, vt, b, s)

        convw8 = jnp.pad(conv_w[l], ((0, 8 - conv_w.shape[1]), (0, 0)))
        wa2p = jnp.zeros((LANES, C_KEY_WIDTH), F32).at[A_HEADS:A_HEADS + C_LOWRANK].set(w_a2[l]).astype(BF16)
        out_c = _gla(gla_in, small, wa2p, b_a[l].reshape(1, -1), g_gla[l].reshape(1, -1), tril_chunk, b, s)

        x2 = _merge(x2, gm, o_a, conv_in, convw8, out_c, w_gate[l].astype(BF16), b_gate[l].reshape(1, -1),
                    w_pa[l].astype(BF16), w_pb[l].astype(BF16), w_pc[l].astype(BF16), w_o[l].astype(BF16),
                    s // TM)

        i = l // 2
        if l % 2 == 0:
            prep = (moe_w1[i], moe_w3[i], moe_w2[i], _sorted_rows(t)) if l + 1 < depth else None
            x2, moe_ready = _dense_ffn(x2, g_ffn[l].reshape(1, d), ffn_w1[i].astype(BF16),
                                       ffn_w3[i].astype(BF16), ffn_w2[i].astype(BF16), prep)
        else:
            done_final = l == depth - 1
            x2 = _moe_layer(x2, g_ffn[l], w_router[i], *moe_ready, g_final, done_final)
    if not done_final:
        x2 = _final_norm(x2, g_final.reshape(1, d))
    return x2.reshape(b, s, d)
```

```python
import functools
import math

import numpy as np
import jax
import jax.numpy as jnp
from jax import lax
from jax.experimental import pallas as pl
from jax.experimental.pallas import tpu as pltpu

F32 = jnp.float32
BF16 = jnp.bfloat16
I32 = jnp.int32

EPS = 1e-6
NEG = -1e30
LOG2E = math.log2(math.e)

A_HEADS, A_HEAD_DIM = 8, 64
A_WIDTH = A_HEADS * A_HEAD_DIM
B_WIDTH = 512
C_HEADS, C_KEY_DIM, C_VAL_DIM = 4, 64, 128
C_KEY_WIDTH = C_HEADS * C_KEY_DIM
C_VAL_WIDTH = C_HEADS * C_VAL_DIM
C_LOWRANK = 16
C_GATE_TEMP = 16.0
GLA_CHUNK = 64
N_EXPERTS = 8
GROUP_W = 1536

LANES = 128
V7X_VMEM_BYTES = 64 * 1024 * 1024
VMEM_LIMIT = V7X_VMEM_BYTES - 8 * 1024 * 1024

TM = 512
TQ = 512
ATT_HEADS = 8
ACC_ROWS = A_HEAD_DIM + 16
ATT_GUARD = 64.0
TL = 256
MIX_BATCH = 2
FF_CHUNK = 256
TMM = 512
ROW_GRAN = 16
SORT_ROWS = 2 * TM + N_EXPERTS * ROW_GRAN


def _cparams(sem):
    return pltpu.CompilerParams(dimension_semantics=sem, vmem_limit_bytes=VMEM_LIMIT)


def _dot(a, b):
    return jnp.dot(a, b, preferred_element_type=F32)


def _dot_nt(a, b):
    return lax.dot_general(a, b, (((1,), (1,)), ((), ())), preferred_element_type=F32)


def _rms(x, g):
    return x * lax.rsqrt(jnp.mean(x * x, axis=-1, keepdims=True) + EPS) * g


def _log_sigmoid(z):
    return jnp.minimum(z, 0.0) - jnp.log(1.0 + jnp.exp(-jnp.abs(z)))


def _sigmoid(z):
    return 1.0 / (1.0 + jnp.exp(-z))


def _split3(x):
    hi = x.astype(BF16)
    r1 = x - hi.astype(F32)
    mid = r1.astype(BF16)
    lo = (r1 - mid.astype(F32)).astype(BF16)
    return hi, mid, lo


def _const_spec(shape):
    return pl.BlockSpec(shape, lambda *_: (0,) * len(shape))


def _inproj_kernel(x_ref, g_ref, wbig_ref, wvt_ref, wsmall_ref, bf_ref, tril_ref, place_ref,
                   qp_ref, kp_ref, vt_ref, conv_ref, gla_ref, small_ref, carry_ref, *, tiles_per_seq):
    @pl.when(pl.program_id(0) % tiles_per_seq == 0)
    def _():
        carry_ref[...] = jnp.zeros_like(carry_ref)

    h = _rms(x_ref[...], g_ref[...]).astype(BF16)
    small = _dot(h, wsmall_ref[...])
    small_ref[...] = small
    q_all = _dot(h, wbig_ref[:, 0:A_WIDTH])
    k_all = _dot(h, wbig_ref[:, A_WIDTH:2 * A_WIDTH])
    off = 2 * A_WIDTH
    for ref in (conv_ref, gla_ref):
        width = ref.shape[1]
        for c in range(0, width, 512):
            ref[:, c:c + 512] = _dot(h, wbig_ref[:, off + c: off + c + 512]).astype(BF16)
        off += width
    vt_ref[0] = _dot_nt(wvt_ref[...], h).astype(BF16)

    lf = LOG2E * _log_sigmoid(small + bf_ref[...])
    hi, mid, lo = _split3(lf)
    tril = tril_ref[...]
    cs = _dot(tril, hi) + _dot(tril, mid) + _dot(tril, lo) + carry_ref[...]
    carry_ref[...] = cs[TM - 1:TM, :]
    fterms = _dot(jnp.concatenate(_split3(cs), axis=1), place_ref[...])

    lane = lax.broadcasted_iota(I32, (1, LANES), 1)
    is_x = lane < A_HEAD_DIM
    is_a = (lane >= A_HEAD_DIM) & (lane < A_HEAD_DIM + 3)
    is_b = (lane >= A_HEAD_DIM + 3) & (lane < A_HEAD_DIM + 6)
    for p in range(A_HEADS // 2):
        q2 = q_all[:, p * LANES:(p + 1) * LANES].astype(BF16).astype(F32)
        k2 = k_all[:, p * LANES:(p + 1) * LANES].astype(BF16).astype(F32)
        halves = ((q2, k2), (pltpu.roll(q2, A_HEAD_DIM, 1), pltpu.roll(k2, A_HEAD_DIM, 1)))
        for j in range(2):
            hd = 2 * p + j
            fh = pltpu.roll(fterms, (A_HEAD_DIM - 8 * hd) % LANES, 1)
            qh, kh = halves[j]
            qp_ref[:, hd * LANES:(hd + 1) * LANES] = jnp.where(
                is_x, qh, jnp.where(is_a, fh, jnp.where(is_b, 1.0, 0.0))).astype(BF16)
            kp_ref[:, hd * LANES:(hd + 1) * LANES] = jnp.where(
                is_x, kh, jnp.where(is_a, 1.0, jnp.where(is_b, -fh, 0.0))).astype(BF16)


def _aug_placement():
    place = np.zeros((3, LANES, LANES), np.float32)
    for h in range(A_HEADS):
        for c in range(3):
            place[c, h, 8 * h + c] = 1.0
            place[c, h, 8 * h + 3 + c] = 1.0
    return jnp.asarray(place.reshape(3 * LANES, LANES), BF16)


def _inproj(x2, g, wbig, wvt, wsmall, bf_pad, tril, tiles_per_seq):
    t, d = x2.shape
    place = _aug_placement()
    row = lambda w: pl.BlockSpec((TM, w), lambda i: (i, 0))
    return pl.pallas_call(
        functools.partial(_inproj_kernel, tiles_per_seq=tiles_per_seq),
        grid=(t // TM,),
        in_specs=[row(d), _const_spec((1, d)), _const_spec(wbig.shape), _const_spec(wvt.shape),
                  _const_spec(wsmall.shape), _const_spec((1, LANES)), _const_spec((TM, TM)),
                  _const_spec(place.shape)],
        out_specs=[row(A_HEADS * LANES), row(A_HEADS * LANES),
                   pl.BlockSpec((1, A_WIDTH, TM), lambda i: (i, 0, 0)), row(GROUP_W), row(GROUP_W), row(LANES)],
        out_shape=[jax.ShapeDtypeStruct((t, A_HEADS * LANES), BF16), jax.ShapeDtypeStruct((t, A_HEADS * LANES), BF16),
                   jax.ShapeDtypeStruct((t // TM, A_WIDTH, TM), BF16),
                   jax.ShapeDtypeStruct((t, GROUP_W), BF16), jax.ShapeDtypeStruct((t, GROUP_W), BF16),
                   jax.ShapeDtypeStruct((t, LANES), F32)],
        scratch_shapes=[pltpu.VMEM((1, LANES), F32)],
        compiler_params=_cparams(("arbitrary",)),
        name="inproj",
    )(x2, g, wbig, wvt, wsmall, bf_pad, tril, place)


def _attn_kernel(q_ref, k_ref, vt_ref, *rest, n_side):
    side_in, o_ref, side_out = rest[:n_side], rest[n_side], rest[n_side + 1:2 * n_side + 1]
    m_ref, acc_ref, exc_ref = rest[2 * n_side + 1:]
    for src_ref, dst_ref in zip(side_in, side_out):
        dst_ref[...] = src_ref[...].astype(BF16)
    qi = pl.program_id(2)
    ones = jnp.ones((ACC_ROWS - A_HEAD_DIM, TQ), BF16)

    def _score(ki, j):
        return _dot_nt(k_ref[ki, :, j * LANES:(j + 1) * LANES], q_ref[:, j * LANES:(j + 1) * LANES])

    def _values(ki, j):
        return jnp.concatenate([vt_ref[ki, j * A_HEAD_DIM:(j + 1) * A_HEAD_DIM, :], ones], axis=0)

    def _diag_step(ki):
        key = lax.broadcasted_iota(I32, (TQ, TQ), 0)
        qry = lax.broadcasted_iota(I32, (TQ, TQ), 1)
        keep = key <= qry
        scores = [_score(ki, j) for j in range(ATT_HEADS)]
        for j in range(ATT_HEADS):
            st = jnp.where(keep, scores[j], NEG)
            m_new = jnp.max(st, axis=0, keepdims=True)
            acc_ref[j] = _dot(_values(ki, j), jnp.exp2(st - m_new).astype(BF16))
            m_ref[j] = m_new

    def _exact_step(ki):
        scores = [_score(ki, j) for j in range(ATT_HEADS)]
        for j in range(ATT_HEADS):
            st = scores[j]
            m_prev = m_ref[j]
            m_new = jnp.maximum(m_prev, jnp.max(st, axis=0, keepdims=True))
            pr = jnp.exp2(st - m_new).astype(BF16)
            acc_ref[j] = jnp.exp2(m_prev - m_new) * acc_ref[j] + _dot(_values(ki, j), pr)
            m_ref[j] = m_new

    def _fast_step(ki, carry):
        excess = exc_ref[...]
        pending = _score(ki, 0)
        for j in range(ATT_HEADS):
            st = pending
            if j + 1 < ATT_HEADS:
                pending = _score(ki, j + 1)
            m_prev = m_ref[j]
            pr = jnp.exp2(st - m_prev).astype(BF16)
            tmax = jnp.max(st, axis=0, keepdims=True)
            excess = jnp.maximum(excess, tmax - m_prev)
            m_new = jnp.maximum(m_prev, tmax)
            acc_ref[j] = (acc_ref[j] + _dot(_values(ki, j), pr)) * jnp.exp2(m_prev - m_new)
            m_ref[j] = m_new
        exc_ref[...] = excess
        return carry

    def _exact_below(ki, carry):
        _exact_step(ki)
        return carry

    exc_ref[...] = jnp.zeros_like(exc_ref)
    _diag_step(qi)
    lax.fori_loop(0, qi, _fast_step, 0)

    @pl.when(jnp.max(exc_ref[...]) > ATT_GUARD)
    def _():
        _diag_step(qi)
        lax.fori_loop(0, qi, _exact_below, 0)

    ot = jnp.concatenate([acc_ref[j, 0:A_HEAD_DIM, :] / acc_ref[j, A_HEAD_DIM:A_HEAD_DIM + 1, :]
                          for j in range(ATT_HEADS)], axis=0)
    o_ref[...] = ot.T.astype(BF16)


def _attention(qp, kp, vt, b, s, side=()):
    nq = s // TQ
    ng = A_HEADS // ATT_HEADS
    steps = b * ng * nq
    kp3 = kp.reshape(b * nq, TQ, A_HEADS * LANES)
    assert all(a.shape[0] % (steps * ROW_GRAN) == 0 for a in side)
    slab = lambda a: pl.BlockSpec((a.shape[0] // steps, a.shape[1]), lambda bi, p, qi: ((bi * ng + p) * nq + qi, 0))
    outs = pl.pallas_call(
        functools.partial(_attn_kernel, n_side=len(side)),
        grid=(b, ng, nq),
        in_specs=[pl.BlockSpec((TQ, ATT_HEADS * LANES), lambda bi, p, qi: (bi * nq + qi, p)),
                  pl.BlockSpec((nq, TQ, ATT_HEADS * LANES), lambda bi, p, qi: (bi, 0, p)),
                  pl.BlockSpec((nq, ATT_HEADS * A_HEAD_DIM, TQ), lambda bi, p, qi: (bi, p, 0))]
        + [slab(a) for a in side],
        out_specs=[pl.BlockSpec((TQ, ATT_HEADS * A_HEAD_DIM), lambda bi, p, qi: (bi * nq + qi, p))]
        + [slab(a) for a in side],
        out_shape=[jax.ShapeDtypeStruct((b * s, A_WIDTH), BF16)]
        + [jax.ShapeDtypeStruct(a.shape, BF16) for a in side],
        scratch_shapes=[pltpu.VMEM((ATT_HEADS, 1, TQ), F32), pltpu.VMEM((ATT_HEADS, ACC_ROWS, TQ), F32),
                        pltpu.VMEM((1, TQ), F32)],
        compiler_params=_cparams(("parallel", "parallel", "arbitrary")),
        name="fox_attention",
    )(qp, kp3, vt, *side)
    return outs[0], outs[1:]


def _gla_kernel(gla_ref, small_ref, wa2_ref, ba_ref, ggla_ref, tri_ref, oc_ref, st_ref, sall_ref):
    @pl.when(pl.program_id(1) == 0)
    def _():
        st_ref[...] = jnp.zeros_like(st_ref)

    nc = TL // GLA_CHUNK
    nb = MIX_BATCH

    tri = tri_ref[...]
    q_t, k_t, ks_t, dec_t = [], [], [], []
    for i in range(nb):
        la = _dot(small_ref[i].astype(BF16), wa2_ref[...]) + ba_ref[...]
        la = _log_sigmoid(la) * (1.0 / C_GATE_TEMP)
        hi, mid, lo = _split3(la)
        bcum = _dot(tri, hi) + _dot(tri, mid) + _dot(tri, lo)
        bl = bcum.reshape(nc, GLA_CHUNK, C_KEY_WIDTH)[:, GLA_CHUNK - 1:GLA_CHUNK, :]
        blast = jnp.broadcast_to(bl, (nc, GLA_CHUNK, C_KEY_WIDTH)).reshape(TL, C_KEY_WIDTH)
        q = gla_ref[i, :, 0:C_KEY_WIDTH].astype(F32)
        k = gla_ref[i, :, C_KEY_WIDTH:2 * C_KEY_WIDTH].astype(F32)
        q_t.append(q * (C_KEY_DIM ** -0.5) * jnp.exp(bcum))
        k_t.append((k * jnp.exp(-bcum)).astype(BF16))
        ks_t.append((k * jnp.exp(blast - bcum)).T)
        dec_t.append(jnp.broadcast_to(jnp.exp(bl), (nc, LANES, C_KEY_WIDTH)).reshape(nc * LANES, C_KEY_WIDTH).T)

    r_i = lax.broadcasted_iota(I32, (TL, TL), 0)
    c_i = lax.broadcasted_iota(I32, (TL, TL), 1)
    same = (r_i // GLA_CHUNK) == (c_i // GLA_CHUNK)
    intra = same & (c_i <= r_i)
    lane_k = lax.broadcasted_iota(I32, (1, C_KEY_WIDTH), 1)
    lane = lax.broadcasted_iota(I32, (1, LANES), 1)

    for h in range(C_HEADS):
        for i in range(nb):
            v_h = gla_ref[i, :, 2 * C_KEY_WIDTH + h * C_VAL_DIM: 2 * C_KEY_WIDTH + (h + 1) * C_VAL_DIM]
            q_h = jnp.where(lane_k // C_KEY_DIM == h, q_t[i], 0.0).astype(BF16)
            a = jnp.where(intra, _dot_nt(q_h, k_t[i]), 0.0).astype(BF16)

            q2 = q_t[i][:, (h // 2) * LANES:(h // 2 + 1) * LANES]
            q2r = pltpu.roll(q2, C_KEY_DIM, 1)
            dup = jnp.where((lane < C_KEY_DIM) == (h % 2 == 0), q2, q2r)
            q_exp = jnp.where(same, jnp.concatenate([dup] * (TL // LANES), axis=1), 0.0).astype(BF16)

            ks_h = ks_t[i][h * C_KEY_DIM:(h + 1) * C_KEY_DIM, :]
            k_exp = jnp.where(same, jnp.concatenate([ks_h] * nc, axis=0), 0.0).astype(BF16)
            kv = _dot(k_exp, v_h)

            st = st_ref[i, h]
            for c in range(nc):
                sall_ref[i, h, c * GLA_CHUNK:(c + 1) * GLA_CHUNK, :] = st.astype(BF16)
                dec = dec_t[i][h * C_KEY_DIM:(h + 1) * C_KEY_DIM, c * LANES:(c + 1) * LANES]
                st = dec * st + kv[c * GLA_CHUNK:(c + 1) * GLA_CHUNK, :]
            st_ref[i, h] = st

            o = _dot(a, v_h) + _dot(q_exp, sall_ref[i, h])
            o = o * lax.rsqrt(jnp.mean(o * o, axis=-1, keepdims=True) + EPS)
            o = o * ggla_ref[:, h * C_VAL_DIM:(h + 1) * C_VAL_DIM]
            r = gla_ref[i, :, 4 * C_KEY_WIDTH + h * C_VAL_DIM: 4 * C_KEY_WIDTH + (h + 1) * C_VAL_DIM].astype(F32)
            oc_ref[i, :, h * C_VAL_DIM:(h + 1) * C_VAL_DIM] = (o * (r * _sigmoid(r))).astype(BF16)


def _gla(gla_in, small, wa2p, ba, ggla, tri_bd, b, s):
    ns = s // TL
    blk = lambda w: pl.BlockSpec((MIX_BATCH, TL, w), lambda bi, si: (bi, si, 0))
    oc = pl.pallas_call(
        _gla_kernel,
        grid=(b // MIX_BATCH, ns),
        in_specs=[blk(GROUP_W), blk(LANES),
                  _const_spec((LANES, C_KEY_WIDTH)),
                  _const_spec((1, C_KEY_WIDTH)),
                  _const_spec((1, C_VAL_WIDTH)),
                  _const_spec((TL, TL))],
        out_specs=blk(C_VAL_WIDTH),
        out_shape=jax.ShapeDtypeStruct((b, s, C_VAL_WIDTH), BF16),
        scratch_shapes=[pltpu.VMEM((MIX_BATCH, C_HEADS, C_KEY_DIM, C_VAL_DIM), F32),
                        pltpu.VMEM((MIX_BATCH, C_HEADS, TL, C_VAL_DIM), BF16)],
        compiler_params=_cparams(("parallel", "arbitrary")),
        name="gla",
    )(gla_in.reshape(b, s, GROUP_W), small.reshape(b, s, LANES), wa2p, ba, ggla, tri_bd)
    return oc.reshape(b * s, C_VAL_WIDTH)


def _merge_kernel(x_ref, g_ref, oa_ref, conv_ref, convw_ref, oc_ref, wg_ref, bgate_ref, wpa_ref, wpb_ref,
                  wpc_ref, wo_ref, out_ref, zprev_ref, *, tiles_per_seq):
    @pl.when(pl.program_id(0) % tiles_per_seq == 0)
    def _():
        zprev_ref[...] = jnp.zeros_like(zprev_ref)

    u = conv_ref[:, 0:B_WIDTH].astype(F32)
    bg = conv_ref[:, B_WIDTH:2 * B_WIDTH].astype(F32)
    cg = conv_ref[:, 2 * B_WIDTH:3 * B_WIDTH].astype(F32)
    z = cg * u
    row = lax.broadcasted_iota(I32, (TM, B_WIDTH), 0)
    zp = zprev_ref[...]
    z1 = jnp.where(row == 0, zp[7:8, :], pltpu.roll(z, 1, 0))
    z2 = jnp.where(row == 0, zp[6:7, :], jnp.where(row == 1, zp[7:8, :], pltpu.roll(z, 2, 0)))
    cw = convw_ref[...]
    o_b = (bg * (cw[0:1, :] * z2 + cw[1:2, :] * z1 + cw[2:3, :] * z)).astype(BF16)
    zprev_ref[...] = z[TM - 8:TM, :]

    x = x_ref[...]
    d = x.shape[1]
    h = _rms(x, g_ref[...]).astype(BF16)
    merged = None
    for j, (o, wp_ref) in enumerate(((oa_ref[...], wpa_ref), (o_b, wpb_ref), (oc_ref[...], wpc_ref))):
        gate = _sigmoid(_dot(h, wg_ref[0, :, j * d:(j + 1) * d]) + bgate_ref[:, j * d:(j + 1) * d])
        term = gate * _dot(o, wp_ref[0])
        merged = term if merged is None else merged + term
    out_ref[...] = x + _dot(merged.astype(BF16), wo_ref[0])


def _layer_spec(stacked, l):
    return pl.BlockSpec((1,) + stacked.shape[1:], lambda *_: (l, 0, 0))


def _merge(x2, g, oa, conv_in, convw8, oc, wg, bgate, wpa, wpb, wpc, wo, l, tiles_per_seq):
    t, d = x2.shape
    row = lambda w: pl.BlockSpec((TM, w), lambda i: (i, 0))
    return pl.pallas_call(
        functools.partial(_merge_kernel, tiles_per_seq=tiles_per_seq),
        grid=(t // TM,),
        in_specs=[row(d), _const_spec((1, d)), row(A_WIDTH), row(GROUP_W), _const_spec((8, B_WIDTH)),
                  row(C_VAL_WIDTH), _layer_spec(wg, l), _const_spec(bgate.shape), _layer_spec(wpa, l),
                  _layer_spec(wpb, l), _layer_spec(wpc, l), _layer_spec(wo, l)],
        out_specs=row(d),
        out_shape=jax.ShapeDtypeStruct((t, d), F32),
        scratch_shapes=[pltpu.VMEM((8, B_WIDTH), F32)],
        compiler_params=_cparams(("arbitrary",)),
        name="merge_outproj",
    )(x2, g, oa, conv_in, convw8, oc, wg, bgate, wpa, wpb, wpc, wo)


def _swiglu_tile(h, w1_ref, w3_ref, w2_ref):
    ff = w1_ref.shape[-1]
    acc = None
    for c in range(0, ff, FF_CHUNK):
        a = _dot(h, w1_ref[:, c:c + FF_CHUNK])
        b = _dot(h, w3_ref[:, c:c + FF_CHUNK])
        g = (a * _sigmoid(a) * b).astype(BF16)
        part = _dot(g, w2_ref[c:c + FF_CHUNK, :])
        acc = part if acc is None else acc + part
    return acc


def _ffn_kernel(x_ref, g_ref, w1_ref, w3_ref, w2_ref, *rest):
    if len(rest) == 1:
        (out_ref,) = rest
    else:
        m1_ref, m3_ref, m2_ref, out_ref, c1_ref, c3_ref, c2_ref, zero_ref = rest
        c1_ref[...] = m1_ref[...].astype(BF16)
        c3_ref[...] = m3_ref[...].astype(BF16)
        c2_ref[...] = m2_ref[...].astype(BF16)
        zero_ref[...] = jnp.zeros_like(zero_ref)
    x = x_ref[...]
    h = _rms(x, g_ref[...]).astype(BF16)
    out_ref[...] = x + _swiglu_tile(h, w1_ref.at[0], w3_ref.at[0], w2_ref.at[0])


def _dense_ffn(x2, g, w1, w3, w2, i, moe_prep=None):
    t, d = x2.shape
    steps = t // TM
    row = pl.BlockSpec((TM, d), lambda i: (i, 0))
    in_specs = [row, _const_spec((1, d)), _layer_spec(w1, i), _layer_spec(w3, i), _layer_spec(w2, i)]
    out_specs = [row]
    out_shape = [jax.ShapeDtypeStruct((t, d), F32)]
    args = [x2, g, w1, w3, w2]
    if moe_prep is not None:
        m1, m3, m2, sorted_rows = moe_prep
        flat = [m1.reshape(-1, m1.shape[-1]), m3.reshape(-1, m3.shape[-1]), m2.reshape(-1, m2.shape[-1])]
        shapes = [a.shape for a in flat] + [(sorted_rows, d)]
        assert all(r % (steps * ROW_GRAN) == 0 for r, _ in shapes)
        slab = lambda shp: pl.BlockSpec((shp[0] // steps, shp[1]), lambda i: (i, 0))
        in_specs += [slab(s) for s in shapes[:3]]
        out_specs += [slab(s) for s in shapes]
        out_shape += [jax.ShapeDtypeStruct(s, BF16) for s in shapes]
        args += flat
    outs = pl.pallas_call(
        _ffn_kernel,
        grid=(steps,),
        in_specs=in_specs,
        out_specs=out_specs,
        out_shape=out_shape,
        compiler_params=_cparams(("parallel",)),
        name="dense_ffn",
    )(*args)
    if moe_prep is None:
        return outs[0], None
    m1, m3, m2, _ = moe_prep
    return outs[0], (outs[1].reshape(m1.shape), outs[2].reshape(m3.shape), outs[3].reshape(m2.shape), outs[4])


def _router_kernel(x_ref, g_ref, wr_ref, stril_ref, h_ref, rt_ref, seg_ref):
    h = _rms(x_ref[...], g_ref[...])
    h_hi = h.astype(BF16)
    h_ref[...] = h_hi
    h_lo = (h - h_hi.astype(F32)).astype(BF16)
    wr = wr_ref[...]
    w_hi = wr.astype(BF16)
    w_lo = (wr - w_hi.astype(F32)).astype(BF16)
    both = _dot(h_hi, jnp.concatenate([w_hi, w_lo], axis=1))
    logits = both[:, 0:LANES] + both[:, LANES:2 * LANES] + _dot(h_lo, w_hi)

    lane = lax.broadcasted_iota(I32, (TM, LANES), 1)
    lane_f = lane.astype(F32)
    lg = jnp.where(lane < N_EXPERTS, logits, NEG)
    m1 = jnp.max(lg, axis=-1, keepdims=True)
    i1 = jnp.min(jnp.where(lg == m1, lane_f, float(LANES)), axis=-1, keepdims=True)
    oh1 = lane_f == i1
    lg2 = jnp.where(oh1, NEG, lg)
    m2 = jnp.max(lg2, axis=-1, keepdims=True)
    i2 = jnp.min(jnp.where(lg2 == m2, lane_f, float(LANES)), axis=-1, keepdims=True)
    oh2 = lane_f == i2
    e = jnp.exp(m2 - m1)
    w1 = 1.0 / (1.0 + e)
    w2 = e / (1.0 + e)

    sel = jnp.where(oh1 | oh2, 1.0, 0.0)
    before = _dot(stril_ref[...], sel.astype(BF16))
    cnt = jnp.sum(sel, axis=0, keepdims=True)
    pc = jnp.floor((cnt + (ROW_GRAN - 1)) * (1.0 / ROW_GRAN)) * ROW_GRAN
    lane1 = lax.broadcasted_iota(I32, (1, LANES), 1)
    incl = pc
    for sh in (1, 2, 4):
        incl = incl + jnp.where(lane1 >= sh, pltpu.roll(incl, sh, 1), 0.0)
    off = incl - pc
    pos = before + off
    d1 = jnp.sum(jnp.where(oh1, pos, 0.0), axis=-1, keepdims=True)
    d2 = jnp.sum(jnp.where(oh2, pos, 0.0), axis=-1, keepdims=True)
    rt_ref[...] = jnp.where(lane == 0, d1, jnp.where(lane == 1, d2, jnp.where(lane == 2, w1, jnp.where(lane == 3, w2, 0.0))))
    row8 = lax.broadcasted_iota(I32, (8, LANES), 0)
    seg_ref[0] = jnp.where(row8 == 0, pc, jnp.where(row8 == 1, off, 0.0))


def _router(x2, g, wr_pad, stril):
    t, d = x2.shape
    nb = t // TM
    row = lambda w: pl.BlockSpec((TM, w), lambda i: (i, 0))
    return pl.pallas_call(
        _router_kernel,
        grid=(nb,),
        in_specs=[row(d), _const_spec((1, d)), _const_spec(wr_pad.shape), _const_spec((TM, TM))],
        out_specs=[row(d), row(LANES), pl.BlockSpec((1, 8, LANES), lambda i: (i, 0, 0))],
        out_shape=[jax.ShapeDtypeStruct((t, d), BF16), jax.ShapeDtypeStruct((t, LANES), F32),
                   jax.ShapeDtypeStruct((nb, 8, LANES), F32)],
        compiler_params=_cparams(("parallel",)),
        name="moe_router",
    )(x2, g, wr_pad, stril)


def _chunk_copy(src_ref, src_row, dst_ref, dst_row, sem):
    return pltpu.make_async_copy(src_ref.at[pl.ds(pl.multiple_of(src_row, ROW_GRAN), ROW_GRAN)],
                                 dst_ref.at[pl.ds(pl.multiple_of(dst_row, ROW_GRAN), ROW_GRAN)], sem)


def _segment_starts(b, gstart_ref, nch_ref, off_ref, hbm_ref, buf_ref, sem, to_hbm):
    for e in range(N_EXPERTS):
        n = nch_ref[b * N_EXPERTS + e]
        g0 = gstart_ref[b * N_EXPERTS + e]
        o0 = off_ref[b * N_EXPERTS + e]

        def start(c, carry, g0=g0, o0=o0):
            if to_hbm:
                _chunk_copy(buf_ref, o0 + c * ROW_GRAN, hbm_ref, g0 + c * ROW_GRAN, sem).start()
            else:
                _chunk_copy(hbm_ref, g0 + c * ROW_GRAN, buf_ref, o0 + c * ROW_GRAN, sem).start()
            return carry

        lax.fori_loop(0, n, start, 0)


def _segment_waits(b, nch_ref, buf_ref, sem):
    total = 0
    for e in range(N_EXPERTS):
        total = total + nch_ref[b * N_EXPERTS + e]

    def wait(c, carry):
        _chunk_copy(buf_ref, 0, buf_ref, 0, sem).wait()
        return carry

    lax.fori_loop(0, total, wait, 0)


def _dispatch_kernel(gstart_ref, nch_ref, off_ref, h_ref, rt_ref, xs_in_ref, xs_ref, buf_ref, sem):
    del xs_in_ref
    b = pl.program_id(0)
    slot = b % 2
    dest = rt_ref[...].T
    r = lax.broadcasted_iota(I32, (SORT_ROWS, TM), 0).astype(F32)
    perm = jnp.where((r == dest[0:1, :]) | (r == dest[1:2, :]), 1.0, 0.0).astype(BF16)
    buf_ref[slot] = _dot(perm, h_ref[...]).astype(BF16)
    _segment_starts(b, gstart_ref, nch_ref, off_ref, xs_ref, buf_ref.at[slot], sem.at[slot], to_hbm=True)

    @pl.when(b > 0)
    def _():
        _segment_waits(b - 1, nch_ref, buf_ref.at[1 - slot], sem.at[1 - slot])

    @pl.when(b == pl.num_programs(0) - 1)
    def _():
        _segment_waits(b, nch_ref, buf_ref.at[slot], sem.at[slot])


def _dispatch(gstart, nch, off, h2, rt, xs_zero):
    t, d = h2.shape
    grid_spec = pltpu.PrefetchScalarGridSpec(
        num_scalar_prefetch=3,
        grid=(t // TM,),
        in_specs=[pl.BlockSpec((TM, d), lambda i, *_: (i, 0)),
                  pl.BlockSpec((TM, LANES), lambda i, *_: (i, 0)),
                  pl.BlockSpec(memory_space=pl.ANY)],
        out_specs=pl.BlockSpec(memory_space=pl.ANY),
        scratch_shapes=[pltpu.VMEM((2, SORT_ROWS, d), BF16), pltpu.SemaphoreType.DMA((2,))],
    )
    return pl.pallas_call(
        _dispatch_kernel,
        grid_spec=grid_spec,
        out_shape=jax.ShapeDtypeStruct(xs_zero.shape, xs_zero.dtype),
        input_output_aliases={5: 0},
        compiler_params=_cparams(("arbitrary",)),
        name="moe_dispatch",
    )(gstart, nch, off, h2, rt, xs_zero)


def _group_ffn_kernel(te_ref, ta_ref, ts_ref, xs_ref, w1_ref, w3_ref, w2_ref, y_ref):
    del te_ref, ts_ref
    active = ta_ref[pl.program_id(0)] == 1

    @pl.when(active)
    def _():
        y_ref[...] = _swiglu_tile(xs_ref[...], w1_ref.at[0], w3_ref.at[0], w2_ref.at[0]).astype(BF16)

    @pl.when(jnp.logical_not(active))
    def _():
        y_ref[...] = jnp.zeros_like(y_ref)


def _group_ffn(tile_expert, tile_active, tile_src, xs, w1, w3, w2):
    ns, d = xs.shape
    ff = w1.shape[-1]
    grid_spec = pltpu.PrefetchScalarGridSpec(
        num_scalar_prefetch=3,
        grid=(ns // TMM,),
        in_specs=[pl.BlockSpec((TMM, d), lambda i, te, ta, ts: (ts[i], 0)),
                  pl.BlockSpec((1, d, ff), lambda i, te, ta, ts: (te[i], 0, 0)),
                  pl.BlockSpec((1, d, ff), lambda i, te, ta, ts: (te[i], 0, 0)),
                  pl.BlockSpec((1, ff, d), lambda i, te, ta, ts: (te[i], 0, 0))],
        out_specs=pl.BlockSpec((TMM, d), lambda i, te, ta, ts: (i, 0)),
    )
    return pl.pallas_call(
        _group_ffn_kernel,
        grid_spec=grid_spec,
        out_shape=jax.ShapeDtypeStruct((ns, d), BF16),
        compiler_params=_cparams(("arbitrary",)),
        name="moe_group_ffn",
    )(tile_expert, tile_active, tile_src, xs, w1, w3, w2)


def _combine_kernel(gstart_ref, nch_ref, off_ref, x_ref, rt_ref, gf_ref, y_ref, out_ref, buf_ref, sem,
                    *, final_norm):
    b = pl.program_id(0)
    slot = b % 2

    def fetch(blk, s):
        buf_ref[s] = jnp.zeros(buf_ref.shape[1:], BF16)
        _segment_starts(blk, gstart_ref, nch_ref, off_ref, y_ref, buf_ref.at[s], sem.at[s], to_hbm=False)

    @pl.when(b == 0)
    def _():
        fetch(b, slot)

    @pl.when(b + 1 < pl.num_programs(0))
    def _():
        fetch(b + 1, 1 - slot)

    _segment_waits(b, nch_ref, buf_ref.at[slot], sem.at[slot])
    rt = rt_ref[...]
    r = lax.broadcasted_iota(I32, (TM, SORT_ROWS), 1).astype(F32)
    ybuf = buf_ref[slot]
    y1 = _dot(jnp.where(r == rt[:, 0:1], 1.0, 0.0).astype(BF16), ybuf)
    y2 = _dot(jnp.where(r == rt[:, 1:2], 1.0, 0.0).astype(BF16), ybuf)
    x = x_ref[...] + rt[:, 2:3] * y1 + rt[:, 3:4] * y2
    out_ref[...] = _rms(x, gf_ref[...]) if final_norm else x


def _combine(gstart, nch, off, x2, rt, g_final, y, final_norm):
    t, d = x2.shape
    grid_spec = pltpu.PrefetchScalarGridSpec(
        num_scalar_prefetch=3,
        grid=(t // TM,),
        in_specs=[pl.BlockSpec((TM, d), lambda i, *_: (i, 0)),
                  pl.BlockSpec((TM, LANES), lambda i, *_: (i, 0)),
                  pl.BlockSpec((1, d), lambda i, *_: (0, 0)),
                  pl.BlockSpec(memory_space=pl.ANY)],
        out_specs=pl.BlockSpec((TM, d), lambda i, *_: (i, 0)),
        scratch_shapes=[pltpu.VMEM((2, SORT_ROWS, d), BF16), pltpu.SemaphoreType.DMA((2,))],
    )
    return pl.pallas_call(
        functools.partial(_combine_kernel, final_norm=final_norm),
        grid_spec=grid_spec,
        out_shape=jax.ShapeDtypeStruct((t, d), F32),
        compiler_params=_cparams(("arbitrary",)),
        name="moe_combine",
    )(gstart, nch, off, x2, rt, g_final, y)


def _norm_kernel(x_ref, g_ref, out_ref):
    out_ref[...] = _rms(x_ref[...], g_ref[...])


def _final_norm(x2, g):
    t, d = x2.shape
    row = pl.BlockSpec((TM, d), lambda i: (i, 0))
    return pl.pallas_call(
        _norm_kernel, grid=(t // TM,), in_specs=[row, _const_spec((1, d))], out_specs=row,
        out_shape=jax.ShapeDtypeStruct((t, d), F32), compiler_params=_cparams(("parallel",)),
        name="final_norm",
    )(x2, g)


def _tril_blocks(n, block, strict=False):
    r = jnp.arange(n)[:, None]
    c = jnp.arange(n)[None, :]
    keep = (c < r) if strict else (c <= r)
    keep = keep & ((r // block) == (c // block))
    return keep.astype(BF16)


def _sorted_rows(t):
    return ((2 * t + (t // TM) * N_EXPERTS * ROW_GRAN) // TMM + N_EXPERTS) * TMM


def _moe_layer(x2, g_ffn, w_router, w1, w3, w2, xs_zero, g_final, final_norm):
    t, d = x2.shape
    nb = t // TM
    wr_pad = jnp.pad(w_router, ((0, 0), (0, LANES - N_EXPERTS)))
    h2, rt, seg = _router(x2, g_ffn.reshape(1, d), wr_pad, _tril_blocks(TM, TM, strict=True))

    pc = seg[:, 0, :N_EXPERTS].astype(I32)
    off = seg[:, 1, :N_EXPERTS].astype(I32)
    region = ((jnp.sum(pc, axis=0) + TMM - 1) // TMM) * TMM
    ends = jnp.cumsum(region)
    gstart = (ends - region)[None, :] + jnp.cumsum(pc, axis=0) - pc
    n_tiles = _sorted_rows(t) // TMM
    tile_start = jnp.arange(n_tiles, dtype=I32) * TMM
    tile_expert = jnp.minimum(jnp.sum(tile_start[:, None] >= ends[None, :], axis=1), N_EXPERTS - 1).astype(I32)
    tile_active = (tile_start < ends[-1]).astype(I32)
    last_active = jnp.maximum(ends[-1] // TMM - 1, 0)
    tile_src = jnp.minimum(jnp.arange(n_tiles, dtype=I32), last_active).astype(I32)
    seg_args = (gstart.reshape(-1).astype(I32), (pc // ROW_GRAN).reshape(-1), off.reshape(-1))

    xs = _dispatch(*seg_args, h2, rt, xs_zero)
    y = _group_ffn(tile_expert, tile_active, tile_src, xs, w1, w3, w2)
    return _combine(*seg_args, x2, rt, g_final.reshape(1, d), y, final_norm)


def kernel(x, g_mix, w_in, b_f, conv_w, w_a2, b_a, g_gla, w_pa, w_pb, w_pc, w_gate, b_gate, w_o,
           g_ffn, ffn_w1, ffn_w3, ffn_w2, w_router, moe_w1, moe_w3, moe_w2, g_final):
    b, s, d = x.shape
    depth = g_mix.shape[0]
    t = b * s
    assert t % TM == 0 and s % TQ == 0 and TQ == TM and s % TL == 0 and TL % GLA_CHUNK == 0 and b % MIX_BATCH == 0
    x2 = x.reshape(t, d)
    tril_full = _tril_blocks(TM, TM)
    tril_chunk = _tril_blocks(TL, GLA_CHUNK)
    o_af = 3 * A_WIDTH
    o_b = o_af + A_HEADS
    o_c = o_b + 3 * B_WIDTH
    o_ca = o_c + 2 * C_KEY_WIDTH + C_VAL_WIDTH
    o_cr = o_ca + C_LOWRANK

    done_final = False
    for l in range(depth):
        w = w_in[l]
        wq = w[:, 0:A_WIDTH] * (A_HEAD_DIM ** -0.5 * LOG2E)
        wbig = jnp.concatenate([wq, w[:, A_WIDTH:2 * A_WIDTH], w[:, o_b:o_c], w[:, o_c:o_ca], w[:, o_cr:]],
                               axis=1).astype(BF16)
        wvt = w[:, 2 * A_WIDTH:o_af].T.astype(BF16)
        wsmall = jnp.concatenate([w[:, o_af:o_b], w[:, o_ca:o_cr],
                                  jnp.zeros((d, LANES - A_HEADS - C_LOWRANK), F32)], axis=1).astype(BF16)
        gm = g_mix[l].reshape(1, d)
        bf_pad = jnp.pad(b_f[l], (0, LANES - A_HEADS)).reshape(1, LANES)
        qp, kp, vt, conv_in, gla_in, small = _inproj(x2, gm, wbig, wvt, wsmall, bf_pad, tril_full, s // TM)
        if l == 0:
            stacks = (w_gate, w_pa, w_pb, w_pc, w_o, ffn_w1, ffn_w3)
            side = [a.reshape(-1, a.shape[-1]) for a in stacks] + [ffn_w2.reshape(-1, ffn_w2.shape[1])]
            o_a, cast = _attention(qp, kp, vt, b, s, side)
            wg_b, wpa_b, wpb_b, wpc_b, wo_b, f1_b, f3_b = (c.reshape(a.shape) for c, a in zip(cast, stacks))
            f2_b = cast[-1].reshape(ffn_w2.shape)
        else:
            o_a, _ = _attention(qp, kp, vt, b, s)

        convw8 = jnp.pad(conv_w[l], ((0, 8 - conv_w.shape[1]), (0, 0)))
        wa2p = jnp.zeros((LANES, C_KEY_WIDTH), F32).at[A_HEADS:A_HEADS + C_LOWRANK].set(w_a2[l]).astype(BF16)
        out_c = _gla(gla_in, small, wa2p, b_a[l].reshape(1, -1), g_gla[l].reshape(1, -1), tril_chunk, b, s)

        x2 = _merge(x2, gm, o_a, conv_in, convw8, out_c, wg_b, b_gate[l].reshape(1, -1),
                    wpa_b, wpb_b, wpc_b, wo_b, l, s // TM)

        i = l // 2
        if l % 2 == 0:
            prep = (moe_w1[i], moe_w3[i], moe_w2[i], _sorted_rows(t)) if l + 1 < depth else None
            x2, moe_ready = _dense_ffn(x2, g_ffn[l].reshape(1, d), f1_b, f3_b, f2_b, i, prep)
        else:
            done_final = l == depth - 1
            x2 = _moe_layer(x2, g_ffn[l], w_router[i], *moe_ready, g_final, done_final)
    if not done_final:
        x2 = _final_norm(x2, g_final.reshape(1, d))
    return x2.reshape(b, s, d)
```

```python
import functools
import math

import numpy as np
import jax
import jax.numpy as jnp
from jax import lax
from jax.experimental import pallas as pl
from jax.experimental.pallas import tpu as pltpu

F32 = jnp.float32
BF16 = jnp.bfloat16
I32 = jnp.int32

EPS = 1e-6
NEG = -1e30
LOG2E = math.log2(math.e)

A_HEADS, A_HEAD_DIM = 8, 64
A_WIDTH = A_HEADS * A_HEAD_DIM
B_WIDTH = 512
C_HEADS, C_KEY_DIM, C_VAL_DIM = 4, 64, 128
C_KEY_WIDTH = C_HEADS * C_KEY_DIM
C_VAL_WIDTH = C_HEADS * C_VAL_DIM
C_LOWRANK = 16
C_GATE_TEMP = 16.0
GLA_CHUNK = 64
N_EXPERTS = 8
GROUP_W = 1536

LANES = 128
V7X_VMEM_BYTES = 64 * 1024 * 1024
VMEM_LIMIT = V7X_VMEM_BYTES - 8 * 1024 * 1024

TM = 512
TQ = 512
ATT_HEADS = 8
ACC_ROWS = A_HEAD_DIM + 16
ATT_GUARD = 64.0
TL = 256
MIX_BATCH = 2
FF_CHUNK = 256
TMM = 512
ROW_GRAN = 16
SORT_ROWS = 2 * TM + N_EXPERTS * ROW_GRAN


def _cparams(sem):
    return pltpu.CompilerParams(dimension_semantics=sem, vmem_limit_bytes=VMEM_LIMIT)


def _dot(a, b):
    return jnp.dot(a, b, preferred_element_type=F32)


def _dot_nt(a, b):
    return lax.dot_general(a, b, (((1,), (1,)), ((), ())), preferred_element_type=F32)


def _rms(x, g):
    return x * lax.rsqrt(jnp.mean(x * x, axis=-1, keepdims=True) + EPS) * g


def _log_sigmoid(z):
    return jnp.minimum(z, 0.0) - jnp.log(1.0 + jnp.exp(-jnp.abs(z)))


def _sigmoid(z):
    return 1.0 / (1.0 + jnp.exp(-z))


def _split3(x):
    hi = x.astype(BF16)
    r1 = x - hi.astype(F32)
    mid = r1.astype(BF16)
    lo = (r1 - mid.astype(F32)).astype(BF16)
    return hi, mid, lo


def _const_spec(shape):
    return pl.BlockSpec(shape, lambda *_: (0,) * len(shape))


def _inproj_kernel(x_ref, g_ref, wbig_ref, wvt_ref, wsmall_ref, bf_ref, tril_ref, place_ref,
                   qp_ref, kp_ref, vt_ref, conv_ref, gla_ref, small_ref, carry_ref, *, tiles_per_seq):
    @pl.when(pl.program_id(0) % tiles_per_seq == 0)
    def _():
        carry_ref[...] = jnp.zeros_like(carry_ref)

    h = _rms(x_ref[...], g_ref[...]).astype(BF16)
    small = _dot(h, wsmall_ref[...])
    small_ref[...] = small
    q_all = _dot(h, wbig_ref[:, 0:A_WIDTH])
    k_all = _dot(h, wbig_ref[:, A_WIDTH:2 * A_WIDTH])
    off = 2 * A_WIDTH
    for ref in (conv_ref, gla_ref):
        width = ref.shape[1]
        for c in range(0, width, 512):
            ref[:, c:c + 512] = _dot(h, wbig_ref[:, off + c: off + c + 512]).astype(BF16)
        off += width
    vt_ref[0] = _dot_nt(wvt_ref[...], h).astype(BF16)

    lf = LOG2E * _log_sigmoid(small + bf_ref[...])
    hi, mid, lo = _split3(lf)
    tril = tril_ref[...]
    cs = _dot(tril, hi) + _dot(tril, mid) + _dot(tril, lo) + carry_ref[...]
    carry_ref[...] = cs[TM - 1:TM, :]
    fterms = _dot(jnp.concatenate(_split3(cs), axis=1), place_ref[...])

    lane = lax.broadcasted_iota(I32, (1, LANES), 1)
    is_x = lane < A_HEAD_DIM
    is_a = (lane >= A_HEAD_DIM) & (lane < A_HEAD_DIM + 3)
    is_b = (lane >= A_HEAD_DIM + 3) & (lane < A_HEAD_DIM + 6)
    for p in range(A_HEADS // 2):
        q2 = q_all[:, p * LANES:(p + 1) * LANES].astype(BF16).astype(F32)
        k2 = k_all[:, p * LANES:(p + 1) * LANES].astype(BF16).astype(F32)
        halves = ((q2, k2), (pltpu.roll(q2, A_HEAD_DIM, 1), pltpu.roll(k2, A_HEAD_DIM, 1)))
        for j in range(2):
            hd = 2 * p + j
            fh = pltpu.roll(fterms, (A_HEAD_DIM - 8 * hd) % LANES, 1)
            qh, kh = halves[j]
            qp_ref[:, hd * LANES:(hd + 1) * LANES] = jnp.where(
                is_x, qh, jnp.where(is_a, fh, jnp.where(is_b, 1.0, 0.0))).astype(BF16)
            kp_ref[:, hd * LANES:(hd + 1) * LANES] = jnp.where(
                is_x, kh, jnp.where(is_a, 1.0, jnp.where(is_b, -fh, 0.0))).astype(BF16)


def _aug_placement():
    place = np.zeros((3, LANES, LANES), np.float32)
    for h in range(A_HEADS):
        for c in range(3):
            place[c, h, 8 * h + c] = 1.0
            place[c, h, 8 * h + 3 + c] = 1.0
    return jnp.asarray(place.reshape(3 * LANES, LANES), BF16)


def _inproj(x2, g, wbig, wvt, wsmall, bf_pad, tril, tiles_per_seq):
    t, d = x2.shape
    place = _aug_placement()
    row = lambda w: pl.BlockSpec((TM, w), lambda i: (i, 0))
    return pl.pallas_call(
        functools.partial(_inproj_kernel, tiles_per_seq=tiles_per_seq),
        grid=(t // TM,),
        in_specs=[row(d), _const_spec((1, d)), _const_spec(wbig.shape), _const_spec(wvt.shape),
                  _const_spec(wsmall.shape), _const_spec((1, LANES)), _const_spec((TM, TM)),
                  _const_spec(place.shape)],
        out_specs=[row(A_HEADS * LANES), row(A_HEADS * LANES),
                   pl.BlockSpec((1, A_WIDTH, TM), lambda i: (i, 0, 0)), row(GROUP_W), row(GROUP_W), row(LANES)],
        out_shape=[jax.ShapeDtypeStruct((t, A_HEADS * LANES), BF16), jax.ShapeDtypeStruct((t, A_HEADS * LANES), BF16),
                   jax.ShapeDtypeStruct((t // TM, A_WIDTH, TM), BF16),
                   jax.ShapeDtypeStruct((t, GROUP_W), BF16), jax.ShapeDtypeStruct((t, GROUP_W), BF16),
                   jax.ShapeDtypeStruct((t, LANES), F32)],
        scratch_shapes=[pltpu.VMEM((1, LANES), F32)],
        compiler_params=_cparams(("arbitrary",)),
        name="inproj",
    )(x2, g, wbig, wvt, wsmall, bf_pad, tril, place)


def _attn_kernel(q_ref, k_ref, vt_ref, *rest, n_side):
    side_in, o_ref, side_out = rest[:n_side], rest[n_side], rest[n_side + 1:2 * n_side + 1]
    m_ref, acc_ref, exc_ref = rest[2 * n_side + 1:]
    for src_ref, dst_ref in zip(side_in, side_out):
        dst_ref[...] = src_ref[...].astype(BF16)
    qi = pl.program_id(2)
    ones = jnp.ones((ACC_ROWS - A_HEAD_DIM, TQ), BF16)

    def _score(ki, j):
        return _dot_nt(k_ref[ki, :, j * LANES:(j + 1) * LANES], q_ref[:, j * LANES:(j + 1) * LANES])

    def _values(ki, j):
        return jnp.concatenate([vt_ref[ki, j * A_HEAD_DIM:(j + 1) * A_HEAD_DIM, :], ones], axis=0)

    def _diag_step(ki):
        key = lax.broadcasted_iota(I32, (TQ, TQ), 0)
        qry = lax.broadcasted_iota(I32, (TQ, TQ), 1)
        keep = key <= qry
        scores = [_score(ki, j) for j in range(ATT_HEADS)]
        for j in range(ATT_HEADS):
            st = jnp.where(keep, scores[j], NEG)
            m_new = jnp.max(st, axis=0, keepdims=True)
            acc_ref[j] = _dot(_values(ki, j), jnp.exp2(st - m_new).astype(BF16))
            m_ref[j] = m_new

    def _exact_step(ki):
        scores = [_score(ki, j) for j in range(ATT_HEADS)]
        for j in range(ATT_HEADS):
            st = scores[j]
            m_prev = m_ref[j]
            m_new = jnp.maximum(m_prev, jnp.max(st, axis=0, keepdims=True))
            pr = jnp.exp2(st - m_new).astype(BF16)
            acc_ref[j] = jnp.exp2(m_prev - m_new) * acc_ref[j] + _dot(_values(ki, j), pr)
            m_ref[j] = m_new

    def _fast_step(ki, carry):
        excess = exc_ref[...]
        pending = _score(ki, 0)
        for j in range(ATT_HEADS):
            st = pending
            if j + 1 < ATT_HEADS:
                pending = _score(ki, j + 1)
            m_prev = m_ref[j]
            pr = jnp.exp2(st - m_prev).astype(BF16)
            tmax = jnp.max(st, axis=0, keepdims=True)
            excess = jnp.maximum(excess, tmax - m_prev)
            m_new = jnp.maximum(m_prev, tmax)
            acc_ref[j] = (acc_ref[j] + _dot(_values(ki, j), pr)) * jnp.exp2(m_prev - m_new)
            m_ref[j] = m_new
        exc_ref[...] = excess
        return carry

    def _exact_below(ki, carry):
        _exact_step(ki)
        return carry

    exc_ref[...] = jnp.zeros_like(exc_ref)
    _diag_step(qi)
    lax.fori_loop(0, qi, _fast_step, 0)

    @pl.when(jnp.max(exc_ref[...]) > ATT_GUARD)
    def _():
        _diag_step(qi)
        lax.fori_loop(0, qi, _exact_below, 0)

    ot = jnp.concatenate([acc_ref[j, 0:A_HEAD_DIM, :] / acc_ref[j, A_HEAD_DIM:A_HEAD_DIM + 1, :]
                          for j in range(ATT_HEADS)], axis=0)
    o_ref[...] = ot.T.astype(BF16)


def _attention(qp, kp, vt, b, s, side=()):
    nq = s // TQ
    ng = A_HEADS // ATT_HEADS
    steps = b * ng * nq
    kp3 = kp.reshape(b * nq, TQ, A_HEADS * LANES)
    assert all(a.shape[0] % (steps * ROW_GRAN) == 0 for a in side)
    slab = lambda a: pl.BlockSpec((a.shape[0] // steps, a.shape[1]), lambda bi, p, qi: ((bi * ng + p) * nq + qi, 0))
    outs = pl.pallas_call(
        functools.partial(_attn_kernel, n_side=len(side)),
        grid=(b, ng, nq),
        in_specs=[pl.BlockSpec((TQ, ATT_HEADS * LANES), lambda bi, p, qi: (bi * nq + qi, p)),
                  pl.BlockSpec((nq, TQ, ATT_HEADS * LANES), lambda bi, p, qi: (bi, 0, p)),
                  pl.BlockSpec((nq, ATT_HEADS * A_HEAD_DIM, TQ), lambda bi, p, qi: (bi, p, 0))]
        + [slab(a) for a in side],
        out_specs=[pl.BlockSpec((TQ, ATT_HEADS * A_HEAD_DIM), lambda bi, p, qi: (bi * nq + qi, p))]
        + [slab(a) for a in side],
        out_shape=[jax.ShapeDtypeStruct((b * s, A_WIDTH), BF16)]
        + [jax.ShapeDtypeStruct(a.shape, BF16) for a in side],
        scratch_shapes=[pltpu.VMEM((ATT_HEADS, 1, TQ), F32), pltpu.VMEM((ATT_HEADS, ACC_ROWS, TQ), F32),
                        pltpu.VMEM((1, TQ), F32)],
        compiler_params=_cparams(("parallel", "parallel", "arbitrary")),
        name="fox_attention",
    )(qp, kp3, vt, *side)
    return outs[0], outs[1:]


def _gla_kernel(gla_ref, small_ref, wa2_ref, ba_ref, ggla_ref, tri_ref, oc_ref, st_ref, sall_ref):
    @pl.when(pl.program_id(1) == 0)
    def _():
        st_ref[...] = jnp.zeros_like(st_ref)

    nc = TL // GLA_CHUNK
    nb = MIX_BATCH

    tri = tri_ref[...]
    q_t, k_t, ks_t, dec_t = [], [], [], []
    for i in range(nb):
        la = _dot(small_ref[i].astype(BF16), wa2_ref[...]) + ba_ref[...]
        la = _log_sigmoid(la) * (1.0 / C_GATE_TEMP)
        hi, mid, lo = _split3(la)
        bcum = _dot(tri, hi) + _dot(tri, mid) + _dot(tri, lo)
        bl = bcum.reshape(nc, GLA_CHUNK, C_KEY_WIDTH)[:, GLA_CHUNK - 1:GLA_CHUNK, :]
        blast = jnp.broadcast_to(bl, (nc, GLA_CHUNK, C_KEY_WIDTH)).reshape(TL, C_KEY_WIDTH)
        q = gla_ref[i, :, 0:C_KEY_WIDTH].astype(F32)
        k = gla_ref[i, :, C_KEY_WIDTH:2 * C_KEY_WIDTH].astype(F32)
        q_t.append(q * (C_KEY_DIM ** -0.5) * jnp.exp(bcum))
        k_t.append((k * jnp.exp(-bcum)).astype(BF16))
        ks_t.append((k * jnp.exp(blast - bcum)).T)
        dec_t.append(jnp.broadcast_to(jnp.exp(bl), (nc, LANES, C_KEY_WIDTH)).reshape(nc * LANES, C_KEY_WIDTH).T)

    r_i = lax.broadcasted_iota(I32, (TL, TL), 0)
    c_i = lax.broadcasted_iota(I32, (TL, TL), 1)
    same = (r_i // GLA_CHUNK) == (c_i // GLA_CHUNK)
    intra = same & (c_i <= r_i)
    lane_k = lax.broadcasted_iota(I32, (1, C_KEY_WIDTH), 1)
    lane = lax.broadcasted_iota(I32, (1, LANES), 1)

    for h in range(C_HEADS):
        for i in range(nb):
            v_h = gla_ref[i, :, 2 * C_KEY_WIDTH + h * C_VAL_DIM: 2 * C_KEY_WIDTH + (h + 1) * C_VAL_DIM]
            q_h = jnp.where(lane_k // C_KEY_DIM == h, q_t[i], 0.0).astype(BF16)
            a = jnp.where(intra, _dot_nt(q_h, k_t[i]), 0.0).astype(BF16)

            q2 = q_t[i][:, (h // 2) * LANES:(h // 2 + 1) * LANES]
            q2r = pltpu.roll(q2, C_KEY_DIM, 1)
            dup = jnp.where((lane < C_KEY_DIM) == (h % 2 == 0), q2, q2r)
            q_exp = jnp.where(same, jnp.concatenate([dup] * (TL // LANES), axis=1), 0.0).astype(BF16)

            ks_h = ks_t[i][h * C_KEY_DIM:(h + 1) * C_KEY_DIM, :]
            k_exp = jnp.where(same, jnp.concatenate([ks_h] * nc, axis=0), 0.0).astype(BF16)
            kv = _dot(k_exp, v_h)

            st = st_ref[i, h]
            for c in range(nc):
                sall_ref[i, h, c * GLA_CHUNK:(c + 1) * GLA_CHUNK, :] = st.astype(BF16)
                dec = dec_t[i][h * C_KEY_DIM:(h + 1) * C_KEY_DIM, c * LANES:(c + 1) * LANES]
                st = dec * st + kv[c * GLA_CHUNK:(c + 1) * GLA_CHUNK, :]
            st_ref[i, h] = st

            o = _dot(a, v_h) + _dot(q_exp, sall_ref[i, h])
            o = o * lax.rsqrt(jnp.mean(o * o, axis=-1, keepdims=True) + EPS)
            o = o * ggla_ref[:, h * C_VAL_DIM:(h + 1) * C_VAL_DIM]
            r = gla_ref[i, :, 4 * C_KEY_WIDTH + h * C_VAL_DIM: 4 * C_KEY_WIDTH + (h + 1) * C_VAL_DIM].astype(F32)
            oc_ref[i, :, h * C_VAL_DIM:(h + 1) * C_VAL_DIM] = (o * (r * _sigmoid(r))).astype(BF16)


def _gla(gla_in, small, wa2p, ba, ggla, tri_bd, b, s):
    ns = s // TL
    blk = lambda w: pl.BlockSpec((MIX_BATCH, TL, w), lambda bi, si: (bi, si, 0))
    oc = pl.pallas_call(
        _gla_kernel,
        grid=(b // MIX_BATCH, ns),
        in_specs=[blk(GROUP_W), blk(LANES),
                  _const_spec((LANES, C_KEY_WIDTH)),
                  _const_spec((1, C_KEY_WIDTH)),
                  _const_spec((1, C_VAL_WIDTH)),
                  _const_spec((TL, TL))],
        out_specs=blk(C_VAL_WIDTH),
        out_shape=jax.ShapeDtypeStruct((b, s, C_VAL_WIDTH), BF16),
        scratch_shapes=[pltpu.VMEM((MIX_BATCH, C_HEADS, C_KEY_DIM, C_VAL_DIM), F32),
                        pltpu.VMEM((MIX_BATCH, C_HEADS, TL, C_VAL_DIM), BF16)],
        compiler_params=_cparams(("parallel", "arbitrary")),
        name="gla",
    )(gla_in.reshape(b, s, GROUP_W), small.reshape(b, s, LANES), wa2p, ba, ggla, tri_bd)
    return oc.reshape(b * s, C_VAL_WIDTH)


def _merge_kernel(x_ref, g_ref, oa_ref, conv_ref, convw_ref, oc_ref, wg_ref, bgate_ref, wpa_ref, wpb_ref,
                  wpc_ref, wo_ref, out_ref, zprev_ref, *, tiles_per_seq):
    @pl.when(pl.program_id(0) % tiles_per_seq == 0)
    def _():
        zprev_ref[...] = jnp.zeros_like(zprev_ref)

    u = conv_ref[:, 0:B_WIDTH].astype(F32)
    bg = conv_ref[:, B_WIDTH:2 * B_WIDTH].astype(F32)
    cg = conv_ref[:, 2 * B_WIDTH:3 * B_WIDTH].astype(F32)
    z = cg * u
    row = lax.broadcasted_iota(I32, (TM, B_WIDTH), 0)
    zp = zprev_ref[...]
    z1 = jnp.where(row == 0, zp[7:8, :], pltpu.roll(z, 1, 0))
    z2 = jnp.where(row == 0, zp[6:7, :], jnp.where(row == 1, zp[7:8, :], pltpu.roll(z, 2, 0)))
    cw = convw_ref[...]
    o_b = (bg * (cw[0:1, :] * z2 + cw[1:2, :] * z1 + cw[2:3, :] * z)).astype(BF16)
    zprev_ref[...] = z[TM - 8:TM, :]

    x = x_ref[...]
    d = x.shape[1]
    h = _rms(x, g_ref[...]).astype(BF16)
    merged = None
    for j, (o, wp_ref) in enumerate(((oa_ref[...], wpa_ref), (o_b, wpb_ref), (oc_ref[...], wpc_ref))):
        gate = _sigmoid(_dot(h, wg_ref[0, :, j * d:(j + 1) * d]) + bgate_ref[:, j * d:(j + 1) * d])
        term = gate * _dot(o, wp_ref[0])
        merged = term if merged is None else merged + term
    out_ref[...] = x + _dot(merged.astype(BF16), wo_ref[0])


def _layer_spec(stacked, l):
    return pl.BlockSpec((1,) + stacked.shape[1:], lambda *_: (l, 0, 0))


def _merge(x2, g, oa, conv_in, convw8, oc, wg, bgate, wpa, wpb, wpc, wo, l, tiles_per_seq):
    t, d = x2.shape
    row = lambda w: pl.BlockSpec((TM, w), lambda i: (i, 0))
    return pl.pallas_call(
        functools.partial(_merge_kernel, tiles_per_seq=tiles_per_seq),
        grid=(t // TM,),
        in_specs=[row(d), _const_spec((1, d)), row(A_WIDTH), row(GROUP_W), _const_spec((8, B_WIDTH)),
                  row(C_VAL_WIDTH), _layer_spec(wg, l), _const_spec(bgate.shape), _layer_spec(wpa, l),
                  _layer_spec(wpb, l), _layer_spec(wpc, l), _layer_spec(wo, l)],
        out_specs=row(d),
        out_shape=jax.ShapeDtypeStruct((t, d), F32),
        scratch_shapes=[pltpu.VMEM((8, B_WIDTH), F32)],
        compiler_params=_cparams(("arbitrary",)),
        name="merge_outproj",
    )(x2, g, oa, conv_in, convw8, oc, wg, bgate, wpa, wpb, wpc, wo)


def _swiglu_tile(h, w1_ref, w3_ref, w2_ref):
    ff = w1_ref.shape[-1]
    acc = None
    for c in range(0, ff, FF_CHUNK):
        a = _dot(h, w1_ref[:, c:c + FF_CHUNK])
        b = _dot(h, w3_ref[:, c:c + FF_CHUNK])
        g = (a * _sigmoid(a) * b).astype(BF16)
        part = _dot(g, w2_ref[c:c + FF_CHUNK, :])
        acc = part if acc is None else acc + part
    return acc


def _ffn_kernel(x_ref, g_ref, w1_ref, w3_ref, w2_ref, *rest):
    if len(rest) == 1:
        (out_ref,) = rest
    else:
        m1_ref, m3_ref, m2_ref, out_ref, c1_ref, c3_ref, c2_ref, zero_ref = rest
        c1_ref[...] = m1_ref[...].astype(BF16)
        c3_ref[...] = m3_ref[...].astype(BF16)
        c2_ref[...] = m2_ref[...].astype(BF16)
        zero_ref[...] = jnp.zeros_like(zero_ref)
    x = x_ref[...]
    h = _rms(x, g_ref[...]).astype(BF16)
    out_ref[...] = x + _swiglu_tile(h, w1_ref.at[0], w3_ref.at[0], w2_ref.at[0])


def _dense_ffn(x2, g, w1, w3, w2, i, moe_prep=None):
    t, d = x2.shape
    steps = t // TM
    row = pl.BlockSpec((TM, d), lambda i: (i, 0))
    in_specs = [row, _const_spec((1, d)), _layer_spec(w1, i), _layer_spec(w3, i), _layer_spec(w2, i)]
    out_specs = [row]
    out_shape = [jax.ShapeDtypeStruct((t, d), F32)]
    args = [x2, g, w1, w3, w2]
    if moe_prep is not None:
        m1, m3, m2, sorted_rows = moe_prep
        flat = [m1.reshape(-1, m1.shape[-1]), m3.reshape(-1, m3.shape[-1]), m2.reshape(-1, m2.shape[-1])]
        shapes = [a.shape for a in flat] + [(sorted_rows, d)]
        assert all(r % (steps * ROW_GRAN) == 0 for r, _ in shapes)
        slab = lambda shp: pl.BlockSpec((shp[0] // steps, shp[1]), lambda i: (i, 0))
        in_specs += [slab(s) for s in shapes[:3]]
        out_specs += [slab(s) for s in shapes]
        out_shape += [jax.ShapeDtypeStruct(s, BF16) for s in shapes]
        args += flat
    outs = pl.pallas_call(
        _ffn_kernel,
        grid=(steps,),
        in_specs=in_specs,
        out_specs=out_specs,
        out_shape=out_shape,
        compiler_params=_cparams(("parallel",)),
        name="dense_ffn",
    )(*args)
    if moe_prep is None:
        return outs[0], None
    m1, m3, m2, _ = moe_prep
    return outs[0], (outs[1].reshape(m1.shape), outs[2].reshape(m3.shape), outs[3].reshape(m2.shape), outs[4])


def _router_kernel(x_ref, g_ref, wr_ref, stril_ref, h_ref, rt_ref, seg_ref):
    h = _rms(x_ref[...], g_ref[...])
    h_hi = h.astype(BF16)
    h_ref[...] = h_hi
    h_lo = (h - h_hi.astype(F32)).astype(BF16)
    wr = wr_ref[...]
    w_hi = wr.astype(BF16)
    w_lo = (wr - w_hi.astype(F32)).astype(BF16)
    both = _dot(h_hi, jnp.concatenate([w_hi, w_lo], axis=1))
    logits = both[:, 0:LANES] + both[:, LANES:2 * LANES] + _dot(h_lo, w_hi)

    lane = lax.broadcasted_iota(I32, (TM, LANES), 1)
    lane_f = lane.astype(F32)
    lg = jnp.where(lane < N_EXPERTS, logits, NEG)
    m1 = jnp.max(lg, axis=-1, keepdims=True)
    i1 = jnp.min(jnp.where(lg == m1, lane_f, float(LANES)), axis=-1, keepdims=True)
    oh1 = lane_f == i1
    lg2 = jnp.where(oh1, NEG, lg)
    m2 = jnp.max(lg2, axis=-1, keepdims=True)
    i2 = jnp.min(jnp.where(lg2 == m2, lane_f, float(LANES)), axis=-1, keepdims=True)
    oh2 = lane_f == i2
    e = jnp.exp(m2 - m1)
    w1 = 1.0 / (1.0 + e)
    w2 = e / (1.0 + e)

    sel = jnp.where(oh1 | oh2, 1.0, 0.0)
    before = _dot(stril_ref[...], sel.astype(BF16))
    cnt = jnp.sum(sel, axis=0, keepdims=True)
    pc = jnp.floor((cnt + (ROW_GRAN - 1)) * (1.0 / ROW_GRAN)) * ROW_GRAN
    lane1 = lax.broadcasted_iota(I32, (1, LANES), 1)
    incl = pc
    for sh in (1, 2, 4):
        incl = incl + jnp.where(lane1 >= sh, pltpu.roll(incl, sh, 1), 0.0)
    off = incl - pc
    pos = before + off
    d1 = jnp.sum(jnp.where(oh1, pos, 0.0), axis=-1, keepdims=True)
    d2 = jnp.sum(jnp.where(oh2, pos, 0.0), axis=-1, keepdims=True)
    rt_ref[...] = jnp.where(lane == 0, d1, jnp.where(lane == 1, d2, jnp.where(lane == 2, w1, jnp.where(lane == 3, w2, 0.0))))
    row8 = lax.broadcasted_iota(I32, (8, LANES), 0)
    seg_ref[0] = jnp.where(row8 == 0, pc, jnp.where(row8 == 1, off, 0.0))


def _router(x2, g, wr_pad, stril):
    t, d = x2.shape
    nb = t // TM
    row = lambda w: pl.BlockSpec((TM, w), lambda i: (i, 0))
    return pl.pallas_call(
        _router_kernel,
        grid=(nb,),
        in_specs=[row(d), _const_spec((1, d)), _const_spec(wr_pad.shape), _const_spec((TM, TM))],
        out_specs=[row(d), row(LANES), pl.BlockSpec((1, 8, LANES), lambda i: (i, 0, 0))],
        out_shape=[jax.ShapeDtypeStruct((t, d), BF16), jax.ShapeDtypeStruct((t, LANES), F32),
                   jax.ShapeDtypeStruct((nb, 8, LANES), F32)],
        compiler_params=_cparams(("parallel",)),
        name="moe_router",
    )(x2, g, wr_pad, stril)


def _chunk_copy(src_ref, src_row, dst_ref, dst_row, sem):
    return pltpu.make_async_copy(src_ref.at[pl.ds(pl.multiple_of(src_row, ROW_GRAN), ROW_GRAN)],
                                 dst_ref.at[pl.ds(pl.multiple_of(dst_row, ROW_GRAN), ROW_GRAN)], sem)


def _segment_starts(b, gstart_ref, nch_ref, off_ref, hbm_ref, buf_ref, sem, to_hbm):
    for e in range(N_EXPERTS):
        n = nch_ref[b * N_EXPERTS + e]
        g0 = gstart_ref[b * N_EXPERTS + e]
        o0 = off_ref[b * N_EXPERTS + e]

        def start(c, carry, g0=g0, o0=o0):
            if to_hbm:
                _chunk_copy(buf_ref, o0 + c * ROW_GRAN, hbm_ref, g0 + c * ROW_GRAN, sem).start()
            else:
                _chunk_copy(hbm_ref, g0 + c * ROW_GRAN, buf_ref, o0 + c * ROW_GRAN, sem).start()
            return carry

        lax.fori_loop(0, n, start, 0)


def _segment_waits(b, nch_ref, buf_ref, sem):
    total = 0
    for e in range(N_EXPERTS):
        total = total + nch_ref[b * N_EXPERTS + e]

    def wait(c, carry):
        _chunk_copy(buf_ref, 0, buf_ref, 0, sem).wait()
        return carry

    lax.fori_loop(0, total, wait, 0)


def _dispatch_kernel(gstart_ref, nch_ref, off_ref, h_ref, rt_ref, xs_in_ref, xs_ref, buf_ref, sem):
    del xs_in_ref
    b = pl.program_id(0)
    slot = b % 2
    dest = rt_ref[...].T
    r = lax.broadcasted_iota(I32, (SORT_ROWS, TM), 0).astype(F32)
    perm = jnp.where((r == dest[0:1, :]) | (r == dest[1:2, :]), 1.0, 0.0).astype(BF16)
    buf_ref[slot] = _dot(perm, h_ref[...]).astype(BF16)
    _segment_starts(b, gstart_ref, nch_ref, off_ref, xs_ref, buf_ref.at[slot], sem.at[slot], to_hbm=True)

    @pl.when(b > 0)
    def _():
        _segment_waits(b - 1, nch_ref, buf_ref.at[1 - slot], sem.at[1 - slot])

    @pl.when(b == pl.num_programs(0) - 1)
    def _():
        _segment_waits(b, nch_ref, buf_ref.at[slot], sem.at[slot])


def _dispatch(gstart, nch, off, h2, rt, xs_zero):
    t, d = h2.shape
    grid_spec = pltpu.PrefetchScalarGridSpec(
        num_scalar_prefetch=3,
        grid=(t // TM,),
        in_specs=[pl.BlockSpec((TM, d), lambda i, *_: (i, 0)),
                  pl.BlockSpec((TM, LANES), lambda i, *_: (i, 0)),
                  pl.BlockSpec(memory_space=pl.ANY)],
        out_specs=pl.BlockSpec(memory_space=pl.ANY),
        scratch_shapes=[pltpu.VMEM((2, SORT_ROWS, d), BF16), pltpu.SemaphoreType.DMA((2,))],
    )
    return pl.pallas_call(
        _dispatch_kernel,
        grid_spec=grid_spec,
        out_shape=jax.ShapeDtypeStruct(xs_zero.shape, xs_zero.dtype),
        input_output_aliases={5: 0},
        compiler_params=_cparams(("arbitrary",)),
        name="moe_dispatch",
    )(gstart, nch, off, h2, rt, xs_zero)


def _group_ffn_kernel(te_ref, ta_ref, ts_ref, xs_ref, w1_ref, w3_ref, w2_ref, y_ref):
    del te_ref, ts_ref
    active = ta_ref[pl.program_id(0)] == 1

    @pl.when(active)
    def _():
        y_ref[...] = _swiglu_tile(xs_ref[...], w1_ref.at[0], w3_ref.at[0], w2_ref.at[0]).astype(BF16)

    @pl.when(jnp.logical_not(active))
    def _():
        y_ref[...] = jnp.zeros_like(y_ref)


def _group_ffn(tile_expert, tile_active, tile_src, xs, w1, w3, w2):
    ns, d = xs.shape
    ff = w1.shape[-1]
    grid_spec = pltpu.PrefetchScalarGridSpec(
        num_scalar_prefetch=3,
        grid=(ns // TMM,),
        in_specs=[pl.BlockSpec((TMM, d), lambda i, te, ta, ts: (ts[i], 0)),
                  pl.BlockSpec((1, d, ff), lambda i, te, ta, ts: (te[i], 0, 0)),
                  pl.BlockSpec((1, d, ff), lambda i, te, ta, ts: (te[i], 0, 0)),
                  pl.BlockSpec((1, ff, d), lambda i, te, ta, ts: (te[i], 0, 0))],
        out_specs=pl.BlockSpec((TMM, d), lambda i, te, ta, ts: (i, 0)),
    )
    return pl.pallas_call(
        _group_ffn_kernel,
        grid_spec=grid_spec,
        out_shape=jax.ShapeDtypeStruct((ns, d), BF16),
        compiler_params=_cparams(("arbitrary",)),
        name="moe_group_ffn",
    )(tile_expert, tile_active, tile_src, xs, w1, w3, w2)


def _combine_kernel(gstart_ref, nch_ref, off_ref, x_ref, rt_ref, gf_ref, y_ref, out_ref, buf_ref, sem,
                    *, final_norm):
    b = pl.program_id(0)
    slot = b % 2

    def fetch(blk, s):
        buf_ref[s] = jnp.zeros(buf_ref.shape[1:], BF16)
        _segment_starts(blk, gstart_ref, nch_ref, off_ref, y_ref, buf_ref.at[s], sem.at[s], to_hbm=False)

    @pl.when(b == 0)
    def _():
        fetch(b, slot)

    @pl.when(b + 1 < pl.num_programs(0))
    def _():
        fetch(b + 1, 1 - slot)

    _segment_waits(b, nch_ref, buf_ref.at[slot], sem.at[slot])
    rt = rt_ref[...]
    r = lax.broadcasted_iota(I32, (TM, SORT_ROWS), 1).astype(F32)
    ybuf = buf_ref[slot]
    y1 = _dot(jnp.where(r == rt[:, 0:1], 1.0, 0.0).astype(BF16), ybuf)
    y2 = _dot(jnp.where(r == rt[:, 1:2], 1.0, 0.0).astype(BF16), ybuf)
    x = x_ref[...] + rt[:, 2:3] * y1 + rt[:, 3:4] * y2
    out_ref[...] = _rms(x, gf_ref[...]) if final_norm else x


def _combine(gstart, nch, off, x2, rt, g_final, y, final_norm):
    t, d = x2.shape
    grid_spec = pltpu.PrefetchScalarGridSpec(
        num_scalar_prefetch=3,
        grid=(t // TM,),
        in_specs=[pl.BlockSpec((TM, d), lambda i, *_: (i, 0)),
                  pl.BlockSpec((TM, LANES), lambda i, *_: (i, 0)),
                  pl.BlockSpec((1, d), lambda i, *_: (0, 0)),
                  pl.BlockSpec(memory_space=pl.ANY)],
        out_specs=pl.BlockSpec((TM, d), lambda i, *_: (i, 0)),
        scratch_shapes=[pltpu.VMEM((2, SORT_ROWS, d), BF16), pltpu.SemaphoreType.DMA((2,))],
    )
    return pl.pallas_call(
        functools.partial(_combine_kernel, final_norm=final_norm),
        grid_spec=grid_spec,
        out_shape=jax.ShapeDtypeStruct((t, d), F32),
        compiler_params=_cparams(("arbitrary",)),
        name="moe_combine",
    )(gstart, nch, off, x2, rt, g_final, y)


def _norm_kernel(x_ref, g_ref, out_ref):
    out_ref[...] = _rms(x_ref[...], g_ref[...])


def _final_norm(x2, g):
    t, d = x2.shape
    row = pl.BlockSpec((TM, d), lambda i: (i, 0))
    return pl.pallas_call(
        _norm_kernel, grid=(t // TM,), in_specs=[row, _const_spec((1, d))], out_specs=row,
        out_shape=jax.ShapeDtypeStruct((t, d), F32), compiler_params=_cparams(("parallel",)),
        name="final_norm",
    )(x2, g)


def _tril_blocks(n, block, strict=False):
    r = jnp.arange(n)[:, None]
    c = jnp.arange(n)[None, :]
    keep = (c < r) if strict else (c <= r)
    keep = keep & ((r // block) == (c // block))
    return keep.astype(BF16)


def _sorted_rows(t):
    return ((2 * t + (t // TM) * N_EXPERTS * ROW_GRAN) // TMM + N_EXPERTS) * TMM


def _moe_layer(x2, g_ffn, w_router, w1, w3, w2, xs_zero, g_final, final_norm):
    t, d = x2.shape
    nb = t // TM
    wr_pad = jnp.pad(w_router, ((0, 0), (0, LANES - N_EXPERTS)))
    h2, rt, seg = _router(x2, g_ffn.reshape(1, d), wr_pad, _tril_blocks(TM, TM, strict=True))

    pc = seg[:, 0, :N_EXPERTS].astype(I32)
    off = seg[:, 1, :N_EXPERTS].astype(I32)
    region = ((jnp.sum(pc, axis=0) + TMM - 1) // TMM) * TMM
    ends = jnp.cumsum(region)
    gstart = (ends - region)[None, :] + jnp.cumsum(pc, axis=0) - pc
    n_tiles = _sorted_rows(t) // TMM
    tile_start = jnp.arange(n_tiles, dtype=I32) * TMM
    tile_expert = jnp.minimum(jnp.sum(tile_start[:, None] >= ends[None, :], axis=1), N_EXPERTS - 1).astype(I32)
    tile_active = (tile_start < ends[-1]).astype(I32)
    last_active = jnp.maximum(ends[-1] // TMM - 1, 0)
    tile_src = jnp.minimum(jnp.arange(n_tiles, dtype=I32), last_active).astype(I32)
    seg_args = (gstart.reshape(-1).astype(I32), (pc // ROW_GRAN).reshape(-1), off.reshape(-1))

    xs = _dispatch(*seg_args, h2, rt, xs_zero)
    y = _group_ffn(tile_expert, tile_active, tile_src, xs, w1, w3, w2)
    return _combine(*seg_args, x2, rt, g_final.reshape(1, d), y, final_norm)


def kernel(x, g_mix, w_in, b_f, conv_w, w_a2, b_a, g_gla, w_pa, w_pb, w_pc, w_gate, b_gate, w_o,
           g_ffn, ffn_w1, ffn_w3, ffn_w2, w_router, moe_w1, moe_w3, moe_w2, g_final):
    b, s, d = x.shape
    depth = g_mix.shape[0]
    t = b * s
    assert t % TM == 0 and s % TQ == 0 and TQ == TM and s % TL == 0 and TL % GLA_CHUNK == 0 and b % MIX_BATCH == 0
    x2 = x.reshape(t, d)
    tril_full = _tril_blocks(TM, TM)
    tril_chunk = _tril_blocks(TL, GLA_CHUNK)
    o_af = 3 * A_WIDTH
    o_b = o_af + A_HEADS
    o_c = o_b + 3 * B_WIDTH
    o_ca = o_c + 2 * C_KEY_WIDTH + C_VAL_WIDTH
    o_cr = o_ca + C_LOWRANK

    done_final = False
    for l in range(depth):
        w = w_in[l]
        wq = w[:, 0:A_WIDTH] * (A_HEAD_DIM ** -0.5 * LOG2E)
        wbig = jnp.concatenate([wq, w[:, A_WIDTH:2 * A_WIDTH], w[:, o_b:o_c], w[:, o_c:o_ca], w[:, o_cr:]],
                               axis=1).astype(BF16)
        wvt = w[:, 2 * A_WIDTH:o_af].T.astype(BF16)
        wsmall = jnp.concatenate([w[:, o_af:o_b], w[:, o_ca:o_cr],
                                  jnp.zeros((d, LANES - A_HEADS - C_LOWRANK), F32)], axis=1).astype(BF16)
        gm = g_mix[l].reshape(1, d)
        bf_pad = jnp.pad(b_f[l], (0, LANES - A_HEADS)).reshape(1, LANES)
        qp, kp, vt, conv_in, gla_in, small = _inproj(x2, gm, wbig, wvt, wsmall, bf_pad, tril_full, s // TM)
        if l == 0:
            stacks = (w_gate, w_pa, w_pb, w_pc, w_o, ffn_w1, ffn_w3)
            o_a, cast = _attention(qp, kp, vt, b, s, [a.reshape(-1, a.shape[-1]) for a in stacks])
            wg_b, wpa_b, wpb_b, wpc_b, wo_b, f1_b, f3_b = (c.reshape(a.shape) for c, a in zip(cast, stacks))
            f2_b = ffn_w2.astype(BF16)
        else:
            o_a, _ = _attention(qp, kp, vt, b, s)

        convw8 = jnp.pad(conv_w[l], ((0, 8 - conv_w.shape[1]), (0, 0)))
        wa2p = jnp.zeros((LANES, C_KEY_WIDTH), F32).at[A_HEADS:A_HEADS + C_LOWRANK].set(w_a2[l]).astype(BF16)
        out_c = _gla(gla_in, small, wa2p, b_a[l].reshape(1, -1), g_gla[l].reshape(1, -1), tril_chunk, b, s)

        x2 = _merge(x2, gm, o_a, conv_in, convw8, out_c, wg_b, b_gate[l].reshape(1, -1),
                    wpa_b, wpb_b, wpc_b, wo_b, l, s // TM)

        i = l // 2
        if l % 2 == 0:
            prep = (moe_w1[i], moe_w3[i], moe_w2[i], _sorted_rows(t)) if l + 1 < depth else None
            x2, moe_ready = _dense_ffn(x2, g_ffn[l].reshape(1, d), f1_b, f3_b, f2_b, i, prep)
        else:
            done_final = l == depth - 1
            x2 = _moe_layer(x2, g_ffn[l], w_router[i], *moe_ready, g_final, done_final)
    if not done_final:
        x2 = _final_norm(x2, g_final.reshape(1, d))
    return x2.reshape(b, s, d)
```

```python
import functools
import math

import numpy as np
import jax
import jax.numpy as jnp
from jax import lax
from jax.experimental import pallas as pl
from jax.experimental.pallas import tpu as pltpu

F32 = jnp.float32
BF16 = jnp.bfloat16
I32 = jnp.int32

EPS = 1e-6
NEG = -1e30
LOG2E = math.log2(math.e)

A_HEADS, A_HEAD_DIM = 8, 64
A_WIDTH = A_HEADS * A_HEAD_DIM
B_WIDTH = 512
C_HEADS, C_KEY_DIM, C_VAL_DIM = 4, 64, 128
C_KEY_WIDTH = C_HEADS * C_KEY_DIM
C_VAL_WIDTH = C_HEADS * C_VAL_DIM
C_LOWRANK = 16
C_GATE_TEMP = 16.0
GLA_CHUNK = 64
N_EXPERTS = 8
GROUP_W = 1536

LANES = 128
V7X_VMEM_BYTES = 64 * 1024 * 1024
VMEM_LIMIT = V7X_VMEM_BYTES - 8 * 1024 * 1024

TM = 512
TQ = 512
ATT_HEADS = 8
ACC_ROWS = A_HEAD_DIM + 16
ATT_GUARD = 64.0
TL = 256
MIX_BATCH = 2
FF_CHUNK = 256
TMM = 512
ROW_GRAN = 16
SORT_ROWS = 2 * TM + N_EXPERTS * ROW_GRAN


def _cparams(sem):
    return pltpu.CompilerParams(dimension_semantics=sem, vmem_limit_bytes=VMEM_LIMIT)


def _dot(a, b):
    return jnp.dot(a, b, preferred_element_type=F32)


def _dot_nt(a, b):
    return lax.dot_general(a, b, (((1,), (1,)), ((), ())), preferred_element_type=F32)


def _rms(x, g):
    return x * lax.rsqrt(jnp.mean(x * x, axis=-1, keepdims=True) + EPS) * g


def _log_sigmoid(z):
    return jnp.minimum(z, 0.0) - jnp.log(1.0 + jnp.exp(-jnp.abs(z)))


def _sigmoid(z):
    return 1.0 / (1.0 + jnp.exp(-z))


def _split3(x):
    hi = x.astype(BF16)
    r1 = x - hi.astype(F32)
    mid = r1.astype(BF16)
    lo = (r1 - mid.astype(F32)).astype(BF16)
    return hi, mid, lo


def _const_spec(shape):
    return pl.BlockSpec(shape, lambda *_: (0,) * len(shape))


def _layer_spec(stacked, l):
    return pl.BlockSpec((1,) + stacked.shape[1:], lambda *_: (l, 0, 0))


def _inproj_kernel(x_ref, g_ref, wbig_ref, wvt_ref, wsmall_ref, bf_ref, tril_ref, place_ref,
                   qp_ref, kp_ref, vt_ref, conv_ref, gla_ref, small_ref, carry_ref, *, tiles_per_seq):
    @pl.when(pl.program_id(0) % tiles_per_seq == 0)
    def _():
        carry_ref[...] = jnp.zeros_like(carry_ref)

    h = _rms(x_ref[...], g_ref[...]).astype(BF16)
    small = _dot(h, wsmall_ref[0])
    small_ref[...] = small
    q_all = _dot(h, wbig_ref[0, :, 0:A_WIDTH])
    k_all = _dot(h, wbig_ref[0, :, A_WIDTH:2 * A_WIDTH])
    off = 2 * A_WIDTH
    for ref in (conv_ref, gla_ref):
        width = ref.shape[1]
        for c in range(0, width, 512):
            ref[:, c:c + 512] = _dot(h, wbig_ref[0, :, off + c: off + c + 512]).astype(BF16)
        off += width
    vt_ref[0] = _dot_nt(wvt_ref[0], h).astype(BF16)

    lf = LOG2E * _log_sigmoid(small + bf_ref[...])
    hi, mid, lo = _split3(lf)
    tril = tril_ref[...]
    cs = _dot(tril, hi) + _dot(tril, mid) + _dot(tril, lo) + carry_ref[...]
    carry_ref[...] = cs[TM - 1:TM, :]
    fterms = _dot(jnp.concatenate(_split3(cs), axis=1), place_ref[...])

    lane = lax.broadcasted_iota(I32, (1, LANES), 1)
    is_x = lane < A_HEAD_DIM
    is_a = (lane >= A_HEAD_DIM) & (lane < A_HEAD_DIM + 3)
    is_b = (lane >= A_HEAD_DIM + 3) & (lane < A_HEAD_DIM + 6)
    for p in range(A_HEADS // 2):
        q2 = q_all[:, p * LANES:(p + 1) * LANES].astype(BF16).astype(F32)
        k2 = k_all[:, p * LANES:(p + 1) * LANES].astype(BF16).astype(F32)
        halves = ((q2, k2), (pltpu.roll(q2, A_HEAD_DIM, 1), pltpu.roll(k2, A_HEAD_DIM, 1)))
        for j in range(2):
            hd = 2 * p + j
            fh = pltpu.roll(fterms, (A_HEAD_DIM - 8 * hd) % LANES, 1)
            qh, kh = halves[j]
            qp_ref[:, hd * LANES:(hd + 1) * LANES] = jnp.where(
                is_x, qh, jnp.where(is_a, fh, jnp.where(is_b, 1.0, 0.0))).astype(BF16)
            kp_ref[:, hd * LANES:(hd + 1) * LANES] = jnp.where(
                is_x, kh, jnp.where(is_a, 1.0, jnp.where(is_b, -fh, 0.0))).astype(BF16)


def _aug_placement():
    place = np.zeros((3, LANES, LANES), np.float32)
    for h in range(A_HEADS):
        for c in range(3):
            place[c, h, 8 * h + c] = 1.0
            place[c, h, 8 * h + 3 + c] = 1.0
    return jnp.asarray(place.reshape(3 * LANES, LANES), BF16)


def _inproj(x2, g, wbig, wvt, wsmall, l, bf_pad, tril, tiles_per_seq):
    t, d = x2.shape
    place = _aug_placement()
    row = lambda w: pl.BlockSpec((TM, w), lambda i: (i, 0))
    return pl.pallas_call(
        functools.partial(_inproj_kernel, tiles_per_seq=tiles_per_seq),
        grid=(t // TM,),
        in_specs=[row(d), _const_spec((1, d)), _layer_spec(wbig, l), _layer_spec(wvt, l),
                  _layer_spec(wsmall, l), _const_spec((1, LANES)), _const_spec((TM, TM)),
                  _const_spec(place.shape)],
        out_specs=[row(A_HEADS * LANES), row(A_HEADS * LANES),
                   pl.BlockSpec((1, A_WIDTH, TM), lambda i: (i, 0, 0)), row(GROUP_W), row(GROUP_W), row(LANES)],
        out_shape=[jax.ShapeDtypeStruct((t, A_HEADS * LANES), BF16), jax.ShapeDtypeStruct((t, A_HEADS * LANES), BF16),
                   jax.ShapeDtypeStruct((t // TM, A_WIDTH, TM), BF16),
                   jax.ShapeDtypeStruct((t, GROUP_W), BF16), jax.ShapeDtypeStruct((t, GROUP_W), BF16),
                   jax.ShapeDtypeStruct((t, LANES), F32)],
        scratch_shapes=[pltpu.VMEM((1, LANES), F32)],
        compiler_params=_cparams(("arbitrary",)),
        name="inproj",
    )(x2, g, wbig, wvt, wsmall, bf_pad, tril, place)


def _attn_kernel(q_ref, k_ref, vt_ref, *rest, n_side):
    side_in, o_ref, side_out = rest[:n_side], rest[n_side], rest[n_side + 1:2 * n_side + 1]
    m_ref, acc_ref, exc_ref = rest[2 * n_side + 1:]
    for src_ref, dst_ref in zip(side_in, side_out):
        dst_ref[...] = src_ref[...].astype(BF16)
    qi = pl.program_id(2)
    ones = jnp.ones((ACC_ROWS - A_HEAD_DIM, TQ), BF16)

    def _score(ki, j):
        return _dot_nt(k_ref[ki, :, j * LANES:(j + 1) * LANES], q_ref[:, j * LANES:(j + 1) * LANES])

    def _values(ki, j):
        return jnp.concatenate([vt_ref[ki, j * A_HEAD_DIM:(j + 1) * A_HEAD_DIM, :], ones], axis=0)

    def _diag_step(ki):
        key = lax.broadcasted_iota(I32, (TQ, TQ), 0)
        qry = lax.broadcasted_iota(I32, (TQ, TQ), 1)
        keep = key <= qry
        scores = [_score(ki, j) for j in range(ATT_HEADS)]
        for j in range(ATT_HEADS):
            st = jnp.where(keep, scores[j], NEG)
            m_new = jnp.max(st, axis=0, keepdims=True)
            acc_ref[j] = _dot(_values(ki, j), jnp.exp2(st - m_new).astype(BF16))
            m_ref[j] = m_new

    def _exact_step(ki):
        scores = [_score(ki, j) for j in range(ATT_HEADS)]
        for j in range(ATT_HEADS):
            st = scores[j]
            m_prev = m_ref[j]
            m_new = jnp.maximum(m_prev, jnp.max(st, axis=0, keepdims=True))
            pr = jnp.exp2(st - m_new).astype(BF16)
            acc_ref[j] = jnp.exp2(m_prev - m_new) * acc_ref[j] + _dot(_values(ki, j), pr)
            m_ref[j] = m_new

    def _fast_step(ki, carry):
        excess = exc_ref[...]
        pending = _score(ki, 0)
        for j in range(ATT_HEADS):
            st = pending
            if j + 1 < ATT_HEADS:
                pending = _score(ki, j + 1)
            m_prev = m_ref[j]
            pr = jnp.exp2(st - m_prev).astype(BF16)
            tmax = jnp.max(st, axis=0, keepdims=True)
            excess = jnp.maximum(excess, tmax - m_prev)
            m_new = jnp.maximum(m_prev, tmax)
            acc_ref[j] = (acc_ref[j] + _dot(_values(ki, j), pr)) * jnp.exp2(m_prev - m_new)
            m_ref[j] = m_new
        exc_ref[...] = excess
        return carry

    def _exact_below(ki, carry):
        _exact_step(ki)
        return carry

    exc_ref[...] = jnp.zeros_like(exc_ref)
    _diag_step(qi)
    lax.fori_loop(0, qi, _fast_step, 0)

    @pl.when(jnp.max(exc_ref[...]) > ATT_GUARD)
    def _():
        _diag_step(qi)
        lax.fori_loop(0, qi, _exact_below, 0)

    ot = jnp.concatenate([acc_ref[j, 0:A_HEAD_DIM, :] / acc_ref[j, A_HEAD_DIM:A_HEAD_DIM + 1, :]
                          for j in range(ATT_HEADS)], axis=0)
    o_ref[...] = ot.T.astype(BF16)


def _attention(qp, kp, vt, b, s, side=()):
    nq = s // TQ
    ng = A_HEADS // ATT_HEADS
    steps = b * ng * nq
    kp3 = kp.reshape(b * nq, TQ, A_HEADS * LANES)
    assert all(a.shape[0] % (steps * ROW_GRAN) == 0 for a in side)
    slab = lambda a: pl.BlockSpec((a.shape[0] // steps, a.shape[1]), lambda bi, p, qi: ((bi * ng + p) * nq + qi, 0))
    outs = pl.pallas_call(
        functools.partial(_attn_kernel, n_side=len(side)),
        grid=(b, ng, nq),
        in_specs=[pl.BlockSpec((TQ, ATT_HEADS * LANES), lambda bi, p, qi: (bi * nq + qi, p)),
                  pl.BlockSpec((nq, TQ, ATT_HEADS * LANES), lambda bi, p, qi: (bi, 0, p)),
                  pl.BlockSpec((nq, ATT_HEADS * A_HEAD_DIM, TQ), lambda bi, p, qi: (bi, p, 0))]
        + [slab(a) for a in side],
        out_specs=[pl.BlockSpec((TQ, ATT_HEADS * A_HEAD_DIM), lambda bi, p, qi: (bi * nq + qi, p))]
        + [slab(a) for a in side],
        out_shape=[jax.ShapeDtypeStruct((b * s, A_WIDTH), BF16)]
        + [jax.ShapeDtypeStruct(a.shape, BF16) for a in side],
        scratch_shapes=[pltpu.VMEM((ATT_HEADS, 1, TQ), F32), pltpu.VMEM((ATT_HEADS, ACC_ROWS, TQ), F32),
                        pltpu.VMEM((1, TQ), F32)],
        compiler_params=_cparams(("parallel", "parallel", "arbitrary")),
        name="fox_attention",
    )(qp, kp3, vt, *side)
    return outs[0], outs[1:]


def _gla_kernel(gla_ref, small_ref, wa2_ref, ba_ref, ggla_ref, tri_ref, oc_ref, st_ref, sall_ref):
    @pl.when(pl.program_id(1) == 0)
    def _():
        st_ref[...] = jnp.zeros_like(st_ref)

    nc = TL // GLA_CHUNK
    nb = MIX_BATCH

    tri = tri_ref[...]
    q_t, k_t, ks_t, dec_t = [], [], [], []
    for i in range(nb):
        la = _dot(small_ref[i].astype(BF16), wa2_ref[...]) + ba_ref[...]
        la = _log_sigmoid(la) * (1.0 / C_GATE_TEMP)
        hi, mid, lo = _split3(la)
        bcum = _dot(tri, hi) + _dot(tri, mid) + _dot(tri, lo)
        bl = bcum.reshape(nc, GLA_CHUNK, C_KEY_WIDTH)[:, GLA_CHUNK - 1:GLA_CHUNK, :]
        blast = jnp.broadcast_to(bl, (nc, GLA_CHUNK, C_KEY_WIDTH)).reshape(TL, C_KEY_WIDTH)
        q = gla_ref[i, :, 0:C_KEY_WIDTH].astype(F32)
        k = gla_ref[i, :, C_KEY_WIDTH:2 * C_KEY_WIDTH].astype(F32)
        q_t.append(q * (C_KEY_DIM ** -0.5) * jnp.exp(bcum))
        k_t.append((k * jnp.exp(-bcum)).astype(BF16))
        ks_t.append((k * jnp.exp(blast - bcum)).T)
        dec_t.append(jnp.broadcast_to(jnp.exp(bl), (nc, LANES, C_KEY_WIDTH)).reshape(nc * LANES, C_KEY_WIDTH).T)

    r_i = lax.broadcasted_iota(I32, (TL, TL), 0)
    c_i = lax.broadcasted_iota(I32, (TL, TL), 1)
    same = (r_i // GLA_CHUNK) == (c_i // GLA_CHUNK)
    intra = same & (c_i <= r_i)
    lane_k = lax.broadcasted_iota(I32, (1, C_KEY_WIDTH), 1)
    lane = lax.broadcasted_iota(I32, (1, LANES), 1)

    for h in range(C_HEADS):
        for i in range(nb):
            v_h = gla_ref[i, :, 2 * C_KEY_WIDTH + h * C_VAL_DIM: 2 * C_KEY_WIDTH + (h + 1) * C_VAL_DIM]
            q_h = jnp.where(lane_k // C_KEY_DIM == h, q_t[i], 0.0).astype(BF16)
            a = jnp.where(intra, _dot_nt(q_h, k_t[i]), 0.0).astype(BF16)

            q2 = q_t[i][:, (h // 2) * LANES:(h // 2 + 1) * LANES]
            q2r = pltpu.roll(q2, C_KEY_DIM, 1)
            dup = jnp.where((lane < C_KEY_DIM) == (h % 2 == 0), q2, q2r)
            q_exp = jnp.where(same, jnp.concatenate([dup] * (TL // LANES), axis=1), 0.0).astype(BF16)

            ks_h = ks_t[i][h * C_KEY_DIM:(h + 1) * C_KEY_DIM, :]
            k_exp = jnp.where(same, jnp.concatenate([ks_h] * nc, axis=0), 0.0).astype(BF16)
            kv = _dot(k_exp, v_h)

            st = st_ref[i, h]
            for c in range(nc):
                sall_ref[i, h, c * GLA_CHUNK:(c + 1) * GLA_CHUNK, :] = st.astype(BF16)
                dec = dec_t[i][h * C_KEY_DIM:(h + 1) * C_KEY_DIM, c * LANES:(c + 1) * LANES]
                st = dec * st + kv[c * GLA_CHUNK:(c + 1) * GLA_CHUNK, :]
            st_ref[i, h] = st

            o = _dot(a, v_h) + _dot(q_exp, sall_ref[i, h])
            o = o * lax.rsqrt(jnp.mean(o * o, axis=-1, keepdims=True) + EPS)
            o = o * ggla_ref[:, h * C_VAL_DIM:(h + 1) * C_VAL_DIM]
            r = gla_ref[i, :, 4 * C_KEY_WIDTH + h * C_VAL_DIM: 4 * C_KEY_WIDTH + (h + 1) * C_VAL_DIM].astype(F32)
            oc_ref[i, :, h * C_VAL_DIM:(h + 1) * C_VAL_DIM] = (o * (r * _sigmoid(r))).astype(BF16)


def _gla(gla_in, small, wa2p, ba, ggla, tri_bd, b, s):
    ns = s // TL
    blk = lambda w: pl.BlockSpec((MIX_BATCH, TL, w), lambda bi, si: (bi, si, 0))
    oc = pl.pallas_call(
        _gla_kernel,
        grid=(b // MIX_BATCH, ns),
        in_specs=[blk(GROUP_W), blk(LANES),
                  _const_spec((LANES, C_KEY_WIDTH)),
                  _const_spec((1, C_KEY_WIDTH)),
                  _const_spec((1, C_VAL_WIDTH)),
                  _const_spec((TL, TL))],
        out_specs=blk(C_VAL_WIDTH),
        out_shape=jax.ShapeDtypeStruct((b, s, C_VAL_WIDTH), BF16),
        scratch_shapes=[pltpu.VMEM((MIX_BATCH, C_HEADS, C_KEY_DIM, C_VAL_DIM), F32),
                        pltpu.VMEM((MIX_BATCH, C_HEADS, TL, C_VAL_DIM), BF16)],
        compiler_params=_cparams(("parallel", "arbitrary")),
        name="gla",
    )(gla_in.reshape(b, s, GROUP_W), small.reshape(b, s, LANES), wa2p, ba, ggla, tri_bd)
    return oc.reshape(b * s, C_VAL_WIDTH)


def _merge_kernel(x_ref, g_ref, oa_ref, conv_ref, convw_ref, oc_ref, wg_ref, bgate_ref, wpa_ref, wpb_ref,
                  wpc_ref, wo_ref, out_ref, zprev_ref, *, tiles_per_seq):
    @pl.when(pl.program_id(0) % tiles_per_seq == 0)
    def _():
        zprev_ref[...] = jnp.zeros_like(zprev_ref)

    u = conv_ref[:, 0:B_WIDTH].astype(F32)
    bg = conv_ref[:, B_WIDTH:2 * B_WIDTH].astype(F32)
    cg = conv_ref[:, 2 * B_WIDTH:3 * B_WIDTH].astype(F32)
    z = cg * u
    row = lax.broadcasted_iota(I32, (TM, B_WIDTH), 0)
    zp = zprev_ref[...]
    z1 = jnp.where(row == 0, zp[7:8, :], pltpu.roll(z, 1, 0))
    z2 = jnp.where(row == 0, zp[6:7, :], jnp.where(row == 1, zp[7:8, :], pltpu.roll(z, 2, 0)))
    cw = convw_ref[...]
    o_b = (bg * (cw[0:1, :] * z2 + cw[1:2, :] * z1 + cw[2:3, :] * z)).astype(BF16)
    zprev_ref[...] = z[TM - 8:TM, :]

    x = x_ref[...]
    d = x.shape[1]
    h = _rms(x, g_ref[...]).astype(BF16)
    merged = None
    for j, (o, wp_ref) in enumerate(((oa_ref[...], wpa_ref), (o_b, wpb_ref), (oc_ref[...], wpc_ref))):
        gate = _sigmoid(_dot(h, wg_ref[0, :, j * d:(j + 1) * d]) + bgate_ref[:, j * d:(j + 1) * d])
        term = gate * _dot(o, wp_ref[0])
        merged = term if merged is None else merged + term
    out_ref[...] = x + _dot(merged.astype(BF16), wo_ref[0])


def _merge(x2, g, oa, conv_in, convw8, oc, wg, bgate, wpa, wpb, wpc, wo, l, tiles_per_seq):
    t, d = x2.shape
    row = lambda w: pl.BlockSpec((TM, w), lambda i: (i, 0))
    return pl.pallas_call(
        functools.partial(_merge_kernel, tiles_per_seq=tiles_per_seq),
        grid=(t // TM,),
        in_specs=[row(d), _const_spec((1, d)), row(A_WIDTH), row(GROUP_W), _const_spec((8, B_WIDTH)),
                  row(C_VAL_WIDTH), _layer_spec(wg, l), _const_spec(bgate.shape), _layer_spec(wpa, l),
                  _layer_spec(wpb, l), _layer_spec(wpc, l), _layer_spec(wo, l)],
        out_specs=row(d),
        out_shape=jax.ShapeDtypeStruct((t, d), F32),
        scratch_shapes=[pltpu.VMEM((8, B_WIDTH), F32)],
        compiler_params=_cparams(("arbitrary",)),
        name="merge_outproj",
    )(x2, g, oa, conv_in, convw8, oc, wg, bgate, wpa, wpb, wpc, wo)


def _swiglu_tile(h, w1_ref, w3_ref, w2_ref):
    ff = w1_ref.shape[-1]
    acc = None
    for c in range(0, ff, FF_CHUNK):
        a = _dot(h, w1_ref[:, c:c + FF_CHUNK])
        b = _dot(h, w3_ref[:, c:c + FF_CHUNK])
        g = (a * _sigmoid(a) * b).astype(BF16)
        part = _dot(g, w2_ref[c:c + FF_CHUNK, :])
        acc = part if acc is None else acc + part
    return acc


def _ffn_kernel(x_ref, g_ref, w1_ref, w3_ref, w2_ref, *rest):
    if len(rest) == 1:
        (out_ref,) = rest
    else:
        m1_ref, m3_ref, m2_ref, out_ref, c1_ref, c3_ref, c2_ref, zero_ref = rest
        c1_ref[...] = m1_ref[...].astype(BF16)
        c3_ref[...] = m3_ref[...].astype(BF16)
        c2_ref[...] = m2_ref[...].astype(BF16)
        zero_ref[...] = jnp.zeros_like(zero_ref)
    x = x_ref[...]
    h = _rms(x, g_ref[...]).astype(BF16)
    out_ref[...] = x + _swiglu_tile(h, w1_ref.at[0], w3_ref.at[0], w2_ref.at[0])


def _dense_ffn(x2, g, w1, w3, w2, i, moe_prep=None):
    t, d = x2.shape
    steps = t // TM
    row = pl.BlockSpec((TM, d), lambda i: (i, 0))
    in_specs = [row, _const_spec((1, d)), _layer_spec(w1, i), _layer_spec(w3, i), _layer_spec(w2, i)]
    out_specs = [row]
    out_shape = [jax.ShapeDtypeStruct((t, d), F32)]
    args = [x2, g, w1, w3, w2]
    if moe_prep is not None:
        m1, m3, m2, sorted_rows = moe_prep
        flat = [m1.reshape(-1, m1.shape[-1]), m3.reshape(-1, m3.shape[-1]), m2.reshape(-1, m2.shape[-1])]
        shapes = [a.shape for a in flat] + [(sorted_rows, d)]
        assert all(r % (steps * ROW_GRAN) == 0 for r, _ in shapes)
        slab = lambda shp: pl.BlockSpec((shp[0] // steps, shp[1]), lambda i: (i, 0))
        in_specs += [slab(s) for s in shapes[:3]]
        out_specs += [slab(s) for s in shapes]
        out_shape += [jax.ShapeDtypeStruct(s, BF16) for s in shapes]
        args += flat
    outs = pl.pallas_call(
        _ffn_kernel,
        grid=(steps,),
        in_specs=in_specs,
        out_specs=out_specs,
        out_shape=out_shape,
        compiler_params=_cparams(("parallel",)),
        name="dense_ffn",
    )(*args)
    if moe_prep is None:
        return outs[0], None
    m1, m3, m2, _ = moe_prep
    return outs[0], (outs[1].reshape(m1.shape), outs[2].reshape(m3.shape), outs[3].reshape(m2.shape), outs[4])


def _router_kernel(x_ref, g_ref, wr_ref, stril_ref, h_ref, rt_ref, seg_ref):
    h = _rms(x_ref[...], g_ref[...])
    h_hi = h.astype(BF16)
    h_ref[...] = h_hi
    h_lo = (h - h_hi.astype(F32)).astype(BF16)
    wr = wr_ref[...]
    w_hi = wr.astype(BF16)
    w_lo = (wr - w_hi.astype(F32)).astype(BF16)
    both = _dot(h_hi, jnp.concatenate([w_hi, w_lo], axis=1))
    logits = both[:, 0:LANES] + both[:, LANES:2 * LANES] + _dot(h_lo, w_hi)

    lane = lax.broadcasted_iota(I32, (TM, LANES), 1)
    lane_f = lane.astype(F32)
    lg = jnp.where(lane < N_EXPERTS, logits, NEG)
    m1 = jnp.max(lg, axis=-1, keepdims=True)
    i1 = jnp.min(jnp.where(lg == m1, lane_f, float(LANES)), axis=-1, keepdims=True)
    oh1 = lane_f == i1
    lg2 = jnp.where(oh1, NEG, lg)
    m2 = jnp.max(lg2, axis=-1, keepdims=True)
    i2 = jnp.min(jnp.where(lg2 == m2, lane_f, float(LANES)), axis=-1, keepdims=True)
    oh2 = lane_f == i2
    e = jnp.exp(m2 - m1)
    w1 = 1.0 / (1.0 + e)
    w2 = e / (1.0 + e)

    sel = jnp.where(oh1 | oh2, 1.0, 0.0)
    before = _dot(stril_ref[...], sel.astype(BF16))
    cnt = jnp.sum(sel, axis=0, keepdims=True)
    pc = jnp.floor((cnt + (ROW_GRAN - 1)) * (1.0 / ROW_GRAN)) * ROW_GRAN
    lane1 = lax.broadcasted_iota(I32, (1, LANES), 1)
    incl = pc
    for sh in (1, 2, 4):
        incl = incl + jnp.where(lane1 >= sh, pltpu.roll(incl, sh, 1), 0.0)
    off = incl - pc
    pos = before + off
    d1 = jnp.sum(jnp.where(oh1, pos, 0.0), axis=-1, keepdims=True)
    d2 = jnp.sum(jnp.where(oh2, pos, 0.0), axis=-1, keepdims=True)
    rt_ref[...] = jnp.where(lane == 0, d1, jnp.where(lane == 1, d2, jnp.where(lane == 2, w1, jnp.where(lane == 3, w2, 0.0))))
    row8 = lax.broadcasted_iota(I32, (8, LANES), 0)
    seg_ref[0] = jnp.where(row8 == 0, pc, jnp.where(row8 == 1, off, 0.0))


def _router(x2, g, wr_pad, stril):
    t, d = x2.shape
    nb = t // TM
    row = lambda w: pl.BlockSpec((TM, w), lambda i: (i, 0))
    return pl.pallas_call(
        _router_kernel,
        grid=(nb,),
        in_specs=[row(d), _const_spec((1, d)), _const_spec(wr_pad.shape), _const_spec((TM, TM))],
        out_specs=[row(d), row(LANES), pl.BlockSpec((1, 8, LANES), lambda i: (i, 0, 0))],
        out_shape=[jax.ShapeDtypeStruct((t, d), BF16), jax.ShapeDtypeStruct((t, LANES), F32),
                   jax.ShapeDtypeStruct((nb, 8, LANES), F32)],
        compiler_params=_cparams(("parallel",)),
        name="moe_router",
    )(x2, g, wr_pad, stril)


def _chunk_copy(src_ref, src_row, dst_ref, dst_row, sem):
    return pltpu.make_async_copy(src_ref.at[pl.ds(pl.multiple_of(src_row, ROW_GRAN), ROW_GRAN)],
                                 dst_ref.at[pl.ds(pl.multiple_of(dst_row, ROW_GRAN), ROW_GRAN)], sem)


def _segment_starts(b, gstart_ref, nch_ref, off_ref, hbm_ref, buf_ref, sem, to_hbm):
    for e in range(N_EXPERTS):
        n = nch_ref[b * N_EXPERTS + e]
        g0 = gstart_ref[b * N_EXPERTS + e]
        o0 = off_ref[b * N_EXPERTS + e]

        def start(c, carry, g0=g0, o0=o0):
            if to_hbm:
                _chunk_copy(buf_ref, o0 + c * ROW_GRAN, hbm_ref, g0 + c * ROW_GRAN, sem).start()
            else:
                _chunk_copy(hbm_ref, g0 + c * ROW_GRAN, buf_ref, o0 + c * ROW_GRAN, sem).start()
            return carry

        lax.fori_loop(0, n, start, 0)


def _segment_waits(b, nch_ref, buf_ref, sem):
    total = 0
    for e in range(N_EXPERTS):
        total = total + nch_ref[b * N_EXPERTS + e]

    def wait(c, carry):
        _chunk_copy(buf_ref, 0, buf_ref, 0, sem).wait()
        return carry

    lax.fori_loop(0, total, wait, 0)


def _dispatch_kernel(gstart_ref, nch_ref, off_ref, h_ref, rt_ref, xs_in_ref, xs_ref, buf_ref, sem):
    del xs_in_ref
    b = pl.program_id(0)
    slot = b % 2
    dest = rt_ref[...].T
    r = lax.broadcasted_iota(I32, (SORT_ROWS, TM), 0).astype(F32)
    perm = jnp.where((r == dest[0:1, :]) | (r == dest[1:2, :]), 1.0, 0.0).astype(BF16)
    buf_ref[slot] = _dot(perm, h_ref[...]).astype(BF16)
    _segment_starts(b, gstart_ref, nch_ref, off_ref, xs_ref, buf_ref.at[slot], sem.at[slot], to_hbm=True)

    @pl.when(b > 0)
    def _():
        _segment_waits(b - 1, nch_ref, buf_ref.at[1 - slot], sem.at[1 - slot])

    @pl.when(b == pl.num_programs(0) - 1)
    def _():
        _segment_waits(b, nch_ref, buf_ref.at[slot], sem.at[slot])


def _dispatch(gstart, nch, off, h2, rt, xs_zero):
    t, d = h2.shape
    grid_spec = pltpu.PrefetchScalarGridSpec(
        num_scalar_prefetch=3,
        grid=(t // TM,),
        in_specs=[pl.BlockSpec((TM, d), lambda i, *_: (i, 0)),
                  pl.BlockSpec((TM, LANES), lambda i, *_: (i, 0)),
                  pl.BlockSpec(memory_space=pl.ANY)],
        out_specs=pl.BlockSpec(memory_space=pl.ANY),
        scratch_shapes=[pltpu.VMEM((2, SORT_ROWS, d), BF16), pltpu.SemaphoreType.DMA((2,))],
    )
    return pl.pallas_call(
        _dispatch_kernel,
        grid_spec=grid_spec,
        out_shape=jax.ShapeDtypeStruct(xs_zero.shape, xs_zero.dtype),
        input_output_aliases={5: 0},
        compiler_params=_cparams(("arbitrary",)),
        name="moe_dispatch",
    )(gstart, nch, off, h2, rt, xs_zero)


def _group_ffn_kernel(te_ref, ta_ref, ts_ref, xs_ref, w1_ref, w3_ref, w2_ref, y_ref):
    del te_ref, ts_ref
    active = ta_ref[pl.program_id(0)] == 1

    @pl.when(active)
    def _():
        y_ref[...] = _swiglu_tile(xs_ref[...], w1_ref.at[0], w3_ref.at[0], w2_ref.at[0]).astype(BF16)

    @pl.when(jnp.logical_not(active))
    def _():
        y_ref[...] = jnp.zeros_like(y_ref)


def _group_ffn(tile_expert, tile_active, tile_src, xs, w1, w3, w2):
    ns, d = xs.shape
    ff = w1.shape[-1]
    grid_spec = pltpu.PrefetchScalarGridSpec(
        num_scalar_prefetch=3,
        grid=(ns // TMM,),
        in_specs=[pl.BlockSpec((TMM, d), lambda i, te, ta, ts: (ts[i], 0)),
                  pl.BlockSpec((1, d, ff), lambda i, te, ta, ts: (te[i], 0, 0)),
                  pl.BlockSpec((1, d, ff), lambda i, te, ta, ts: (te[i], 0, 0)),
                  pl.BlockSpec((1, ff, d), lambda i, te, ta, ts: (te[i], 0, 0))],
        out_specs=pl.BlockSpec((TMM, d), lambda i, te, ta, ts: (i, 0)),
    )
    return pl.pallas_call(
        _group_ffn_kernel,
        grid_spec=grid_spec,
        out_shape=jax.ShapeDtypeStruct((ns, d), BF16),
        compiler_params=_cparams(("arbitrary",)),
        name="moe_group_ffn",
    )(tile_expert, tile_active, tile_src, xs, w1, w3, w2)


def _combine_kernel(gstart_ref, nch_ref, off_ref, x_ref, rt_ref, gf_ref, y_ref, out_ref, buf_ref, sem,
                    *, final_norm):
    b = pl.program_id(0)
    slot = b % 2

    def fetch(blk, s):
        buf_ref[s] = jnp.zeros(buf_ref.shape[1:], BF16)
        _segment_starts(blk, gstart_ref, nch_ref, off_ref, y_ref, buf_ref.at[s], sem.at[s], to_hbm=False)

    @pl.when(b == 0)
    def _():
        fetch(b, slot)

    @pl.when(b + 1 < pl.num_programs(0))
    def _():
        fetch(b + 1, 1 - slot)

    _segment_waits(b, nch_ref, buf_ref.at[slot], sem.at[slot])
    rt = rt_ref[...]
    r = lax.broadcasted_iota(I32, (TM, SORT_ROWS), 1).astype(F32)
    ybuf = buf_ref[slot]
    y1 = _dot(jnp.where(r == rt[:, 0:1], 1.0, 0.0).astype(BF16), ybuf)
    y2 = _dot(jnp.where(r == rt[:, 1:2], 1.0, 0.0).astype(BF16), ybuf)
    x = x_ref[...] + rt[:, 2:3] * y1 + rt[:, 3:4] * y2
    out_ref[...] = _rms(x, gf_ref[...]) if final_norm else x


def _combine(gstart, nch, off, x2, rt, g_final, y, final_norm):
    t, d = x2.shape
    grid_spec = pltpu.PrefetchScalarGridSpec(
        num_scalar_prefetch=3,
        grid=(t // TM,),
        in_specs=[pl.BlockSpec((TM, d), lambda i, *_: (i, 0)),
                  pl.BlockSpec((TM, LANES), lambda i, *_: (i, 0)),
                  pl.BlockSpec((1, d), lambda i, *_: (0, 0)),
                  pl.BlockSpec(memory_space=pl.ANY)],
        out_specs=pl.BlockSpec((TM, d), lambda i, *_: (i, 0)),
        scratch_shapes=[pltpu.VMEM((2, SORT_ROWS, d), BF16), pltpu.SemaphoreType.DMA((2,))],
    )
    return pl.pallas_call(
        functools.partial(_combine_kernel, final_norm=final_norm),
        grid_spec=grid_spec,
        out_shape=jax.ShapeDtypeStruct((t, d), F32),
        compiler_params=_cparams(("arbitrary",)),
        name="moe_combine",
    )(gstart, nch, off, x2, rt, g_final, y)


def _norm_kernel(x_ref, g_ref, out_ref):
    out_ref[...] = _rms(x_ref[...], g_ref[...])


def _final_norm(x2, g):
    t, d = x2.shape
    row = pl.BlockSpec((TM, d), lambda i: (i, 0))
    return pl.pallas_call(
        _norm_kernel, grid=(t // TM,), in_specs=[row, _const_spec((1, d))], out_specs=row,
        out_shape=jax.ShapeDtypeStruct((t, d), F32), compiler_params=_cparams(("parallel",)),
        name="final_norm",
    )(x2, g)


def _tril_blocks(n, block, strict=False):
    r = jnp.arange(n)[:, None]
    c = jnp.arange(n)[None, :]
    keep = (c < r) if strict else (c <= r)
    keep = keep & ((r // block) == (c // block))
    return keep.astype(BF16)


def _sorted_rows(t):
    return ((2 * t + (t // TM) * N_EXPERTS * ROW_GRAN) // TMM + N_EXPERTS) * TMM


def _moe_layer(x2, g_ffn, w_router, w1, w3, w2, xs_zero, g_final, final_norm):
    t, d = x2.shape
    nb = t // TM
    wr_pad = jnp.pad(w_router, ((0, 0), (0, LANES - N_EXPERTS)))
    h2, rt, seg = _router(x2, g_ffn.reshape(1, d), wr_pad, _tril_blocks(TM, TM, strict=True))

    pc = seg[:, 0, :N_EXPERTS].astype(I32)
    off = seg[:, 1, :N_EXPERTS].astype(I32)
    region = ((jnp.sum(pc, axis=0) + TMM - 1) // TMM) * TMM
    ends = jnp.cumsum(region)
    gstart = (ends - region)[None, :] + jnp.cumsum(pc, axis=0) - pc
    n_tiles = _sorted_rows(t) // TMM
    tile_start = jnp.arange(n_tiles, dtype=I32) * TMM
    tile_expert = jnp.minimum(jnp.sum(tile_start[:, None] >= ends[None, :], axis=1), N_EXPERTS - 1).astype(I32)
    tile_active = (tile_start < ends[-1]).astype(I32)
    last_active = jnp.maximum(ends[-1] // TMM - 1, 0)
    tile_src = jnp.minimum(jnp.arange(n_tiles, dtype=I32), last_active).astype(I32)
    seg_args = (gstart.reshape(-1).astype(I32), (pc // ROW_GRAN).reshape(-1), off.reshape(-1))

    xs = _dispatch(*seg_args, h2, rt, xs_zero)
    y = _group_ffn(tile_expert, tile_active, tile_src, xs, w1, w3, w2)
    return _combine(*seg_args, x2, rt, g_final.reshape(1, d), y, final_norm)


def kernel(x, g_mix, w_in, b_f, conv_w, w_a2, b_a, g_gla, w_pa, w_pb, w_pc, w_gate, b_gate, w_o,
           g_ffn, ffn_w1, ffn_w3, ffn_w2, w_router, moe_w1, moe_w3, moe_w2, g_final):
    b, s, d = x.shape
    depth = g_mix.shape[0]
    t = b * s
    assert t % TM == 0 and s % TQ == 0 and TQ == TM and s % TL == 0 and TL % GLA_CHUNK == 0 and b % MIX_BATCH == 0
    x2 = x.reshape(t, d)
    tril_full = _tril_blocks(TM, TM)
    tril_chunk = _tril_blocks(TL, GLA_CHUNK)
    o_af = 3 * A_WIDTH
    o_b = o_af + A_HEADS
    o_c = o_b + 3 * B_WIDTH
    o_ca = o_c + 2 * C_KEY_WIDTH + C_VAL_WIDTH
    o_cr = o_ca + C_LOWRANK

    wq = w_in[:, :, 0:A_WIDTH] * (A_HEAD_DIM ** -0.5 * LOG2E)
    wbig = jnp.concatenate([wq, w_in[:, :, A_WIDTH:2 * A_WIDTH], w_in[:, :, o_b:o_c], w_in[:, :, o_c:o_ca],
                            w_in[:, :, o_cr:]], axis=2).astype(BF16)
    wvt = jnp.swapaxes(w_in[:, :, 2 * A_WIDTH:o_af], 1, 2).astype(BF16)
    wsmall = jnp.concatenate([w_in[:, :, o_af:o_b], w_in[:, :, o_ca:o_cr],
                              jnp.zeros((depth, d, LANES - A_HEADS - C_LOWRANK), F32)], axis=2).astype(BF16)

    done_final = False
    for l in range(depth):
        gm = g_mix[l].reshape(1, d)
        bf_pad = jnp.pad(b_f[l], (0, LANES - A_HEADS)).reshape(1, LANES)
        qp, kp, vt, conv_in, gla_in, small = _inproj(x2, gm, wbig, wvt, wsmall, l, bf_pad, tril_full, s // TM)
        if l == 0:
            stacks = (w_gate, w_pa, w_pb, w_pc, w_o, ffn_w1, ffn_w3)
            o_a, cast = _attention(qp, kp, vt, b, s, [a.reshape(-1, a.shape[-1]) for a in stacks])
            wg_b, wpa_b, wpb_b, wpc_b, wo_b, f1_b, f3_b = (c.reshape(a.shape) for c, a in zip(cast, stacks))
            f2_b = ffn_w2.astype(BF16)
        else:
            o_a, _ = _attention(qp, kp, vt, b, s)

        convw8 = jnp.pad(conv_w[l], ((0, 8 - conv_w.shape[1]), (0, 0)))
        wa2p = jnp.zeros((LANES, C_KEY_WIDTH), F32).at[A_HEADS:A_HEADS + C_LOWRANK].set(w_a2[l]).astype(BF16)
        out_c = _gla(gla_in, small, wa2p, b_a[l].reshape(1, -1), g_gla[l].reshape(1, -1), tril_chunk, b, s)

        x2 = _merge(x2, gm, o_a, conv_in, convw8, out_c, wg_b, b_gate[l].reshape(1, -1),
                    wpa_b, wpb_b, wpc_b, wo_b, l, s // TM)

        i = l // 2
        if l % 2 == 0:
            prep = (moe_w1[i], moe_w3[i], moe_w2[i], _sorted_rows(t)) if l + 1 < depth else None
            x2, moe_ready = _dense_ffn(x2, g_ffn[l].reshape(1, d), f1_b, f3_b, f2_b, i, prep)
        else:
            done_final = l == depth - 1
            x2 = _moe_layer(x2, g_ffn[l], w_router[i], *moe_ready, g_final, done_final)
    if not done_final:
        x2 = _final_norm(x2, g_final.reshape(1, d))
    return x2.reshape(b, s, d)
```

```python
import functools
import math

import numpy as np
import jax
import jax.numpy as jnp
from jax import lax
from jax.experimental import pallas as pl
from jax.experimental.pallas import tpu as pltpu

F32 = jnp.float32
BF16 = jnp.bfloat16
I32 = jnp.int32

EPS = 1e-6
NEG = -1e30
LOG2E = math.log2(math.e)

A_HEADS, A_HEAD_DIM = 8, 64
A_WIDTH = A_HEADS * A_HEAD_DIM
B_WIDTH = 512
C_HEADS, C_KEY_DIM, C_VAL_DIM = 4, 64, 128
C_KEY_WIDTH = C_HEADS * C_KEY_DIM
C_VAL_WIDTH = C_HEADS * C_VAL_DIM
C_LOWRANK = 16
C_GATE_TEMP = 16.0
GLA_CHUNK = 64
N_EXPERTS = 8
GROUP_W = 1536

LANES = 128
V7X_VMEM_BYTES = 64 * 1024 * 1024
VMEM_LIMIT = V7X_VMEM_BYTES - 8 * 1024 * 1024

TM = 512
TQ = 512
ATT_HEADS = 8
ACC_ROWS = A_HEAD_DIM + 16
ATT_GUARD = 64.0
TL = 256
MIX_BATCH = 4
FF_CHUNK = 256
TMM = 512
ROW_GRAN = 16
SORT_ROWS = 2 * TM + N_EXPERTS * ROW_GRAN


def _cparams(sem):
    return pltpu.CompilerParams(dimension_semantics=sem, vmem_limit_bytes=VMEM_LIMIT)


def _dot(a, b):
    return jnp.dot(a, b, preferred_element_type=F32)


def _dot_nt(a, b):
    return lax.dot_general(a, b, (((1,), (1,)), ((), ())), preferred_element_type=F32)


def _rms(x, g):
    return x * lax.rsqrt(jnp.mean(x * x, axis=-1, keepdims=True) + EPS) * g


def _log_sigmoid(z):
    return jnp.minimum(z, 0.0) - jnp.log(1.0 + jnp.exp(-jnp.abs(z)))


def _sigmoid(z):
    return 1.0 / (1.0 + jnp.exp(-z))


def _split3(x):
    hi = x.astype(BF16)
    r1 = x - hi.astype(F32)
    mid = r1.astype(BF16)
    lo = (r1 - mid.astype(F32)).astype(BF16)
    return hi, mid, lo


def _const_spec(shape):
    return pl.BlockSpec(shape, lambda *_: (0,) * len(shape))


def _layer_spec(stacked, l):
    return pl.BlockSpec((1,) + stacked.shape[1:], lambda *_: (l, 0, 0))


def _inproj_kernel(x_ref, g_ref, wbig_ref, wvt_ref, wsmall_ref, bf_ref, tril_ref, place_ref,
                   qp_ref, kp_ref, vt_ref, conv_ref, gla_ref, small_ref, carry_ref, *, tiles_per_seq):
    @pl.when(pl.program_id(0) % tiles_per_seq == 0)
    def _():
        carry_ref[...] = jnp.zeros_like(carry_ref)

    h = _rms(x_ref[...], g_ref[...]).astype(BF16)
    small = _dot(h, wsmall_ref[0])
    small_ref[...] = small
    q_all = _dot(h, wbig_ref[0, :, 0:A_WIDTH])
    k_all = _dot(h, wbig_ref[0, :, A_WIDTH:2 * A_WIDTH])
    off = 2 * A_WIDTH
    for ref in (conv_ref, gla_ref):
        width = ref.shape[1]
        for c in range(0, width, 512):
            ref[:, c:c + 512] = _dot(h, wbig_ref[0, :, off + c: off + c + 512]).astype(BF16)
        off += width
    vt_ref[0] = _dot_nt(wvt_ref[0], h).astype(BF16)

    lf = LOG2E * _log_sigmoid(small + bf_ref[...])
    hi, mid, lo = _split3(lf)
    tril = tril_ref[...]
    cs = _dot(tril, hi) + _dot(tril, mid) + _dot(tril, lo) + carry_ref[...]
    carry_ref[...] = cs[TM - 1:TM, :]
    fterms = _dot(jnp.concatenate(_split3(cs), axis=1), place_ref[...])

    lane = lax.broadcasted_iota(I32, (1, LANES), 1)
    is_x = lane < A_HEAD_DIM
    is_a = (lane >= A_HEAD_DIM) & (lane < A_HEAD_DIM + 3)
    is_b = (lane >= A_HEAD_DIM + 3) & (lane < A_HEAD_DIM + 6)
    for p in range(A_HEADS // 2):
        q2 = q_all[:, p * LANES:(p + 1) * LANES].astype(BF16).astype(F32)
        k2 = k_all[:, p * LANES:(p + 1) * LANES].astype(BF16).astype(F32)
        halves = ((q2, k2), (pltpu.roll(q2, A_HEAD_DIM, 1), pltpu.roll(k2, A_HEAD_DIM, 1)))
        for j in range(2):
            hd = 2 * p + j
            fh = pltpu.roll(fterms, (A_HEAD_DIM - 8 * hd) % LANES, 1)
            qh, kh = halves[j]
            qp_ref[:, hd * LANES:(hd + 1) * LANES] = jnp.where(
                is_x, qh, jnp.where(is_a, fh, jnp.where(is_b, 1.0, 0.0))).astype(BF16)
            kp_ref[:, hd * LANES:(hd + 1) * LANES] = jnp.where(
                is_x, kh, jnp.where(is_a, 1.0, jnp.where(is_b, -fh, 0.0))).astype(BF16)


def _aug_placement():
    place = np.zeros((3, LANES, LANES), np.float32)
    for h in range(A_HEADS):
        for c in range(3):
            place[c, h, 8 * h + c] = 1.0
            place[c, h, 8 * h + 3 + c] = 1.0
    return jnp.asarray(place.reshape(3 * LANES, LANES), BF16)


def _inproj(x2, g, wbig, wvt, wsmall, l, bf_pad, tril, tiles_per_seq):
    t, d = x2.shape
    place = _aug_placement()
    row = lambda w: pl.BlockSpec((TM, w), lambda i: (i, 0))
    return pl.pallas_call(
        functools.partial(_inproj_kernel, tiles_per_seq=tiles_per_seq),
        grid=(t // TM,),
        in_specs=[row(d), _const_spec((1, d)), _layer_spec(wbig, l), _layer_spec(wvt, l),
                  _layer_spec(wsmall, l), _const_spec((1, LANES)), _const_spec((TM, TM)),
                  _const_spec(place.shape)],
        out_specs=[row(A_HEADS * LANES), row(A_HEADS * LANES),
                   pl.BlockSpec((1, A_WIDTH, TM), lambda i: (i, 0, 0)), row(GROUP_W), row(GROUP_W), row(LANES)],
        out_shape=[jax.ShapeDtypeStruct((t, A_HEADS * LANES), BF16), jax.ShapeDtypeStruct((t, A_HEADS * LANES), BF16),
                   jax.ShapeDtypeStruct((t // TM, A_WIDTH, TM), BF16),
                   jax.ShapeDtypeStruct((t, GROUP_W), BF16), jax.ShapeDtypeStruct((t, GROUP_W), BF16),
                   jax.ShapeDtypeStruct((t, LANES), F32)],
        scratch_shapes=[pltpu.VMEM((1, LANES), F32)],
        compiler_params=_cparams(("arbitrary",)),
        name="inproj",
    )(x2, g, wbig, wvt, wsmall, bf_pad, tril, place)


def _attn_kernel(q_ref, k_ref, vt_ref, *rest, n_side):
    side_in, o_ref, side_out = rest[:n_side], rest[n_side], rest[n_side + 1:2 * n_side + 1]
    m_ref, acc_ref, exc_ref = rest[2 * n_side + 1:]
    for src_ref, dst_ref in zip(side_in, side_out):
        dst_ref[...] = src_ref[...].astype(BF16)
    qi = pl.program_id(2)
    ones = jnp.ones((ACC_ROWS - A_HEAD_DIM, TQ), BF16)

    def _score(ki, j):
        return _dot_nt(k_ref[ki, :, j * LANES:(j + 1) * LANES], q_ref[:, j * LANES:(j + 1) * LANES])

    def _values(ki, j):
        return jnp.concatenate([vt_ref[ki, j * A_HEAD_DIM:(j + 1) * A_HEAD_DIM, :], ones], axis=0)

    def _diag_step(ki):
        key = lax.broadcasted_iota(I32, (TQ, TQ), 0)
        qry = lax.broadcasted_iota(I32, (TQ, TQ), 1)
        keep = key <= qry
        scores = [_score(ki, j) for j in range(ATT_HEADS)]
        for j in range(ATT_HEADS):
            st = jnp.where(keep, scores[j], NEG)
            m_new = jnp.max(st, axis=0, keepdims=True)
            acc_ref[j] = _dot(_values(ki, j), jnp.exp2(st - m_new).astype(BF16))
            m_ref[j] = m_new

    def _exact_step(ki):
        scores = [_score(ki, j) for j in range(ATT_HEADS)]
        for j in range(ATT_HEADS):
            st = scores[j]
            m_prev = m_ref[j]
            m_new = jnp.maximum(m_prev, jnp.max(st, axis=0, keepdims=True))
            pr = jnp.exp2(st - m_new).astype(BF16)
            acc_ref[j] = jnp.exp2(m_prev - m_new) * acc_ref[j] + _dot(_values(ki, j), pr)
            m_ref[j] = m_new

    def _fast_step(ki, carry):
        excess = exc_ref[...]
        pending = _score(ki, 0)
        for j in range(ATT_HEADS):
            st = pending
            if j + 1 < ATT_HEADS:
                pending = _score(ki, j + 1)
            m_prev = m_ref[j]
            pr = jnp.exp2(st - m_prev).astype(BF16)
            tmax = jnp.max(st, axis=0, keepdims=True)
            excess = jnp.maximum(excess, tmax - m_prev)
            m_new = jnp.maximum(m_prev, tmax)
            acc_ref[j] = (acc_ref[j] + _dot(_values(ki, j), pr)) * jnp.exp2(m_prev - m_new)
            m_ref[j] = m_new
        exc_ref[...] = excess
        return carry

    def _exact_below(ki, carry):
        _exact_step(ki)
        return carry

    exc_ref[...] = jnp.zeros_like(exc_ref)
    _diag_step(qi)
    lax.fori_loop(0, qi, _fast_step, 0)

    @pl.when(jnp.max(exc_ref[...]) > ATT_GUARD)
    def _():
        _diag_step(qi)
        lax.fori_loop(0, qi, _exact_below, 0)

    ot = jnp.concatenate([acc_ref[j, 0:A_HEAD_DIM, :] / acc_ref[j, A_HEAD_DIM:A_HEAD_DIM + 1, :]
                          for j in range(ATT_HEADS)], axis=0)
    o_ref[...] = ot.T.astype(BF16)


def _attention(qp, kp, vt, b, s, side=()):
    nq = s // TQ
    ng = A_HEADS // ATT_HEADS
    steps = b * ng * nq
    kp3 = kp.reshape(b * nq, TQ, A_HEADS * LANES)
    assert all(a.shape[0] % (steps * ROW_GRAN) == 0 for a in side)
    slab = lambda a: pl.BlockSpec((a.shape[0] // steps, a.shape[1]), lambda bi, p, qi: ((bi * ng + p) * nq + qi, 0))
    outs = pl.pallas_call(
        functools.partial(_attn_kernel, n_side=len(side)),
        grid=(b, ng, nq),
        in_specs=[pl.BlockSpec((TQ, ATT_HEADS * LANES), lambda bi, p, qi: (bi * nq + qi, p)),
                  pl.BlockSpec((nq, TQ, ATT_HEADS * LANES), lambda bi, p, qi: (bi, 0, p)),
                  pl.BlockSpec((nq, ATT_HEADS * A_HEAD_DIM, TQ), lambda bi, p, qi: (bi, p, 0))]
        + [slab(a) for a in side],
        out_specs=[pl.BlockSpec((TQ, ATT_HEADS * A_HEAD_DIM), lambda bi, p, qi: (bi * nq + qi, p))]
        + [slab(a) for a in side],
        out_shape=[jax.ShapeDtypeStruct((b * s, A_WIDTH), BF16)]
        + [jax.ShapeDtypeStruct(a.shape, BF16) for a in side],
        scratch_shapes=[pltpu.VMEM((ATT_HEADS, 1, TQ), F32), pltpu.VMEM((ATT_HEADS, ACC_ROWS, TQ), F32),
                        pltpu.VMEM((1, TQ), F32)],
        compiler_params=_cparams(("parallel", "parallel", "arbitrary")),
        name="fox_attention",
    )(qp, kp3, vt, *side)
    return outs[0], outs[1:]


def _gla_kernel(gla_ref, small_ref, wa2_ref, ba_ref, ggla_ref, tri_ref, oc_ref, st_ref, sall_ref):
    @pl.when(pl.program_id(1) == 0)
    def _():
        st_ref[...] = jnp.zeros_like(st_ref)

    nc = TL // GLA_CHUNK
    nb = MIX_BATCH

    tri = tri_ref[...]
    q_t, k_t, ks_t, dec_t = [], [], [], []
    for i in range(nb):
        la = _dot(small_ref[i].astype(BF16), wa2_ref[...]) + ba_ref[...]
        la = _log_sigmoid(la) * (1.0 / C_GATE_TEMP)
        hi, mid, lo = _split3(la)
        bcum = _dot(tri, hi) + _dot(tri, mid) + _dot(tri, lo)
        bl = bcum.reshape(nc, GLA_CHUNK, C_KEY_WIDTH)[:, GLA_CHUNK - 1:GLA_CHUNK, :]
        blast = jnp.broadcast_to(bl, (nc, GLA_CHUNK, C_KEY_WIDTH)).reshape(TL, C_KEY_WIDTH)
        q = gla_ref[i, :, 0:C_KEY_WIDTH].astype(F32)
        k = gla_ref[i, :, C_KEY_WIDTH:2 * C_KEY_WIDTH].astype(F32)
        q_t.append(q * (C_KEY_DIM ** -0.5) * jnp.exp(bcum))
        k_t.append((k * jnp.exp(-bcum)).astype(BF16))
        ks_t.append((k * jnp.exp(blast - bcum)).T)
        dec_t.append(jnp.broadcast_to(jnp.exp(bl), (nc, LANES, C_KEY_WIDTH)).reshape(nc * LANES, C_KEY_WIDTH).T)

    r_i = lax.broadcasted_iota(I32, (TL, TL), 0)
    c_i = lax.broadcasted_iota(I32, (TL, TL), 1)
    same = (r_i // GLA_CHUNK) == (c_i // GLA_CHUNK)
    intra = same & (c_i <= r_i)
    lane_k = lax.broadcasted_iota(I32, (1, C_KEY_WIDTH), 1)
    lane = lax.broadcasted_iota(I32, (1, LANES), 1)

    for h in range(C_HEADS):
        for i in range(nb):
            v_h = gla_ref[i, :, 2 * C_KEY_WIDTH + h * C_VAL_DIM: 2 * C_KEY_WIDTH + (h + 1) * C_VAL_DIM]
            q_h = jnp.where(lane_k // C_KEY_DIM == h, q_t[i], 0.0).astype(BF16)
            a = jnp.where(intra, _dot_nt(q_h, k_t[i]), 0.0).astype(BF16)

            q2 = q_t[i][:, (h // 2) * LANES:(h // 2 + 1) * LANES]
            q2r = pltpu.roll(q2, C_KEY_DIM, 1)
            dup = jnp.where((lane < C_KEY_DIM) == (h % 2 == 0), q2, q2r)
            q_exp = jnp.where(same, jnp.concatenate([dup] * (TL // LANES), axis=1), 0.0).astype(BF16)

            ks_h = ks_t[i][h * C_KEY_DIM:(h + 1) * C_KEY_DIM, :]
            k_exp = jnp.where(same, jnp.concatenate([ks_h] * nc, axis=0), 0.0).astype(BF16)
            kv = _dot(k_exp, v_h)

            st = st_ref[i, h]
            for c in range(nc):
                sall_ref[i, h, c * GLA_CHUNK:(c + 1) * GLA_CHUNK, :] = st.astype(BF16)
                dec = dec_t[i][h * C_KEY_DIM:(h + 1) * C_KEY_DIM, c * LANES:(c + 1) * LANES]
                st = dec * st + kv[c * GLA_CHUNK:(c + 1) * GLA_CHUNK, :]
            st_ref[i, h] = st

            o = _dot(a, v_h) + _dot(q_exp, sall_ref[i, h])
            o = o * lax.rsqrt(jnp.mean(o * o, axis=-1, keepdims=True) + EPS)
            o = o * ggla_ref[:, h * C_VAL_DIM:(h + 1) * C_VAL_DIM]
            r = gla_ref[i, :, 4 * C_KEY_WIDTH + h * C_VAL_DIM: 4 * C_KEY_WIDTH + (h + 1) * C_VAL_DIM].astype(F32)
            oc_ref[i, :, h * C_VAL_DIM:(h + 1) * C_VAL_DIM] = (o * (r * _sigmoid(r))).astype(BF16)


def _gla(gla_in, small, wa2p, ba, ggla, tri_bd, b, s):
    ns = s // TL
    blk = lambda w: pl.BlockSpec((MIX_BATCH, TL, w), lambda bi, si: (bi, si, 0))
    oc = pl.pallas_call(
        _gla_kernel,
        grid=(b // MIX_BATCH, ns),
        in_specs=[blk(GROUP_W), blk(LANES),
                  _const_spec((LANES, C_KEY_WIDTH)),
                  _const_spec((1, C_KEY_WIDTH)),
                  _const_spec((1, C_VAL_WIDTH)),
                  _const_spec((TL, TL))],
        out_specs=blk(C_VAL_WIDTH),
        out_shape=jax.ShapeDtypeStruct((b, s, C_VAL_WIDTH), BF16),
        scratch_shapes=[pltpu.VMEM((MIX_BATCH, C_HEADS, C_KEY_DIM, C_VAL_DIM), F32),
                        pltpu.VMEM((MIX_BATCH, C_HEADS, TL, C_VAL_DIM), BF16)],
        compiler_params=_cparams(("parallel", "arbitrary")),
        name="gla",
    )(gla_in.reshape(b, s, GROUP_W), small.reshape(b, s, LANES), wa2p, ba, ggla, tri_bd)
    return oc.reshape(b * s, C_VAL_WIDTH)


def _merge_kernel(x_ref, g_ref, oa_ref, conv_ref, convw_ref, oc_ref, wg_ref, bgate_ref, wpa_ref, wpb_ref,
                  wpc_ref, wo_ref, out_ref, zprev_ref, *, tiles_per_seq):
    @pl.when(pl.program_id(0) % tiles_per_seq == 0)
    def _():
        zprev_ref[...] = jnp.zeros_like(zprev_ref)

    u = conv_ref[:, 0:B_WIDTH].astype(F32)
    bg = conv_ref[:, B_WIDTH:2 * B_WIDTH].astype(F32)
    cg = conv_ref[:, 2 * B_WIDTH:3 * B_WIDTH].astype(F32)
    z = cg * u
    row = lax.broadcasted_iota(I32, (TM, B_WIDTH), 0)
    zp = zprev_ref[...]
    z1 = jnp.where(row == 0, zp[7:8, :], pltpu.roll(z, 1, 0))
    z2 = jnp.where(row == 0, zp[6:7, :], jnp.where(row == 1, zp[7:8, :], pltpu.roll(z, 2, 0)))
    cw = convw_ref[...]
    o_b = (bg * (cw[0:1, :] * z2 + cw[1:2, :] * z1 + cw[2:3, :] * z)).astype(BF16)
    zprev_ref[...] = z[TM - 8:TM, :]

    x = x_ref[...]
    d = x.shape[1]
    h = _rms(x, g_ref[...]).astype(BF16)
    merged = None
    for j, (o, wp_ref) in enumerate(((oa_ref[...], wpa_ref), (o_b, wpb_ref), (oc_ref[...], wpc_ref))):
        gate = _sigmoid(_dot(h, wg_ref[0, :, j * d:(j + 1) * d]) + bgate_ref[:, j * d:(j + 1) * d])
        term = gate * _dot(o, wp_ref[0])
        merged = term if merged is None else merged + term
    out_ref[...] = x + _dot(merged.astype(BF16), wo_ref[0])


def _merge(x2, g, oa, conv_in, convw8, oc, wg, bgate, wpa, wpb, wpc, wo, l, tiles_per_seq):
    t, d = x2.shape
    row = lambda w: pl.BlockSpec((TM, w), lambda i: (i, 0))
    return pl.pallas_call(
        functools.partial(_merge_kernel, tiles_per_seq=tiles_per_seq),
        grid=(t // TM,),
        in_specs=[row(d), _const_spec((1, d)), row(A_WIDTH), row(GROUP_W), _const_spec((8, B_WIDTH)),
                  row(C_VAL_WIDTH), _layer_spec(wg, l), _const_spec(bgate.shape), _layer_spec(wpa, l),
                  _layer_spec(wpb, l), _layer_spec(wpc, l), _layer_spec(wo, l)],
        out_specs=row(d),
        out_shape=jax.ShapeDtypeStruct((t, d), F32),
        scratch_shapes=[pltpu.VMEM((8, B_WIDTH), F32)],
        compiler_params=_cparams(("arbitrary",)),
        name="merge_outproj",
    )(x2, g, oa, conv_in, convw8, oc, wg, bgate, wpa, wpb, wpc, wo)


def _swiglu_tile(h, w1_ref, w3_ref, w2_ref):
    ff = w1_ref.shape[-1]
    acc = None
    for c in range(0, ff, FF_CHUNK):
        a = _dot(h, w1_ref[:, c:c + FF_CHUNK])
        b = _dot(h, w3_ref[:, c:c + FF_CHUNK])
        g = (a * _sigmoid(a) * b).astype(BF16)
        part = _dot(g, w2_ref[c:c + FF_CHUNK, :])
        acc = part if acc is None else acc + part
    return acc


def _ffn_kernel(x_ref, g_ref, w1_ref, w3_ref, w2_ref, *rest):
    if len(rest) == 1:
        (out_ref,) = rest
    else:
        m1_ref, m3_ref, m2_ref, out_ref, c1_ref, c3_ref, c2_ref, zero_ref = rest
        c1_ref[...] = m1_ref[...].astype(BF16)
        c3_ref[...] = m3_ref[...].astype(BF16)
        c2_ref[...] = m2_ref[...].astype(BF16)
        zero_ref[...] = jnp.zeros_like(zero_ref)
    x = x_ref[...]
    h = _rms(x, g_ref[...]).astype(BF16)
    out_ref[...] = x + _swiglu_tile(h, w1_ref.at[0], w3_ref.at[0], w2_ref.at[0])


def _dense_ffn(x2, g, w1, w3, w2, i, moe_prep=None):
    t, d = x2.shape
    steps = t // TM
    row = pl.BlockSpec((TM, d), lambda i: (i, 0))
    in_specs = [row, _const_spec((1, d)), _layer_spec(w1, i), _layer_spec(w3, i), _layer_spec(w2, i)]
    out_specs = [row]
    out_shape = [jax.ShapeDtypeStruct((t, d), F32)]
    args = [x2, g, w1, w3, w2]
    if moe_prep is not None:
        m1, m3, m2, sorted_rows = moe_prep
        flat = [m1.reshape(-1, m1.shape[-1]), m3.reshape(-1, m3.shape[-1]), m2.reshape(-1, m2.shape[-1])]
        shapes = [a.shape for a in flat] + [(sorted_rows, d)]
        assert all(r % (steps * ROW_GRAN) == 0 for r, _ in shapes)
        slab = lambda shp: pl.BlockSpec((shp[0] // steps, shp[1]), lambda i: (i, 0))
        in_specs += [slab(s) for s in shapes[:3]]
        out_specs += [slab(s) for s in shapes]
        out_shape += [jax.ShapeDtypeStruct(s, BF16) for s in shapes]
        args += flat
    outs = pl.pallas_call(
        _ffn_kernel,
        grid=(steps,),
        in_specs=in_specs,
        out_specs=out_specs,
        out_shape=out_shape,
        compiler_params=_cparams(("parallel",)),
        name="dense_ffn",
    )(*args)
    if moe_prep is None:
        return outs[0], None
    m1, m3, m2, _ = moe_prep
    return outs[0], (outs[1].reshape(m1.shape), outs[2].reshape(m3.shape), outs[3].reshape(m2.shape), outs[4])


def _router_kernel(x_ref, g_ref, wr_ref, stril_ref, h_ref, rt_ref, seg_ref):
    h = _rms(x_ref[...], g_ref[...])
    h_hi = h.astype(BF16)
    h_ref[...] = h_hi
    h_lo = (h - h_hi.astype(F32)).astype(BF16)
    wr = wr_ref[...]
    w_hi = wr.astype(BF16)
    w_lo = (wr - w_hi.astype(F32)).astype(BF16)
    both = _dot(h_hi, jnp.concatenate([w_hi, w_lo], axis=1))
    logits = both[:, 0:LANES] + both[:, LANES:2 * LANES] + _dot(h_lo, w_hi)

    lane = lax.broadcasted_iota(I32, (TM, LANES), 1)
    lane_f = lane.astype(F32)
    lg = jnp.where(lane < N_EXPERTS, logits, NEG)
    m1 = jnp.max(lg, axis=-1, keepdims=True)
    i1 = jnp.min(jnp.where(lg == m1, lane_f, float(LANES)), axis=-1, keepdims=True)
    oh1 = lane_f == i1
    lg2 = jnp.where(oh1, NEG, lg)
    m2 = jnp.max(lg2, axis=-1, keepdims=True)
    i2 = jnp.min(jnp.where(lg2 == m2, lane_f, float(LANES)), axis=-1, keepdims=True)
    oh2 = lane_f == i2
    e = jnp.exp(m2 - m1)
    w1 = 1.0 / (1.0 + e)
    w2 = e / (1.0 + e)

    sel = jnp.where(oh1 | oh2, 1.0, 0.0)
    before = _dot(stril_ref[...], sel.astype(BF16))
    cnt = jnp.sum(sel, axis=0, keepdims=True)
    pc = jnp.floor((cnt + (ROW_GRAN - 1)) * (1.0 / ROW_GRAN)) * ROW_GRAN
    lane1 = lax.broadcasted_iota(I32, (1, LANES), 1)
    incl = pc
    for sh in (1, 2, 4):
        incl = incl + jnp.where(lane1 >= sh, pltpu.roll(incl, sh, 1), 0.0)
    off = incl - pc
    pos = before + off
    d1 = jnp.sum(jnp.where(oh1, pos, 0.0), axis=-1, keepdims=True)
    d2 = jnp.sum(jnp.where(oh2, pos, 0.0), axis=-1, keepdims=True)
    rt_ref[...] = jnp.where(lane == 0, d1, jnp.where(lane == 1, d2, jnp.where(lane == 2, w1, jnp.where(lane == 3, w2, 0.0))))
    row8 = lax.broadcasted_iota(I32, (8, LANES), 0)
    seg_ref[0] = jnp.where(row8 == 0, pc, jnp.where(row8 == 1, off, 0.0))


def _router(x2, g, wr_pad, stril):
    t, d = x2.shape
    nb = t // TM
    row = lambda w: pl.BlockSpec((TM, w), lambda i: (i, 0))
    return pl.pallas_call(
        _router_kernel,
        grid=(nb,),
        in_specs=[row(d), _const_spec((1, d)), _const_spec(wr_pad.shape), _const_spec((TM, TM))],
        out_specs=[row(d), row(LANES), pl.BlockSpec((1, 8, LANES), lambda i: (i, 0, 0))],
        out_shape=[jax.ShapeDtypeStruct((t, d), BF16), jax.ShapeDtypeStruct((t, LANES), F32),
                   jax.ShapeDtypeStruct((nb, 8, LANES), F32)],
        compiler_params=_cparams(("parallel",)),
        name="moe_router",
    )(x2, g, wr_pad, stril)


def _chunk_copy(src_ref, src_row, dst_ref, dst_row, sem):
    return pltpu.make_async_copy(src_ref.at[pl.ds(pl.multiple_of(src_row, ROW_GRAN), ROW_GRAN)],
                                 dst_ref.at[pl.ds(pl.multiple_of(dst_row, ROW_GRAN), ROW_GRAN)], sem)


def _segment_starts(b, gstart_ref, nch_ref, off_ref, hbm_ref, buf_ref, sem, to_hbm):
    for e in range(N_EXPERTS):
        n = nch_ref[b * N_EXPERTS + e]
        g0 = gstart_ref[b * N_EXPERTS + e]
        o0 = off_ref[b * N_EXPERTS + e]

        def start(c, carry, g0=g0, o0=o0):
            if to_hbm:
                _chunk_copy(buf_ref, o0 + c * ROW_GRAN, hbm_ref, g0 + c * ROW_GRAN, sem).start()
            else:
                _chunk_copy(hbm_ref, g0 + c * ROW_GRAN, buf_ref, o0 + c * ROW_GRAN, sem).start()
            return carry

        lax.fori_loop(0, n, start, 0)


def _segment_waits(b, nch_ref, buf_ref, sem):
    total = 0
    for e in range(N_EXPERTS):
        total = total + nch_ref[b * N_EXPERTS + e]

    def wait(c, carry):
        _chunk_copy(buf_ref, 0, buf_ref, 0, sem).wait()
        return carry

    lax.fori_loop(0, total, wait, 0)


def _dispatch_kernel(gstart_ref, nch_ref, off_ref, h_ref, rt_ref, xs_in_ref, xs_ref, buf_ref, sem):
    del xs_in_ref
    b = pl.program_id(0)
    slot = b % 2
    dest = rt_ref[...].T
    r = lax.broadcasted_iota(I32, (SORT_ROWS, TM), 0).astype(F32)
    perm = jnp.where((r == dest[0:1, :]) | (r == dest[1:2, :]), 1.0, 0.0).astype(BF16)
    buf_ref[slot] = _dot(perm, h_ref[...]).astype(BF16)
    _segment_starts(b, gstart_ref, nch_ref, off_ref, xs_ref, buf_ref.at[slot], sem.at[slot], to_hbm=True)

    @pl.when(b > 0)
    def _():
        _segment_waits(b - 1, nch_ref, buf_ref.at[1 - slot], sem.at[1 - slot])

    @pl.when(b == pl.num_programs(0) - 1)
    def _():
        _segment_waits(b, nch_ref, buf_ref.at[slot], sem.at[slot])


def _dispatch(gstart, nch, off, h2, rt, xs_zero):
    t, d = h2.shape
    grid_spec = pltpu.PrefetchScalarGridSpec(
        num_scalar_prefetch=3,
        grid=(t // TM,),
        in_specs=[pl.BlockSpec((TM, d), lambda i, *_: (i, 0)),
                  pl.BlockSpec((TM, LANES), lambda i, *_: (i, 0)),
                  pl.BlockSpec(memory_space=pl.ANY)],
        out_specs=pl.BlockSpec(memory_space=pl.ANY),
        scratch_shapes=[pltpu.VMEM((2, SORT_ROWS, d), BF16), pltpu.SemaphoreType.DMA((2,))],
    )
    return pl.pallas_call(
        _dispatch_kernel,
        grid_spec=grid_spec,
        out_shape=jax.ShapeDtypeStruct(xs_zero.shape, xs_zero.dtype),
        input_output_aliases={5: 0},
        compiler_params=_cparams(("arbitrary",)),
        name="moe_dispatch",
    )(gstart, nch, off, h2, rt, xs_zero)


def _group_ffn_kernel(te_ref, ta_ref, ts_ref, xs_ref, w1_ref, w3_ref, w2_ref, y_ref):
    del te_ref, ts_ref
    active = ta_ref[pl.program_id(0)] == 1

    @pl.when(active)
    def _():
        y_ref[...] = _swiglu_tile(xs_ref[...], w1_ref.at[0], w3_ref.at[0], w2_ref.at[0]).astype(BF16)

    @pl.when(jnp.logical_not(active))
    def _():
        y_ref[...] = jnp.zeros_like(y_ref)


def _group_ffn(tile_expert, tile_active, tile_src, xs, w1, w3, w2):
    ns, d = xs.shape
    ff = w1.shape[-1]
    grid_spec = pltpu.PrefetchScalarGridSpec(
        num_scalar_prefetch=3,
        grid=(ns // TMM,),
        in_specs=[pl.BlockSpec((TMM, d), lambda i, te, ta, ts: (ts[i], 0)),
                  pl.BlockSpec((1, d, ff), lambda i, te, ta, ts: (te[i], 0, 0)),
                  pl.BlockSpec((1, d, ff), lambda i, te, ta, ts: (te[i], 0, 0)),
                  pl.BlockSpec((1, ff, d), lambda i, te, ta, ts: (te[i], 0, 0))],
        out_specs=pl.BlockSpec((TMM, d), lambda i, te, ta, ts: (i, 0)),
    )
    return pl.pallas_call(
        _group_ffn_kernel,
        grid_spec=grid_spec,
        out_shape=jax.ShapeDtypeStruct((ns, d), BF16),
        compiler_params=_cparams(("arbitrary",)),
        name="moe_group_ffn",
    )(tile_expert, tile_active, tile_src, xs, w1, w3, w2)


def _combine_kernel(gstart_ref, nch_ref, off_ref, x_ref, rt_ref, gf_ref, y_ref, out_ref, buf_ref, sem,
                    *, final_norm):
    b = pl.program_id(0)
    slot = b % 2

    def fetch(blk, s):
        buf_ref[s] = jnp.zeros(buf_ref.shape[1:], BF16)
        _segment_starts(blk, gstart_ref, nch_ref, off_ref, y_ref, buf_ref.at[s], sem.at[s], to_hbm=False)

    @pl.when(b == 0)
    def _():
        fetch(b, slot)

    @pl.when(b + 1 < pl.num_programs(0))
    def _():
        fetch(b + 1, 1 - slot)

    _segment_waits(b, nch_ref, buf_ref.at[slot], sem.at[slot])
    rt = rt_ref[...]
    r = lax.broadcasted_iota(I32, (TM, SORT_ROWS), 1).astype(F32)
    ybuf = buf_ref[slot]
    y1 = _dot(jnp.where(r == rt[:, 0:1], 1.0, 0.0).astype(BF16), ybuf)
    y2 = _dot(jnp.where(r == rt[:, 1:2], 1.0, 0.0).astype(BF16), ybuf)
    x = x_ref[...] + rt[:, 2:3] * y1 + rt[:, 3:4] * y2
    out_ref[...] = _rms(x, gf_ref[...]) if final_norm else x


def _combine(gstart, nch, off, x2, rt, g_final, y, final_norm):
    t, d = x2.shape
    grid_spec = pltpu.PrefetchScalarGridSpec(
        num_scalar_prefetch=3,
        grid=(t // TM,),
        in_specs=[pl.BlockSpec((TM, d), lambda i, *_: (i, 0)),
                  pl.BlockSpec((TM, LANES), lambda i, *_: (i, 0)),
                  pl.BlockSpec((1, d), lambda i, *_: (0, 0)),
                  pl.BlockSpec(memory_space=pl.ANY)],
        out_specs=pl.BlockSpec((TM, d), lambda i, *_: (i, 0)),
        scratch_shapes=[pltpu.VMEM((2, SORT_ROWS, d), BF16), pltpu.SemaphoreType.DMA((2,))],
    )
    return pl.pallas_call(
        functools.partial(_combine_kernel, final_norm=final_norm),
        grid_spec=grid_spec,
        out_shape=jax.ShapeDtypeStruct((t, d), F32),
        compiler_params=_cparams(("arbitrary",)),
        name="moe_combine",
    )(gstart, nch, off, x2, rt, g_final, y)


def _norm_kernel(x_ref, g_ref, out_ref):
    out_ref[...] = _rms(x_ref[...], g_ref[...])


def _final_norm(x2, g):
    t, d = x2.shape
    row = pl.BlockSpec((TM, d), lambda i: (i, 0))
    return pl.pallas_call(
        _norm_kernel, grid=(t // TM,), in_specs=[row, _const_spec((1, d))], out_specs=row,
        out_shape=jax.ShapeDtypeStruct((t, d), F32), compiler_params=_cparams(("parallel",)),
        name="final_norm",
    )(x2, g)


def _tril_blocks(n, block, strict=False):
    r = jnp.arange(n)[:, None]
    c = jnp.arange(n)[None, :]
    keep = (c < r) if strict else (c <= r)
    keep = keep & ((r // block) == (c // block))
    return keep.astype(BF16)


def _sorted_rows(t):
    return ((2 * t + (t // TM) * N_EXPERTS * ROW_GRAN) // TMM + N_EXPERTS) * TMM


def _moe_layer(x2, g_ffn, w_router, w1, w3, w2, xs_zero, g_final, final_norm):
    t, d = x2.shape
    nb = t // TM
    wr_pad = jnp.pad(w_router, ((0, 0), (0, LANES - N_EXPERTS)))
    h2, rt, seg = _router(x2, g_ffn.reshape(1, d), wr_pad, _tril_blocks(TM, TM, strict=True))

    pc = seg[:, 0, :N_EXPERTS].astype(I32)
    off = seg[:, 1, :N_EXPERTS].astype(I32)
    region = ((jnp.sum(pc, axis=0) + TMM - 1) // TMM) * TMM
    ends = jnp.cumsum(region)
    gstart = (ends - region)[None, :] + jnp.cumsum(pc, axis=0) - pc
    n_tiles = _sorted_rows(t) // TMM
    tile_start = jnp.arange(n_tiles, dtype=I32) * TMM
    tile_expert = jnp.minimum(jnp.sum(tile_start[:, None] >= ends[None, :], axis=1), N_EXPERTS - 1).astype(I32)
    tile_active = (tile_start < ends[-1]).astype(I32)
    last_active = jnp.maximum(ends[-1] // TMM - 1, 0)
    tile_src = jnp.minimum(jnp.arange(n_tiles, dtype=I32), last_active).astype(I32)
    seg_args = (gstart.reshape(-1).astype(I32), (pc // ROW_GRAN).reshape(-1), off.reshape(-1))

    xs = _dispatch(*seg_args, h2, rt, xs_zero)
    y = _group_ffn(tile_expert, tile_active, tile_src, xs, w1, w3, w2)
    return _combine(*seg_args, x2, rt, g_final.reshape(1, d), y, final_norm)


def kernel(x, g_mix, w_in, b_f, conv_w, w_a2, b_a, g_gla, w_pa, w_pb, w_pc, w_gate, b_gate, w_o,
           g_ffn, ffn_w1, ffn_w3, ffn_w2, w_router, moe_w1, moe_w3, moe_w2, g_final):
    b, s, d = x.shape
    depth = g_mix.shape[0]
    t = b * s
    assert t % TM == 0 and s % TQ == 0 and TQ == TM and s % TL == 0 and TL % GLA_CHUNK == 0 and b % MIX_BATCH == 0
    x2 = x.reshape(t, d)
    tril_full = _tril_blocks(TM, TM)
    tril_chunk = _tril_blocks(TL, GLA_CHUNK)
    o_af = 3 * A_WIDTH
    o_b = o_af + A_HEADS
    o_c = o_b + 3 * B_WIDTH
    o_ca = o_c + 2 * C_KEY_WIDTH + C_VAL_WIDTH
    o_cr = o_ca + C_LOWRANK

    wq = w_in[:, :, 0:A_WIDTH] * (A_HEAD_DIM ** -0.5 * LOG2E)
    wbig = jnp.concatenate([wq, w_in[:, :, A_WIDTH:2 * A_WIDTH], w_in[:, :, o_b:o_c], w_in[:, :, o_c:o_ca],
                            w_in[:, :, o_cr:]], axis=2).astype(BF16)
    wvt = jnp.swapaxes(w_in[:, :, 2 * A_WIDTH:o_af], 1, 2).astype(BF16)
    wsmall = jnp.concatenate([w_in[:, :, o_af:o_b], w_in[:, :, o_ca:o_cr],
                              jnp.zeros((depth, d, LANES - A_HEADS - C_LOWRANK), F32)], axis=2).astype(BF16)

    done_final = False
    for l in range(depth):
        gm = g_mix[l].reshape(1, d)
        bf_pad = jnp.pad(b_f[l], (0, LANES - A_HEADS)).reshape(1, LANES)
        qp, kp, vt, conv_in, gla_in, small = _inproj(x2, gm, wbig, wvt, wsmall, l, bf_pad, tril_full, s // TM)
        if l == 0:
            stacks = (w_gate, w_pa, w_pb, w_pc, w_o, ffn_w1, ffn_w3)
            o_a, cast = _attention(qp, kp, vt, b, s, [a.reshape(-1, a.shape[-1]) for a in stacks])
            wg_b, wpa_b, wpb_b, wpc_b, wo_b, f1_b, f3_b = (c.reshape(a.shape) for c, a in zip(cast, stacks))
            f2_b = ffn_w2.astype(BF16)
        else:
            o_a, _ = _attention(qp, kp, vt, b, s)

        convw8 = jnp.pad(conv_w[l], ((0, 8 - conv_w.shape[1]), (0, 0)))
        wa2p = jnp.zeros((LANES, C_KEY_WIDTH), F32).at[A_HEADS:A_HEADS + C_LOWRANK].set(w_a2[l]).astype(BF16)
        out_c = _gla(gla_in, small, wa2p, b_a[l].reshape(1, -1), g_gla[l].reshape(1, -1), tril_chunk, b, s)

        x2 = _merge(x2, gm, o_a, conv_in, convw8, out_c, wg_b, b_gate[l].reshape(1, -1),
                    wpa_b, wpb_b, wpc_b, wo_b, l, s // TM)

        i = l // 2
        if l % 2 == 0:
            prep = (moe_w1[i], moe_w3[i], moe_w2[i], _sorted_rows(t)) if l + 1 < depth else None
            x2, moe_ready = _dense_ffn(x2, g_ffn[l].reshape(1, d), f1_b, f3_b, f2_b, i, prep)
        else:
            done_final = l == depth - 1
            x2 = _moe_layer(x2, g_ffn[l], w_router[i], *moe_ready, g_final, done_final)
    if not done_final:
        x2 = _final_norm(x2, g_final.reshape(1, d))
    return x2.reshape(b, s, d)
```
